```python
import math
import jax, jax.numpy as jnp
from jax import lax
import numpy as np

D_MODEL = 2048
BATCH = 8
SEQ = 2048
DEPTH = 2

HEAD_DIM = 128
ATTN_WIDTH = D_MODEL // 2
ATTN_HEADS = ATTN_WIDTH // HEAD_DIM
POOL_WIDTH = D_MODEL // 4
POOL_WINDOWS = (2, 4, 8, 16)
POOL_GROUPS = len(POOL_WINDOWS)
POOL_GROUP_DIM = POOL_WIDTH // POOL_GROUPS
CONV_WIDTH = D_MODEL // 4
CONV_K = 3
D_MIX = ATTN_WIDTH + POOL_WIDTH + CONV_WIDTH
IN_PROJ_SPLITS = (ATTN_WIDTH, ATTN_WIDTH, ATTN_WIDTH, POOL_WIDTH, CONV_WIDTH, CONV_WIDTH, CONV_WIDTH)
D_IN_PROJ = sum(IN_PROJ_SPLITS)

MOBA_BLOCK = 256
MOBA_TOPK = 3
QUERY_CHUNK = 16
ATTN_SCALE = HEAD_DIM ** -0.5

REL_BUCKETS = 32
REL_MAX_DISTANCE = 128

N_EXPERTS = 32
TOPK_EXPERTS = 4
D_EXPERT = D_MODEL // 2
SWIGLU_LIMIT = 7.0
SWIGLU_ALPHA = 1.702
EXPERT_ROW_BLOCK = 512

N_MOD = 6
RMS_EPS = 1e-6
NEG_INF = -1e30

kernel_name = "hybrid_moba_pool_conv_moe_block"


def _normal(k, shape, scale):
    return jax.random.normal(k, shape, jnp.float32) * scale


def rms_norm(x, g):
    xf = x.astype(jnp.float32)
    y = xf * lax.rsqrt(jnp.mean(xf * xf, axis=-1, keepdims=True) + RMS_EPS)
    return (y * g.astype(jnp.float32)).astype(x.dtype)


def rel_bucket(rel):
    n = jnp.maximum(-rel, 0)
    max_exact = REL_BUCKETS // 2
    nf = jnp.maximum(n, max_exact).astype(jnp.float32)
    large = max_exact + (jnp.log(nf / max_exact) / math.log(REL_MAX_DISTANCE / max_exact)
                         * (REL_BUCKETS - max_exact)).astype(jnp.int32)
    large = jnp.minimum(large, REL_BUCKETS - 1)
    return jnp.where(n < max_exact, n, large)


def moba_attention(q, k, v, rel_bias):
    B, S, _ = q.shape
    H, Dh = ATTN_HEADS, HEAD_DIM
    q = q.reshape(B, S, H, Dh).transpose(0, 2, 1, 3)
    k = k.reshape(B, S, H, Dh).transpose(0, 2, 1, 3)
    v = v.reshape(B, S, H, Dh).transpose(0, 2, 1, 3)
    n_blk = -(-S // MOBA_BLOCK)
    pad = n_blk * MOBA_BLOCK - S
    kb = jnp.pad(k, ((0, 0), (0, 0), (0, pad), (0, 0))).reshape(B, H, n_blk, MOBA_BLOCK, Dh)
    vb = jnp.pad(v, ((0, 0), (0, 0), (0, pad), (0, 0))).reshape(B, H, n_blk, MOBA_BLOCK, Dh)

    k_mean = jnp.mean(kb.astype(jnp.float32), axis=3).astype(q.dtype)
    gate = jnp.einsum('bhsd,bhnd->bhsn', q, k_mean, preferred_element_type=jnp.float32)
    q_blk = jnp.arange(S) // MOBA_BLOCK
    past = jnp.arange(n_blk)[None, :] < q_blk[:, None]
    gate = jnp.where(past, gate, NEG_INF)
    kk = min(MOBA_TOPK, n_blk)
    _, sel = lax.top_k(gate, kk)
    sel_ok = sel < q_blk[:, None]

    n_chunk = S // QUERY_CHUNK

    def chunks(a):
        a = a.reshape(B, H, n_chunk, QUERY_CHUNK, *a.shape[3:])
        return jnp.moveaxis(a, 2, 0)

    starts = jnp.arange(n_chunk, dtype=jnp.int32) * QUERY_CHUNK
    table_h = rel_bias.T.astype(jnp.float32)
    head_ix = jnp.arange(H)[:, None, None, None]
    gather_blocks = jax.vmap(jax.vmap(lambda blocks, ix: blocks[ix]))

    def body(args):
        qc, sc, okc, start = args
        qpos = start + jnp.arange(QUERY_CHUNK)
        own = start // MOBA_BLOCK
        k_own = lax.dynamic_index_in_dim(kb, own, axis=2, keepdims=False)
        v_own = lax.dynamic_index_in_dim(vb, own, axis=2, keepdims=False)
        rel_own = own * MOBA_BLOCK + jnp.arange(MOBA_BLOCK)[None, :] - qpos[:, None]
        l_own = (jnp.einsum('bhqd,bhkd->bhqk', qc, k_own, preferred_element_type=jnp.float32) * ATTN_SCALE
                 + table_h[:, rel_bucket(rel_own)])
        l_own = jnp.where(rel_own <= 0, l_own, NEG_INF)
        k_sel = gather_blocks(kb, sc)
        v_sel = gather_blocks(vb, sc)
        rel_sel = sc[..., None] * MOBA_BLOCK + jnp.arange(MOBA_BLOCK) - qpos[:, None, None]
        l_sel = (jnp.einsum('bhqd,bhqnkd->bhqnk', qc, k_sel, preferred_element_type=jnp.float32) * ATTN_SCALE
                 + table_h[head_ix, rel_bucket(rel_sel)])
        l_sel = jnp.where(okc[..., None], l_sel, NEG_INF)
        n_sel = kk * MOBA_BLOCK
        logits = jnp.concatenate([l_sel.reshape(B, H, QUERY_CHUNK, n_sel), l_own], axis=-1)
        p = jax.nn.softmax(logits, axis=-1).astype(v.dtype)
        p_sel = p[..., :n_sel].reshape(B, H, QUERY_CHUNK, kk, MOBA_BLOCK)
        p_own = p[..., n_sel:]
        return (jnp.einsum('bhqnk,bhqnkd->bhqd', p_sel, v_sel)
                + jnp.einsum('bhqk,bhkd->bhqd', p_own, v_own))

    out = lax.map(body, (chunks(q), chunks(sel), chunks(sel_ok), starts))
    out = jnp.moveaxis(out, 0, 2).reshape(B, H, S, Dh)
    return out.transpose(0, 2, 1, 3).reshape(B, S, H * Dh)


def pool_mixer(u, w_pool, pool_scale):
    B, S, _ = u.shape
    ug = u.astype(jnp.float32).reshape(B, S, POOL_GROUPS, POOL_GROUP_DIM)
    cs = jnp.concatenate([jnp.zeros((B, 1, POOL_GROUPS, POOL_GROUP_DIM), jnp.float32),
                          jnp.cumsum(ug, axis=1)], axis=1)
    t = jnp.arange(1, S + 1)
    pooled = []
    for gi, w in enumerate(POOL_WINDOWS):
        csg = cs[:, :, gi]
        start = jnp.maximum(t - w, 0)
        win_sum = csg[:, 1:] - jnp.take(csg, start, axis=1)
        count = jnp.minimum(t, w).astype(jnp.float32)
        pooled.append(win_sum / count[None, :, None])
    pooled = jnp.stack(pooled, axis=2)
    d = (pooled - ug).astype(u.dtype)
    y = jnp.einsum('bsgc,gcd->bsgd', d, w_pool).reshape(B, S, POOL_WIDTH)
    return y * pool_scale


def conv_mixer(gate_b, gate_c, h_in, conv_w):
    u = gate_c * h_in
    conv = lax.conv_general_dilated(u, conv_w[:, None, :], window_strides=(1,),
                                    padding=[(CONV_K - 1, 0)],
                                    dimension_numbers=('NWC', 'WIO', 'NWC'),
                                    feature_group_count=CONV_WIDTH)
    return gate_b * conv


def clamped_swiglu(g, u):
    g = jnp.minimum(g, SWIGLU_LIMIT)
    u = jnp.clip(u, -SWIGLU_LIMIT, SWIGLU_LIMIT)
    return g * jax.nn.sigmoid(SWIGLU_ALPHA * g) * (u + 1)


def moe_ffn(h, router_w, router_b, w_gate, b_gate, w_up, b_up, w_down, b_down):
    B, S, D = h.shape
    T = B * S
    xt = h.reshape(T, D)
    logits = (xt @ router_w + router_b).astype(jnp.float32)
    top_val, top_idx = lax.top_k(logits, TOPK_EXPERTS)
    gates = jax.nn.softmax(top_val, axis=-1)
    M = T * TOPK_EXPERTS
    e_flat = top_idx.reshape(M)
    order = jnp.argsort(e_flat)
    e_sorted = e_flat[order]
    tok_sorted = order // TOPK_EXPERTS
    counts = jnp.bincount(e_flat, length=N_EXPERTS)
    padded = ((counts + EXPERT_ROW_BLOCK - 1) // EXPERT_ROW_BLOCK) * EXPERT_ROW_BLOCK
    pad_end = jnp.cumsum(padded)
    pad_start = pad_end - padded
    grp_start = jnp.cumsum(counts) - counts
    dest = pad_start[e_sorted] + (jnp.arange(M) - grp_start[e_sorted])
    n_rb = -(-M // EXPERT_ROW_BLOCK) + N_EXPERTS
    P = n_rb * EXPERT_ROW_BLOCK
    xp = jnp.zeros((P, D), xt.dtype).at[dest].set(xt[tok_sorted])
    rb_expert = jnp.minimum(jnp.searchsorted(pad_end, jnp.arange(n_rb) * EXPERT_ROW_BLOCK, side='right'),
                            N_EXPERTS - 1)

    def expert_rows(args):
        xb, e = args
        g = xb @ w_gate[e] + b_gate[e]
        u = xb @ w_up[e] + b_up[e]
        return clamped_swiglu(g, u) @ w_down[e] + b_down[e]

    yp = lax.map(expert_rows, (xp.reshape(n_rb, EXPERT_ROW_BLOCK, D), rb_expert)).reshape(P, D)
    y_sorted = yp[dest] * gates.reshape(M)[order][:, None].astype(yp.dtype)
    y = jax.ops.segment_sum(y_sorted, tok_sorted, num_segments=T)
    return y.reshape(B, S, D)


def setup_inputs(seed: int = 0) -> dict:
    key = jax.random.key(seed)
    ks = jax.random.split(key, 24)
    L = DEPTH
    gain = lambda k: 1.0 + _normal(k, (L, D_MODEL), 0.1)
    return {
        "x": _normal(ks[0], (BATCH, SEQ, D_MODEL), 1.0),
        "c": _normal(ks[1], (BATCH, D_MODEL), 1.0),
        "w_ada": _normal(ks[2], (L, D_MODEL, N_MOD * D_MODEL), 0.5 * D_MODEL ** -0.5),
        "b_ada": _normal(ks[3], (L, N_MOD * D_MODEL), 0.01),
        "g_pre_mix": gain(ks[4]),
        "g_post_mix": gain(ks[5]),
        "g_pre_ffn": gain(ks[6]),
        "g_post_ffn": gain(ks[7]),
        "w_in": _normal(ks[8], (L, D_MODEL, D_IN_PROJ), D_MODEL ** -0.5),
        "w_out": _normal(ks[9], (L, D_MIX, D_MODEL), D_MIX ** -0.5),
        "pool_w": _normal(ks[10], (L, POOL_GROUPS, POOL_GROUP_DIM, POOL_GROUP_DIM), POOL_GROUP_DIM ** -0.5),
        "pool_scale": 1.0 + _normal(ks[11], (L, POOL_WIDTH), 0.1),
        "conv_w": _normal(ks[12], (L, CONV_K, CONV_WIDTH), CONV_K ** -0.5),
        "rel_bias": _normal(ks[13], (REL_BUCKETS, ATTN_HEADS), 0.5),
        "router_w": _normal(ks[14], (L, D_MODEL, N_EXPERTS), D_MODEL ** -0.5),
        "router_b": _normal(ks[15], (L, N_EXPERTS), 0.01),
        "w_gate": _normal(ks[16], (L, N_EXPERTS, D_MODEL, D_EXPERT), D_MODEL ** -0.5),
        "b_gate": _normal(ks[17], (L, N_EXPERTS, D_EXPERT), 0.01),
        "w_up": _normal(ks[18], (L, N_EXPERTS, D_MODEL, D_EXPERT), D_MODEL ** -0.5),
        "b_up": _normal(ks[19], (L, N_EXPERTS, D_EXPERT), 0.01),
        "w_down": _normal(ks[20], (L, N_EXPERTS, D_EXPERT, D_MODEL), D_EXPERT ** -0.5),
        "b_down": _normal(ks[21], (L, N_EXPERTS, D_MODEL), 0.01),
    }


def reference(x, c, w_ada, b_ada, g_pre_mix, g_post_mix, g_pre_ffn, g_post_ffn, w_in, w_out,
              pool_w, pool_scale, conv_w, rel_bias, router_w, router_b,
              w_gate, b_gate, w_up, b_up, w_down, b_down):
    c_act = jax.nn.silu(c)
    split_points = np.cumsum(IN_PROJ_SPLITS)[:-1].tolist()
    for l in range(DEPTH):
        mod = c_act @ w_ada[l] + b_ada[l]
        sh_m, sc_m, gt_m, sh_f, sc_f, gt_f = jnp.split(mod[:, None, :], N_MOD, axis=-1)

        h = rms_norm(x, g_pre_mix[l]) * (1 + sc_m) + sh_m
        proj = h @ w_in[l]
        q, k, v, u_pool, gate_b, gate_c, h_conv = jnp.split(proj, split_points, axis=-1)
        y_att = moba_attention(q, k, v, rel_bias)
        y_pool = pool_mixer(u_pool, pool_w[l], pool_scale[l])
        y_conv = conv_mixer(gate_b, gate_c, h_conv, conv_w[l])
        mix = jnp.concatenate([y_att, y_pool, y_conv], axis=-1) @ w_out[l]
        x = x + gt_m * rms_norm(mix, g_post_mix[l])

        h = rms_norm(x, g_pre_ffn[l]) * (1 + sc_f) + sh_f
        ffn = moe_ffn(h, router_w[l], router_b[l], w_gate[l], b_gate[l], w_up[l], b_up[l],
                      w_down[l], b_down[l])
        x = x + gt_f * rms_norm(ffn, g_post_ffn[l])
    return x
```

```python
import functools
import math

import numpy as np
import jax
import jax.numpy as jnp
from jax import lax
from jax.experimental import pallas as pl
from jax.experimental.pallas import tpu as pltpu

F32 = jnp.float32
BF16 = jnp.bfloat16

LANES = 128
HEAD_DIM = 128
MOBA_BLOCK = 256
MOBA_TOPK = 3
REL_BUCKETS = 32
REL_MAX_DISTANCE = 128
POOL_WINDOWS = (2, 4, 8, 16)
GROUP_DIM = 128
CONV_K = 3
TOPK_EXPERTS = 4
SWIGLU_LIMIT = 7.0
SWIGLU_ALPHA = 1.702
N_MOD = 6
RMS_EPS = 1e-6
NEG_INF = -1e30
ATTN_SCALE = HEAD_DIM ** -0.5

VMEM_LIMIT = 56 * 1024 * 1024


def _cparams(sem):
    return pltpu.CompilerParams(dimension_semantics=sem, vmem_limit_bytes=VMEM_LIMIT)


def _rms(x):
    return x * lax.rsqrt(jnp.mean(x * x, axis=-1, keepdims=True) + RMS_EPS)


def _ada_kernel(c_ref, w_ref, b_ref, o_ref):
    c = c_ref[...]
    ca = (c * jax.nn.sigmoid(c)).astype(BF16)
    o_ref[0] = jnp.dot(ca, w_ref[0].astype(BF16), preferred_element_type=F32) + b_ref[0]


def _ada(c, w_ada, b_ada):
    L, D, N = w_ada.shape
    B = c.shape[0]
    tn = 1024
    return pl.pallas_call(
        _ada_kernel,
        out_shape=jax.ShapeDtypeStruct((L, B, N), F32),
        grid=(L, N // tn),
        in_specs=[
            pl.BlockSpec((B, D), lambda l, j: (0, 0)),
            pl.BlockSpec((1, D, tn), lambda l, j: (l, 0, j)),
            pl.BlockSpec((1, 1, tn), lambda l, j: (l, 0, j)),
        ],
        out_specs=pl.BlockSpec((1, B, tn), lambda l, j: (l, 0, j)),
        compiler_params=_cparams(("parallel", "parallel")),
        name="ada_mod",
    )(c, w_ada, b_ada.reshape(L, 1, N))


def _inproj_kernel(x_ref, g_ref, sc_ref, sh_ref, w_ref, qkv_ref, rest_ref, h_scr, *, n_qkv):
    j = pl.program_id(1)

    @pl.when(j == 0)
    def _():
        h = _rms(x_ref[...]) * g_ref[...]
        h = h * (1.0 + sc_ref[0]) + sh_ref[0]
        h_scr[...] = h.astype(BF16)

    acc = jnp.dot(h_scr[...], w_ref[...], preferred_element_type=F32)

    @pl.when(j < n_qkv)
    def _():
        qkv_ref[...] = acc.astype(BF16)

    @pl.when(j >= n_qkv)
    def _():
        rest_ref[...] = acc


def _inproj(x2, g, sc, sh, w_bf, seq, attn_w):
    T, D = x2.shape
    N = w_bf.shape[1]
    tm, tn = 1024, 512
    n_qkv = 3 * attn_w // tn
    n_rest = (N - 3 * attn_w) // tn
    per_b = seq // tm
    return pl.pallas_call(
        functools.partial(_inproj_kernel, n_qkv=n_qkv),
        out_shape=(jax.ShapeDtypeStruct((T, 3 * attn_w), BF16),
                   jax.ShapeDtypeStruct((T, N - 3 * attn_w), F32)),
        grid=(T // tm, n_qkv + n_rest),
        in_specs=[
            pl.BlockSpec((tm, D), lambda i, j: (i, 0)),
            pl.BlockSpec((1, D), lambda i, j: (0, 0)),
            pl.BlockSpec((1, 1, D), lambda i, j: (i // per_b, 0, 0)),
            pl.BlockSpec((1, 1, D), lambda i, j: (i // per_b, 0, 0)),
            pl.BlockSpec((D, tn), lambda i, j: (0, j)),
        ],
        out_specs=(
            pl.BlockSpec((tm, tn), lambda i, j: (i, jnp.minimum(j, n_qkv - 1))),
            pl.BlockSpec((tm, tn), lambda i, j: (i, jnp.maximum(j - n_qkv, 0))),
        ),
        scratch_shapes=[pltpu.VMEM((tm, D), BF16)],
        compiler_params=_cparams(("parallel", "arbitrary")),
        name="mixer_in_proj",
    )(x2, g, sc, sh, w_bf)


def _rel_bucket_np(n):
    n = np.maximum(n, 0)
    max_exact = REL_BUCKETS // 2
    nf = np.maximum(n, max_exact).astype(np.float32)
    large = max_exact + (np.log(nf / np.float32(max_exact)) / np.float32(math.log(REL_MAX_DISTANCE / max_exact))
                         * np.float32(REL_BUCKETS - max_exact)).astype(np.int32)
    large = np.minimum(large, REL_BUCKETS - 1)
    return np.where(n < max_exact, n, large).astype(np.int32)


def _bucket_tables(seq):
    qi = np.arange(MOBA_BLOCK)[:, None]
    ki = np.arange(MOBA_BLOCK)[None, :]
    own = np.where(ki <= qi, _rel_bucket_np(qi - ki), -1)
    prev = _rel_bucket_np(qi - ki + MOBA_BLOCK)
    far = _rel_bucket_np(np.arange(MOBA_BLOCK + 1, max(seq, MOBA_BLOCK + 2)))
    assert np.all(far == far[0])
    return np.stack([own, prev]).astype(np.int32), int(far[0])


def _attn_kernel(tab_ref, q_ref, k_ref, v_ref, bkt_ref, o_ref, bias_scr, *, nblk, far_bucket):
    h = pl.program_id(0)
    b = pl.program_id(1)
    L = MOBA_BLOCK

    @pl.when(b == 0)
    def _():
        for m in range(2):
            bk = bkt_ref[m]
            acc = jnp.full((L, L), NEG_INF, F32)
            for r in range(REL_BUCKETS):
                acc = jnp.where(bk == r, tab_ref[h, r], acc)
            bias_scr[m] = acc

    far_bias = tab_ref[h, far_bucket]
    q = q_ref[...]
    k = k_ref[...]
    v = v_ref[...]
    dn = (((1,), (1,)), ((), ()))

    kmean = jnp.mean(k.astype(F32).reshape(nblk, L, HEAD_DIM), axis=1)
    kmean = jnp.concatenate([kmean, jnp.zeros((LANES - nblk, HEAD_DIM), F32)], axis=0).astype(BF16)
    gate = lax.dot_general(q, kmean, dn, preferred_element_type=F32)
    lane = lax.broadcasted_iota(jnp.int32, (L, LANES), 1)

    for a in range(nblk):
        qa = q[a * L:(a + 1) * L]
        n = (a + 1) * L
        s = lax.dot_general(qa, k[:n], dn, preferred_element_type=F32) * ATTN_SCALE
        sel = None
        if a > MOBA_TOPK:
            ga = jnp.where(lane < a, gate[a * L:(a + 1) * L], NEG_INF)
            cnt = jnp.zeros((L, LANES), F32)
            for i in range(a):
                gi = ga[:, i:i + 1]
                ge = jnp.where(gi >= ga, 1.0, 0.0)
                gt = jnp.where(gi > ga, 1.0, 0.0)
                cnt = cnt + jnp.where(lane > i, ge, gt)
            sel = jnp.where((cnt < float(MOBA_TOPK)) & (lane < a), 1.0, 0.0)
        pieces = []
        for j in range(a + 1):
            sj = s[:, j * L:(j + 1) * L]
            if j == a:
                sj = sj + bias_scr[0]
            else:
                sj = sj + (bias_scr[1] if j == a - 1 else far_bias)
                if sel is not None:
                    sj = jnp.where(sel[:, j:j + 1] > 0.5, sj, NEG_INF)
            pieces.append(sj)
        s = pieces[0] if a == 0 else jnp.concatenate(pieces, axis=1)
        m = jnp.max(s, axis=-1, keepdims=True)
        p = jnp.exp(s - m)
        l = jnp.sum(p, axis=-1, keepdims=True)
        o = jnp.dot(p.astype(BF16), v[:n], preferred_element_type=F32) / l
        o_ref[a * L:(a + 1) * L, :] = o.astype(o_ref.dtype)


def _attention(qkv, rel_bias, batch, seq, heads):
    T = qkv.shape[0]
    nblk = seq // MOBA_BLOCK
    bkt, far_bucket = _bucket_tables(seq)
    table = rel_bias.T.astype(F32)
    return pl.pallas_call(
        functools.partial(_attn_kernel, nblk=nblk, far_bucket=far_bucket),
        out_shape=jax.ShapeDtypeStruct((T, heads * HEAD_DIM), BF16),
        grid=(heads, batch),
        in_specs=[
            pl.BlockSpec(memory_space=pltpu.SMEM),
            pl.BlockSpec((seq, HEAD_DIM), lambda h, b: (b, h)),
            pl.BlockSpec((seq, HEAD_DIM), lambda h, b: (b, heads + h)),
            pl.BlockSpec((seq, HEAD_DIM), lambda h, b: (b, 2 * heads + h)),
            pl.BlockSpec((2, MOBA_BLOCK, MOBA_BLOCK), lambda h, b: (0, 0, 0)),
        ],
        out_specs=pl.BlockSpec((seq, HEAD_DIM), lambda h, b: (b, h)),
        scratch_shapes=[pltpu.VMEM((2, MOBA_BLOCK, MOBA_BLOCK), F32)],
        compiler_params=_cparams(("arbitrary", "arbitrary")),
        name="moba_attention",
    )(table, qkv, qkv, qkv, jnp.asarray(bkt))


def _poolconv_kernel(u_ref, gb_ref, gc_ref, hc_ref, pw_ref, ps_ref, cw_ref, o_ref):
    S = u_ref.shape[0]
    G = len(POOL_WINDOWS)
    row = lax.broadcasted_iota(jnp.int32, (S, GROUP_DIM), 0)

    def shift(x, k):
        return jnp.where(row >= k, pltpu.roll(x, k, 0), 0.0)

    for gi, w in enumerate(POOL_WINDOWS):
        cs = slice(gi * GROUP_DIM, (gi + 1) * GROUP_DIM)
        ug = u_ref[:, cs]
        win = ug
        k = 1
        while k < w:
            win = win + shift(win, k)
            k *= 2
        count = jnp.minimum(row + 1, w).astype(F32)
        d = win / count - ug
        y = jnp.dot(d.astype(BF16), pw_ref[gi].astype(BF16), preferred_element_type=F32)
        o_ref[:, cs] = (y * ps_ref[:, cs]).astype(o_ref.dtype)

        uu = gc_ref[:, cs] * hc_ref[:, cs]
        conv = (cw_ref[0:1, cs] * shift(uu, 2) + cw_ref[1:2, cs] * shift(uu, 1)
                + cw_ref[2:3, cs] * uu)
        oc = slice((G + gi) * GROUP_DIM, (G + gi + 1) * GROUP_DIM)
        o_ref[:, oc] = (gb_ref[:, cs] * conv).astype(o_ref.dtype)


def _poolconv(rest, pool_w, pool_scale, conv_w, batch, seq):
    T = rest.shape[0]
    W = len(POOL_WINDOWS) * GROUP_DIM
    spec = lambda c: pl.BlockSpec((seq, W), lambda b: (b, c))
    return pl.pallas_call(
        _poolconv_kernel,
        out_shape=jax.ShapeDtypeStruct((T, 2 * W), BF16),
        grid=(batch,),
        in_specs=[
            spec(0), spec(1), spec(2), spec(3),
            pl.BlockSpec(pool_w.shape, lambda b: (0, 0, 0)),
            pl.BlockSpec((1, W), lambda b: (0, 0)),
            pl.BlockSpec((CONV_K, W), lambda b: (0, 0)),
        ],
        out_specs=pl.BlockSpec((seq, 2 * W), lambda b: (b, 0)),
        compiler_params=_cparams(("parallel",)),
        name="pool_conv_mixers",
    )(rest, rest, rest, rest, pool_w, pool_scale.reshape(1, W), conv_w)


def _outproj_kernel(att_ref, pc_ref, x_ref, wa_ref, wp_ref, gpost_ref, gtm_ref, gpre_ref, scf_ref, shf_ref,
                    rwh_ref, rwl_ref, rb_ref,
                    x1_ref, h2_ref, idx_ref, gate_ref, rank_ref, cnt_ref, carry_scr):
    i = pl.program_id(0)
    tm = x_ref.shape[0]

    @pl.when(i == 0)
    def _():
        carry_scr[...] = jnp.zeros_like(carry_scr)

    mix = (jnp.dot(att_ref[...], wa_ref[...], preferred_element_type=F32)
           + jnp.dot(pc_ref[...], wp_ref[...], preferred_element_type=F32))
    x1 = x_ref[...] + gtm_ref[0] * (_rms(mix) * gpost_ref[...])
    x1_ref[...] = x1
    h2 = (_rms(x1) * gpre_ref[...]) * (1.0 + scf_ref[0]) + shf_ref[0]
    h2_ref[...] = h2

    hi = h2.astype(BF16)
    lo = (h2 - hi.astype(F32)).astype(BF16)
    logits = (jnp.dot(hi, rwh_ref[...], preferred_element_type=F32)
              + jnp.dot(lo, rwh_ref[...], preferred_element_type=F32)
              + jnp.dot(hi, rwl_ref[...], preferred_element_type=F32)) + rb_ref[...]

    lane = lax.broadcasted_iota(jnp.int32, (tm, LANES), 1)
    work = logits
    mem = jnp.zeros((tm, LANES), F32)
    vals, hots = [], []
    idx_out = jnp.zeros((tm, LANES), jnp.int32)
    for kk in range(TOPK_EXPERTS):
        m = jnp.max(work, axis=-1, keepdims=True)
        ik = jnp.min(jnp.where(work == m, lane, LANES), axis=-1, keepdims=True)
        hot = lane == ik
        vals.append(m)
        hots.append(hot)
        idx_out = jnp.where(lane == kk, ik, idx_out)
        mem = jnp.where(hot, 1.0, mem)
        work = jnp.where(hot, -jnp.inf, work)
    idx_ref[...] = idx_out

    es = [jnp.exp(vk - vals[0]) for vk in vals]
    denom = es[0]
    for e in es[1:]:
        denom = denom + e
    gates = jnp.zeros((tm, LANES), F32)
    for kk in range(TOPK_EXPERTS):
        gates = jnp.where(lane == kk, es[kk] / denom, gates)
    gate_ref[...] = gates

    r_i = lax.broadcasted_iota(jnp.int32, (tm, tm), 0)
    c_i = lax.broadcasted_iota(jnp.int32, (tm, tm), 1)
    tri = jnp.where(c_i < r_i, 1.0, 0.0).astype(BF16)
    before = jnp.dot(tri, mem.astype(BF16), preferred_element_type=F32) + carry_scr[0:1, :]
    ranks = jnp.zeros((tm, LANES), jnp.int32)
    for kk in range(TOPK_EXPERTS):
        rk = jnp.sum(jnp.where(hots[kk], before, 0.0), axis=-1, keepdims=True)
        ranks = jnp.where(lane == kk, rk.astype(jnp.int32), ranks)
    rank_ref[...] = ranks
    carry = carry_scr[...] + jnp.sum(mem, axis=0, keepdims=True)
    carry_scr[...] = carry
    cnt_ref[...] = carry


def _outproj(att, pc, x2, wa, wp, gpost, gtm, gpre, scf, shf, rwh, rwl, rb, seq):
    T, D = x2.shape
    tm = 256
    per_b = seq // tm
    row = lambda w: pl.BlockSpec((tm, w), lambda i: (i, 0))
    const = lambda shape: pl.BlockSpec(shape, lambda i: tuple(0 for _ in shape))
    perb = pl.BlockSpec((1, 1, D), lambda i: (i // per_b, 0, 0))
    return pl.pallas_call(
        _outproj_kernel,
        out_shape=(jax.ShapeDtypeStruct((T, D), F32), jax.ShapeDtypeStruct((T, D), F32),
                   jax.ShapeDtypeStruct((T, LANES), jnp.int32), jax.ShapeDtypeStruct((T, LANES), F32),
                   jax.ShapeDtypeStruct((T, LANES), jnp.int32), jax.ShapeDtypeStruct((8, LANES), F32)),
        grid=(T // tm,),
        in_specs=[row(att.shape[1]), row(pc.shape[1]), row(D), const(wa.shape), const(wp.shape),
                  const((1, D)), perb, const((1, D)), perb, perb,
                  const(rwh.shape), const(rwl.shape), const((1, LANES))],
        out_specs=(row(D), row(D), row(LANES), row(LANES), row(LANES), const((8, LANES))),
        scratch_shapes=[pltpu.VMEM((8, LANES), F32)],
        compiler_params=_cparams(("arbitrary",)),
        name="out_proj_router",
    )(att, pc, x2, wa, wp, gpost, gtm, gpre, scf, shf, rwh, rwl, rb)


def _moe_kernel(te_ref, rt_ref, nu_ref, h_hbm, wg_ref, bg_ref, wu_ref, bu_ref, wd_ref, bd_ref, o_ref,
                xbuf, sem):
    i = pl.program_id(0)
    tm = xbuf.shape[0]

    @pl.when(i < nu_ref[0])
    def _():
        base = i * tm

        def issue(r, carry):
            tok = rt_ref[base + r]
            pltpu.make_async_copy(h_hbm.at[pl.ds(tok, 1)], xbuf.at[pl.ds(r, 1)], sem).start()
            return carry

        lax.fori_loop(0, tm, issue, 0)
        pltpu.make_async_copy(h_hbm.at[pl.ds(0, tm)], xbuf, sem).wait()
        x = xbuf[...].astype(BF16)
        g = jnp.dot(x, wg_ref[0], preferred_element_type=F32) + bg_ref[0]
        u = jnp.dot(x, wu_ref[0], preferred_element_type=F32) + bu_ref[0]
        g = jnp.minimum(g, SWIGLU_LIMIT)
        u = jnp.clip(u, -SWIGLU_LIMIT, SWIGLU_LIMIT)
        act = g * jax.nn.sigmoid(SWIGLU_ALPHA * g) * (u + 1.0)
        o_ref[...] = jnp.dot(act.astype(BF16), wd_ref[0], preferred_element_type=F32) + bd_ref[0]

    @pl.when(i >= nu_ref[0])
    def _():
        o_ref[...] = jnp.zeros_like(o_ref)


def _moe(tile_expert, row_token, n_used, h2, wg, bg, wu, bu, wd, bd, tm):
    T, D = h2.shape
    E, _, Fe = wg.shape
    n_tiles = tile_expert.shape[0]
    wspec = lambda shape: pl.BlockSpec((1,) + shape, lambda i, te, rt, nu: (te[i], 0, 0))
    return pl.pallas_call(
        _moe_kernel,
        out_shape=jax.ShapeDtypeStruct((n_tiles * tm, D), F32),
        grid_spec=pltpu.PrefetchScalarGridSpec(
            num_scalar_prefetch=3,
            grid=(n_tiles,),
            in_specs=[pl.BlockSpec(memory_space=pl.ANY),
                      wspec((D, Fe)), wspec((1, Fe)), wspec((D, Fe)), wspec((1, Fe)),
                      wspec((Fe, D)), wspec((1, D))],
            out_specs=pl.BlockSpec((tm, D), lambda i, te, rt, nu: (i, 0)),
            scratch_shapes=[pltpu.VMEM((tm, D), F32), pltpu.SemaphoreType.DMA],
        ),
        compiler_params=_cparams(("arbitrary",)),
        name="moe_experts",
    )(tile_expert, row_token, n_used, h2, wg, bg.reshape(E, 1, Fe), wu, bu.reshape(E, 1, Fe),
      wd, bd.reshape(E, 1, D))


def _combine_kernel(dest_ref, yp_hbm, gate_ref, x_ref, gpost_ref, gtf_ref, o_ref, buf, sem):
    i = pl.program_id(0)
    tc = x_ref.shape[0]
    base = i * tc * TOPK_EXPERTS

    def issue(r, carry):
        for kk in range(TOPK_EXPERTS):
            d = dest_ref[base + r * TOPK_EXPERTS + kk]
            pltpu.make_async_copy(yp_hbm.at[pl.ds(d, 1)], buf.at[kk, pl.ds(r, 1)], sem).start()
        return carry

    lax.fori_loop(0, tc, issue, 0)
    for kk in range(TOPK_EXPERTS):
        pltpu.make_async_copy(yp_hbm.at[pl.ds(0, tc)], buf.at[kk], sem).wait()
    gates = gate_ref[...]
    y = gates[:, 0:1] * buf[0]
    for kk in range(1, TOPK_EXPERTS):
        y = y + gates[:, kk:kk + 1] * buf[kk]
    o_ref[...] = x_ref[...] + gtf_ref[0] * (_rms(y) * gpost_ref[...])


def _combine(dest, yp, gates, x1, gpost, gtf, seq):
    T, D = x1.shape
    tc = 256
    per_b = seq // tc
    return pl.pallas_call(
        _combine_kernel,
        out_shape=jax.ShapeDtypeStruct((T, D), F32),
        grid_spec=pltpu.PrefetchScalarGridSpec(
            num_scalar_prefetch=1,
            grid=(T // tc,),
            in_specs=[pl.BlockSpec(memory_space=pl.ANY),
                      pl.BlockSpec((tc, LANES), lambda i, d: (i, 0)),
                      pl.BlockSpec((tc, D), lambda i, d: (i, 0)),
                      pl.BlockSpec((1, D), lambda i, d: (0, 0)),
                      pl.BlockSpec((1, 1, D), lambda i, d: (i // per_b, 0, 0))],
            out_specs=pl.BlockSpec((tc, D), lambda i, d: (i, 0)),
            scratch_shapes=[pltpu.VMEM((TOPK_EXPERTS, tc, D), F32), pltpu.SemaphoreType.DMA],
        ),
        compiler_params=_cparams(("arbitrary",)),
        name="moe_combine",
    )(dest, yp, gates, x1, gpost, gtf)


def _routing_tables(idx, rank, cnt, n_experts, tm):
    T = idx.shape[0]
    e_idx = idx[:, :TOPK_EXPERTS]
    counts = cnt[0, :n_experts].astype(jnp.int32)
    padded = ((counts + tm - 1) // tm) * tm
    pad_end = jnp.cumsum(padded)
    pad_start = pad_end - padded
    dest = (pad_start[e_idx] + rank[:, :TOPK_EXPERTS]).reshape(-1)
    n_tiles = (T * TOPK_EXPERTS) // tm + n_experts
    tok = jnp.repeat(jnp.arange(T, dtype=jnp.int32), TOPK_EXPERTS)
    row_token = jnp.zeros((n_tiles * tm,), jnp.int32).at[dest].set(tok)
    tile_expert = jnp.minimum(
        jnp.searchsorted(pad_end, jnp.arange(n_tiles, dtype=jnp.int32) * tm, side="right"),
        n_experts - 1).astype(jnp.int32)
    n_used = (pad_end[-1:] // tm).astype(jnp.int32)
    return dest.astype(jnp.int32), row_token, tile_expert, n_used


def kernel(x, c, w_ada, b_ada, g_pre_mix, g_post_mix, g_pre_ffn, g_post_ffn, w_in, w_out, pool_w, pool_scale,
           conv_w, rel_bias, router_w, router_b, w_gate, b_gate, w_up, b_up, w_down, b_down):
    B, S, D = x.shape
    L = w_ada.shape[0]
    T = B * S
    E = router_w.shape[-1]
    attn_w = D // 2
    heads = attn_w // HEAD_DIM
    moe_tm = 256

    mod = _ada(c, w_ada, b_ada)
    x2 = x.reshape(T, D)
    for l in range(L):
        sh_m, sc_m, gt_m, sh_f, sc_f, gt_f = [mod[l, :, i * D:(i + 1) * D].reshape(B, 1, D) for i in range(N_MOD)]
        row = lambda v: v.reshape(1, D)

        qkv, rest = _inproj(x2, row(g_pre_mix[l]), sc_m, sh_m, w_in[l].astype(BF16), S, attn_w)
        att = _attention(qkv, rel_bias, B, S, heads)
        pc = _poolconv(rest, pool_w[l], pool_scale[l], conv_w[l], B, S)

        wo = w_out[l].astype(BF16)
        rw = jnp.zeros((D, LANES), F32).at[:, :E].set(router_w[l])
        rwh = rw.astype(BF16)
        rwl = (rw - rwh.astype(F32)).astype(BF16)
        rb = jnp.full((1, LANES), NEG_INF, F32).at[0, :E].set(router_b[l])
        x1, h2, idx, gates, rank, cnt = _outproj(
            att, pc, x2, wo[:attn_w], wo[attn_w:], row(g_post_mix[l]), gt_m, row(g_pre_ffn[l]), sc_f, sh_f,
            rwh, rwl, rb, S)

        dest, row_token, tile_expert, n_used = _routing_tables(idx, rank, cnt, E, moe_tm)
        yp = _moe(tile_expert, row_token, n_used, h2,
                  w_gate[l].astype(BF16), b_gate[l], w_up[l].astype(BF16), b_up[l],
                  w_down[l].astype(BF16), b_down[l], moe_tm)
        x2 = _combine(dest, yp, gates, x1, row(g_post_ffn[l]), gt_f, S)
    return x2.reshape(B, S, D)
```

```python
import functools
import math

import numpy as np
import jax
import jax.numpy as jnp
from jax import lax
from jax.experimental import pallas as pl
from jax.experimental.pallas import tpu as pltpu

F32 = jnp.float32
BF16 = jnp.bfloat16

LANES = 128
HEAD_DIM = 128
MOBA_BLOCK = 256
MOBA_TOPK = 3
REL_BUCKETS = 32
REL_MAX_DISTANCE = 128
POOL_WINDOWS = (2, 4, 8, 16)
GROUP_DIM = 128
CONV_K = 3
TOPK_EXPERTS = 4
SWIGLU_LIMIT = 7.0
SWIGLU_ALPHA = 1.702
N_MOD = 6
RMS_EPS = 1e-6
NEG_INF = -1e30
ATTN_SCALE = HEAD_DIM ** -0.5

VMEM_LIMIT = 56 * 1024 * 1024


def _cparams(sem):
    return pltpu.CompilerParams(dimension_semantics=sem, vmem_limit_bytes=VMEM_LIMIT)


def _rms(x):
    return x * lax.rsqrt(jnp.mean(x * x, axis=-1, keepdims=True) + RMS_EPS)


def _ada_kernel(c_ref, w_ref, b_ref, o_ref):
    c = c_ref[...]
    ca = (c * jax.nn.sigmoid(c)).astype(BF16)
    o_ref[0] = jnp.dot(ca, w_ref[0].astype(BF16), preferred_element_type=F32) + b_ref[0]


def _ada(c, w_ada, b_ada):
    L, D, N = w_ada.shape
    B = c.shape[0]
    tn = 1024
    return pl.pallas_call(
        _ada_kernel,
        out_shape=jax.ShapeDtypeStruct((L, B, N), F32),
        grid=(L, N // tn),
        in_specs=[
            pl.BlockSpec((B, D), lambda l, j: (0, 0)),
            pl.BlockSpec((1, D, tn), lambda l, j: (l, 0, j)),
            pl.BlockSpec((1, 1, tn), lambda l, j: (l, 0, j)),
        ],
        out_specs=pl.BlockSpec((1, B, tn), lambda l, j: (l, 0, j)),
        compiler_params=_cparams(("parallel", "parallel")),
        name="ada_mod",
    )(c, w_ada, b_ada.reshape(L, 1, N))


def _inproj_kernel(x_ref, g_ref, sc_ref, sh_ref, w_ref, qkv_ref, rest_ref, h_scr, *, n_qkv):
    j = pl.program_id(1)

    @pl.when(j == 0)
    def _():
        h = _rms(x_ref[...]) * g_ref[...]
        h = h * (1.0 + sc_ref[0]) + sh_ref[0]
        h_scr[...] = h.astype(BF16)

    acc = jnp.dot(h_scr[...], w_ref[...], preferred_element_type=F32)

    @pl.when(j < n_qkv)
    def _():
        qkv_ref[...] = acc.astype(BF16)

    @pl.when(j >= n_qkv)
    def _():
        rest_ref[...] = acc


def _inproj(x2, g, sc, sh, w_bf, seq, attn_w):
    T, D = x2.shape
    N = w_bf.shape[1]
    tm, tn = 1024, 512
    n_qkv = 3 * attn_w // tn
    n_rest = (N - 3 * attn_w) // tn
    per_b = seq // tm
    return pl.pallas_call(
        functools.partial(_inproj_kernel, n_qkv=n_qkv),
        out_shape=(jax.ShapeDtypeStruct((T, 3 * attn_w), BF16),
                   jax.ShapeDtypeStruct((T, N - 3 * attn_w), F32)),
        grid=(T // tm, n_qkv + n_rest),
        in_specs=[
            pl.BlockSpec((tm, D), lambda i, j: (i, 0)),
            pl.BlockSpec((1, D), lambda i, j: (0, 0)),
            pl.BlockSpec((1, 1, D), lambda i, j: (i // per_b, 0, 0)),
            pl.BlockSpec((1, 1, D), lambda i, j: (i // per_b, 0, 0)),
            pl.BlockSpec((D, tn), lambda i, j: (0, j)),
        ],
        out_specs=(
            pl.BlockSpec((tm, tn), lambda i, j: (i, jnp.minimum(j, n_qkv - 1))),
            pl.BlockSpec((tm, tn), lambda i, j: (i, jnp.maximum(j - n_qkv, 0))),
        ),
        scratch_shapes=[pltpu.VMEM((tm, D), BF16)],
        compiler_params=_cparams(("parallel", "arbitrary")),
        name="mixer_in_proj",
    )(x2, g, sc, sh, w_bf)


def _rel_bucket_np(n):
    n = np.maximum(n, 0)
    max_exact = REL_BUCKETS // 2
    nf = np.maximum(n, max_exact).astype(np.float32)
    large = max_exact + (np.log(nf / np.float32(max_exact)) / np.float32(math.log(REL_MAX_DISTANCE / max_exact))
                         * np.float32(REL_BUCKETS - max_exact)).astype(np.int32)
    large = np.minimum(large, REL_BUCKETS - 1)
    return np.where(n < max_exact, n, large).astype(np.int32)


def _bucket_tables(seq):
    qi = np.arange(MOBA_BLOCK)[:, None]
    ki = np.arange(MOBA_BLOCK)[None, :]
    own = np.where(ki <= qi, _rel_bucket_np(qi - ki), -1)
    prev = _rel_bucket_np(qi - ki + MOBA_BLOCK)
    far = _rel_bucket_np(np.arange(MOBA_BLOCK + 1, max(seq, MOBA_BLOCK + 2)))
    assert np.all(far == far[0])
    return np.stack([own, prev]).astype(np.int32), int(far[0])


def _attn_kernel(tab_ref, q_ref, k_ref, v_ref, bkt_ref, o_ref, bias_scr, *, nblk, far_bucket):
    h = pl.program_id(0)
    b = pl.program_id(1)
    L = MOBA_BLOCK

    @pl.when(b == 0)
    def _():
        for m in range(2):
            bk = bkt_ref[m]
            acc = jnp.full((L, L), NEG_INF, F32)
            for r in range(REL_BUCKETS):
                acc = jnp.where(bk == r, tab_ref[h, r], acc)
            bias_scr[m] = acc

    far_bias = tab_ref[h, far_bucket]
    q = q_ref[...]
    k = k_ref[...]
    v = v_ref[...]
    dn = (((1,), (1,)), ((), ()))

    kmean = jnp.mean(k.astype(F32).reshape(nblk, L, HEAD_DIM), axis=1)
    kmean = jnp.concatenate([kmean, jnp.zeros((LANES - nblk, HEAD_DIM), F32)], axis=0).astype(BF16)
    gate = lax.dot_general(q, kmean, dn, preferred_element_type=F32)
    lane = lax.broadcasted_iota(jnp.int32, (L, LANES), 1)

    for a in range(nblk):
        qa = q[a * L:(a + 1) * L]
        n = (a + 1) * L
        s = lax.dot_general(qa, k[:n], dn, preferred_element_type=F32) * ATTN_SCALE
        sel = None
        if a > MOBA_TOPK:
            ga = jnp.where(lane < a, gate[a * L:(a + 1) * L], NEG_INF)
            cnt = jnp.zeros((L, LANES), F32)
            for i in range(a):
                gi = ga[:, i:i + 1]
                ge = jnp.where(gi >= ga, 1.0, 0.0)
                gt = jnp.where(gi > ga, 1.0, 0.0)
                cnt = cnt + jnp.where(lane > i, ge, gt)
            sel = jnp.where((cnt < float(MOBA_TOPK)) & (lane < a), 1.0, 0.0)
        pieces = []
        for j in range(a + 1):
            sj = s[:, j * L:(j + 1) * L]
            if j == a:
                sj = sj + bias_scr[0]
            else:
                sj = sj + (bias_scr[1] if j == a - 1 else far_bias)
                if sel is not None:
                    sj = jnp.where(sel[:, j:j + 1] > 0.5, sj, NEG_INF)
            pieces.append(sj)
        s = pieces[0] if a == 0 else jnp.concatenate(pieces, axis=1)
        m = jnp.max(s, axis=-1, keepdims=True)
        p = jnp.exp(s - m)
        l = jnp.sum(p, axis=-1, keepdims=True)
        o = jnp.dot(p.astype(BF16), v[:n], preferred_element_type=F32) / l
        o_ref[a * L:(a + 1) * L, :] = o.astype(o_ref.dtype)


def _attention(qkv, rel_bias, batch, seq, heads):
    T = qkv.shape[0]
    nblk = seq // MOBA_BLOCK
    bkt, far_bucket = _bucket_tables(seq)
    table = rel_bias.T.astype(F32)
    return pl.pallas_call(
        functools.partial(_attn_kernel, nblk=nblk, far_bucket=far_bucket),
        out_shape=jax.ShapeDtypeStruct((T, heads * HEAD_DIM), BF16),
        grid=(heads, batch),
        in_specs=[
            pl.BlockSpec(memory_space=pltpu.SMEM),
            pl.BlockSpec((seq, HEAD_DIM), lambda h, b: (b, h)),
            pl.BlockSpec((seq, HEAD_DIM), lambda h, b: (b, heads + h)),
            pl.BlockSpec((seq, HEAD_DIM), lambda h, b: (b, 2 * heads + h)),
            pl.BlockSpec((2, MOBA_BLOCK, MOBA_BLOCK), lambda h, b: (0, 0, 0)),
        ],
        out_specs=pl.BlockSpec((seq, HEAD_DIM), lambda h, b: (b, h)),
        scratch_shapes=[pltpu.VMEM((2, MOBA_BLOCK, MOBA_BLOCK), F32)],
        compiler_params=_cparams(("arbitrary", "arbitrary")),
        name="moba_attention",
    )(table, qkv, qkv, qkv, jnp.asarray(bkt))


def _poolconv_kernel(u_ref, gb_ref, gc_ref, hc_ref, pw_ref, ps_ref, cw_ref, o_ref):
    S = u_ref.shape[0]
    G = len(POOL_WINDOWS)
    row = lax.broadcasted_iota(jnp.int32, (S, GROUP_DIM), 0)

    def shift(x, k):
        return jnp.where(row >= k, pltpu.roll(x, k, 0), 0.0)

    for gi, w in enumerate(POOL_WINDOWS):
        cs = slice(gi * GROUP_DIM, (gi + 1) * GROUP_DIM)
        ug = u_ref[:, cs]
        win = ug
        k = 1
        while k < w:
            win = win + shift(win, k)
            k *= 2
        count = jnp.minimum(row + 1, w).astype(F32)
        d = win / count - ug
        y = jnp.dot(d.astype(BF16), pw_ref[gi].astype(BF16), preferred_element_type=F32)
        o_ref[:, cs] = (y * ps_ref[:, cs]).astype(o_ref.dtype)

        uu = gc_ref[:, cs] * hc_ref[:, cs]
        conv = (cw_ref[0:1, cs] * shift(uu, 2) + cw_ref[1:2, cs] * shift(uu, 1)
                + cw_ref[2:3, cs] * uu)
        oc = slice((G + gi) * GROUP_DIM, (G + gi + 1) * GROUP_DIM)
        o_ref[:, oc] = (gb_ref[:, cs] * conv).astype(o_ref.dtype)


def _poolconv(rest, pool_w, pool_scale, conv_w, batch, seq):
    T = rest.shape[0]
    W = len(POOL_WINDOWS) * GROUP_DIM
    spec = lambda c: pl.BlockSpec((seq, W), lambda b: (b, c))
    return pl.pallas_call(
        _poolconv_kernel,
        out_shape=jax.ShapeDtypeStruct((T, 2 * W), BF16),
        grid=(batch,),
        in_specs=[
            spec(0), spec(1), spec(2), spec(3),
            pl.BlockSpec(pool_w.shape, lambda b: (0, 0, 0)),
            pl.BlockSpec((1, W), lambda b: (0, 0)),
            pl.BlockSpec((CONV_K, W), lambda b: (0, 0)),
        ],
        out_specs=pl.BlockSpec((seq, 2 * W), lambda b: (b, 0)),
        compiler_params=_cparams(("parallel",)),
        name="pool_conv_mixers",
    )(rest, rest, rest, rest, pool_w, pool_scale.reshape(1, W), conv_w)


def _outproj_kernel(att_ref, pc_ref, x_ref, wa_ref, wp_ref, gpost_ref, gtm_ref, gpre_ref, scf_ref, shf_ref,
                    rwh_ref, rwl_ref, rb_ref,
                    x1_ref, h2_ref, idx_ref, gate_ref, rank_ref, cnt_ref, carry_scr):
    i = pl.program_id(0)
    tm = x_ref.shape[0]

    @pl.when(i == 0)
    def _():
        carry_scr[...] = jnp.zeros_like(carry_scr)

    mix = (jnp.dot(att_ref[...], wa_ref[...], preferred_element_type=F32)
           + jnp.dot(pc_ref[...], wp_ref[...], preferred_element_type=F32))
    x1 = x_ref[...] + gtm_ref[0] * (_rms(mix) * gpost_ref[...])
    x1_ref[...] = x1
    h2 = (_rms(x1) * gpre_ref[...]) * (1.0 + scf_ref[0]) + shf_ref[0]
    h2_ref[...] = h2

    hi = h2.astype(BF16)
    lo = (h2 - hi.astype(F32)).astype(BF16)
    logits = (jnp.dot(hi, rwh_ref[...], preferred_element_type=F32)
              + jnp.dot(lo, rwh_ref[...], preferred_element_type=F32)
              + jnp.dot(hi, rwl_ref[...], preferred_element_type=F32)) + rb_ref[...]

    lane = lax.broadcasted_iota(jnp.int32, (tm, LANES), 1)
    work = logits
    mem = jnp.zeros((tm, LANES), F32)
    vals, hots = [], []
    idx_out = jnp.zeros((tm, LANES), jnp.int32)
    for kk in range(TOPK_EXPERTS):
        m = jnp.max(work, axis=-1, keepdims=True)
        ik = jnp.min(jnp.where(work == m, lane, LANES), axis=-1, keepdims=True)
        hot = lane == ik
        vals.append(m)
        hots.append(hot)
        idx_out = jnp.where(lane == kk, ik, idx_out)
        mem = jnp.where(hot, 1.0, mem)
        work = jnp.where(hot, -jnp.inf, work)
    idx_ref[...] = idx_out

    es = [jnp.exp(vk - vals[0]) for vk in vals]
    denom = es[0]
    for e in es[1:]:
        denom = denom + e
    gates = jnp.zeros((tm, LANES), F32)
    for kk in range(TOPK_EXPERTS):
        gates = jnp.where(lane == kk, es[kk] / denom, gates)
    gate_ref[...] = gates

    r_i = lax.broadcasted_iota(jnp.int32, (tm, tm), 0)
    c_i = lax.broadcasted_iota(jnp.int32, (tm, tm), 1)
    tri = jnp.where(c_i < r_i, 1.0, 0.0).astype(BF16)
    before = jnp.dot(tri, mem.astype(BF16), preferred_element_type=F32) + carry_scr[0:1, :]
    ranks = jnp.zeros((tm, LANES), jnp.int32)
    for kk in range(TOPK_EXPERTS):
        rk = jnp.sum(jnp.where(hots[kk], before, 0.0), axis=-1, keepdims=True)
        ranks = jnp.where(lane == kk, rk.astype(jnp.int32), ranks)
    rank_ref[...] = ranks
    carry = carry_scr[...] + jnp.sum(mem, axis=0, keepdims=True)
    carry_scr[...] = carry
    cnt_ref[...] = carry


def _outproj(att, pc, x2, wa, wp, gpost, gtm, gpre, scf, shf, rwh, rwl, rb, seq):
    T, D = x2.shape
    tm = 256
    per_b = seq // tm
    row = lambda w: pl.BlockSpec((tm, w), lambda i: (i, 0))
    const = lambda shape: pl.BlockSpec(shape, lambda i: tuple(0 for _ in shape))
    perb = pl.BlockSpec((1, 1, D), lambda i: (i // per_b, 0, 0))
    return pl.pallas_call(
        _outproj_kernel,
        out_shape=(jax.ShapeDtypeStruct((T, D), F32), jax.ShapeDtypeStruct((T, D), F32),
                   jax.ShapeDtypeStruct((T, LANES), jnp.int32), jax.ShapeDtypeStruct((T, LANES), F32),
                   jax.ShapeDtypeStruct((T, LANES), jnp.int32), jax.ShapeDtypeStruct((8, LANES), F32)),
        grid=(T // tm,),
        in_specs=[row(att.shape[1]), row(pc.shape[1]), row(D), const(wa.shape), const(wp.shape),
                  const((1, D)), perb, const((1, D)), perb, perb,
                  const(rwh.shape), const(rwl.shape), const((1, LANES))],
        out_specs=(row(D), row(D), row(LANES), row(LANES), row(LANES), const((8, LANES))),
        scratch_shapes=[pltpu.VMEM((8, LANES), F32)],
        compiler_params=_cparams(("arbitrary",)),
        name="out_proj_router",
    )(att, pc, x2, wa, wp, gpost, gtm, gpre, scf, shf, rwh, rwl, rb)


MOE_F_CHUNK = 256
MOE_D_CHUNK = 512


def _moe_kernel(te_ref, rt_ref, nu_ref, h_hbm, wg_ref, bg_ref, wu_ref, bu_ref, wd_ref, bd_ref, o_ref,
                xbuf0, xbuf1, act_scr, sem):
    i = pl.program_id(0)
    tm = xbuf0.shape[0]
    Fe = act_scr.shape[1]
    D = o_ref.shape[1]
    n_used = nu_ref[0]
    bufs = (xbuf0, xbuf1)

    def row_copy(base, r, slot):
        tok = rt_ref[base + r]
        return pltpu.make_async_copy(h_hbm.at[pl.ds(tok, 1)], bufs[slot].at[pl.ds(r, 1)], sem.at[slot])

    def tile_wait(slot):
        pltpu.make_async_copy(h_hbm.at[pl.ds(0, tm)], bufs[slot], sem.at[slot]).wait()

    @pl.when(i == 0)
    def _():
        def issue(r, carry):
            row_copy(0, r, 0).start()
            return carry
        lax.fori_loop(0, tm, issue, 0)

    def tile(slot):
        nslot = 1 - slot
        nbase = jnp.minimum(i + 1, n_used - 1) * tm
        n_f, n_d = Fe // MOE_F_CHUNK, D // MOE_D_CHUNK
        per = tm // (n_f + n_d)

        def issue_part(p):
            for r in range(p * per, (p + 1) * per):
                row_copy(nbase, r, nslot).start()

        tile_wait(slot)
        x = bufs[slot][...].astype(BF16)
        for c in range(n_f):
            issue_part(c)
            cs = slice(c * MOE_F_CHUNK, (c + 1) * MOE_F_CHUNK)
            g = jnp.dot(x, wg_ref[0, 0, :, cs], preferred_element_type=F32) + bg_ref[0, 0, :, cs]
            u = jnp.dot(x, wu_ref[0, 0, :, cs], preferred_element_type=F32) + bu_ref[0, 0, :, cs]
            g = jnp.minimum(g, SWIGLU_LIMIT)
            u = jnp.clip(u, -SWIGLU_LIMIT, SWIGLU_LIMIT)
            act_scr[:, cs] = (g * jax.nn.sigmoid(SWIGLU_ALPHA * g) * (u + 1.0)).astype(BF16)
        act = act_scr[...]
        for c in range(n_d):
            issue_part(n_f + c)
            ds = slice(c * MOE_D_CHUNK, (c + 1) * MOE_D_CHUNK)
            o_ref[:, ds] = jnp.dot(act, wd_ref[0, 0, :, ds], preferred_element_type=F32) + bd_ref[0, 0, :, ds]

        @pl.when(i == n_used - 1)
        def _():
            tile_wait(nslot)

    for slot in range(2):
        pl.when((i < n_used) & (i % 2 == slot))(functools.partial(tile, slot))

    @pl.when(i >= n_used)
    def _():
        o_ref[...] = jnp.zeros_like(o_ref)


def _moe(layer, tile_expert, row_token, n_used, h2, wg, bg, wu, bu, wd, bd, tm):
    T, D = h2.shape
    L, E, _, Fe = wg.shape
    n_tiles = tile_expert.shape[0]
    assert tm % (Fe // MOE_F_CHUNK + D // MOE_D_CHUNK) == 0
    wspec = lambda shape: pl.BlockSpec((1, 1) + shape, lambda i, te, rt, nu: (layer, te[i], 0, 0))
    return pl.pallas_call(
        _moe_kernel,
        out_shape=jax.ShapeDtypeStruct((n_tiles * tm, D), F32),
        grid_spec=pltpu.PrefetchScalarGridSpec(
            num_scalar_prefetch=3,
            grid=(n_tiles,),
            in_specs=[pl.BlockSpec(memory_space=pl.ANY),
                      wspec((D, Fe)), wspec((1, Fe)), wspec((D, Fe)), wspec((1, Fe)),
                      wspec((Fe, D)), wspec((1, D))],
            out_specs=pl.BlockSpec((tm, D), lambda i, te, rt, nu: (i, 0)),
            scratch_shapes=[pltpu.VMEM((tm, D), F32), pltpu.VMEM((tm, D), F32), pltpu.VMEM((tm, Fe), BF16),
                            pltpu.SemaphoreType.DMA((2,))],
        ),
        compiler_params=_cparams(("arbitrary",)),
        name="moe_experts",
    )(tile_expert, row_token, n_used, h2, wg, bg.reshape(L, E, 1, Fe), wu, bu.reshape(L, E, 1, Fe),
      wd, bd.reshape(L, E, 1, D))


def _combine_kernel(dest_ref, yp_hbm, gate_ref, x_ref, gpost_ref, gtf_ref, o_ref, buf, sem):
    i = pl.program_id(0)
    tc = x_ref.shape[0]
    base = i * tc * TOPK_EXPERTS

    def issue(r, carry):
        for kk in range(TOPK_EXPERTS):
            d = dest_ref[base + r * TOPK_EXPERTS + kk]
            pltpu.make_async_copy(yp_hbm.at[pl.ds(d, 1)], buf.at[kk, pl.ds(r, 1)], sem).start()
        return carry

    lax.fori_loop(0, tc, issue, 0)
    for kk in range(TOPK_EXPERTS):
        pltpu.make_async_copy(yp_hbm.at[pl.ds(0, tc)], buf.at[kk], sem).wait()
    gates = gate_ref[...]
    y = gates[:, 0:1] * buf[0]
    for kk in range(1, TOPK_EXPERTS):
        y = y + gates[:, kk:kk + 1] * buf[kk]
    o_ref[...] = x_ref[...] + gtf_ref[0] * (_rms(y) * gpost_ref[...])


def _combine(dest, yp, gates, x1, gpost, gtf, seq):
    T, D = x1.shape
    tc = 256
    per_b = seq // tc
    return pl.pallas_call(
        _combine_kernel,
        out_shape=jax.ShapeDtypeStruct((T, D), F32),
        grid_spec=pltpu.PrefetchScalarGridSpec(
            num_scalar_prefetch=1,
            grid=(T // tc,),
            in_specs=[pl.BlockSpec(memory_space=pl.ANY),
                      pl.BlockSpec((tc, LANES), lambda i, d: (i, 0)),
                      pl.BlockSpec((tc, D), lambda i, d: (i, 0)),
                      pl.BlockSpec((1, D), lambda i, d: (0, 0)),
                      pl.BlockSpec((1, 1, D), lambda i, d: (i // per_b, 0, 0))],
            out_specs=pl.BlockSpec((tc, D), lambda i, d: (i, 0)),
            scratch_shapes=[pltpu.VMEM((TOPK_EXPERTS, tc, D), F32), pltpu.SemaphoreType.DMA],
        ),
        compiler_params=_cparams(("arbitrary",)),
        name="moe_combine",
    )(dest, yp, gates, x1, gpost, gtf)


def _dest_kernel(idx_ref, rank_ref, cnt_ref, dest_ref, *, tm_rows):
    tm = idx_ref.shape[0]
    tiles = jnp.floor((cnt_ref[0:1, :] + float(tm_rows - 1)) * (1.0 / tm_rows))
    r_i = lax.broadcasted_iota(jnp.int32, (LANES, LANES), 0)
    c_i = lax.broadcasted_iota(jnp.int32, (LANES, LANES), 1)
    before = jnp.where(r_i < c_i, 1.0, 0.0).astype(BF16)
    start = jnp.dot(jnp.broadcast_to(tiles, (8, LANES)).astype(BF16), before,
                    preferred_element_type=F32)[0:1] * float(tm_rows)
    lane = lax.broadcasted_iota(jnp.int32, (tm, LANES), 1)
    idx = idx_ref[...]
    rank = rank_ref[...]
    out = jnp.zeros((tm, LANES), jnp.int32)
    for kk in range(TOPK_EXPERTS):
        base = jnp.sum(jnp.where(lane == idx[:, kk:kk + 1], start, 0.0), axis=-1, keepdims=True)
        out = jnp.where(lane == kk, base.astype(jnp.int32) + rank[:, kk:kk + 1], out)
    dest_ref[...] = out


def _dest(idx, rank, cnt, tm_rows):
    T = idx.shape[0]
    tm = 2048
    assert tm_rows & (tm_rows - 1) == 0
    row = pl.BlockSpec((tm, LANES), lambda i: (i, 0))
    return pl.pallas_call(
        functools.partial(_dest_kernel, tm_rows=tm_rows),
        out_shape=jax.ShapeDtypeStruct((T, LANES), jnp.int32),
        grid=(T // tm,),
        in_specs=[row, row, pl.BlockSpec((8, LANES), lambda i: (0, 0))],
        out_specs=row,
        compiler_params=_cparams(("parallel",)),
        name="route_dest",
    )(idx, rank, cnt)


INVERT_UNROLL = 8


def _invert_kernel(dest_ref, lo_ref, hi_ref, out_ref):
    def fill(j, carry):
        out_ref[j] = 0
        return carry

    for e in range(lo_ref.shape[0]):
        lax.fori_loop(lo_ref[e], hi_ref[e], fill, 0)

    def body(b, carry):
        for u in range(INVERT_UNROLL):
            i = b * INVERT_UNROLL + u
            out_ref[dest_ref[i]] = i // TOPK_EXPERTS
        return carry

    lax.fori_loop(0, dest_ref.shape[0] // INVERT_UNROLL, body, 0)


def _invert(dest, lo, hi, n_rows):
    smem = pl.BlockSpec(memory_space=pltpu.SMEM)
    return pl.pallas_call(
        _invert_kernel,
        out_shape=jax.ShapeDtypeStruct((n_rows,), jnp.int32),
        in_specs=[smem, smem, smem],
        out_specs=smem,
        name="route_invert",
    )(dest, lo, hi)


def _routing_tables(idx, rank, cnt, n_experts, tm):
    T = idx.shape[0]
    n_tiles = (T * TOPK_EXPERTS) // tm + n_experts
    counts = cnt[0, :n_experts].astype(jnp.int32)
    padded = ((counts + tm - 1) // tm) * tm
    pad_end = jnp.cumsum(padded)
    tile_start = jnp.arange(n_tiles, dtype=jnp.int32) * tm
    tile_expert = jnp.minimum(jnp.sum((pad_end[None, :] <= tile_start[:, None]).astype(jnp.int32), axis=1),
                              n_experts - 1)
    n_used = pad_end[-1:] // tm
    total = jnp.full((1,), n_tiles * tm, jnp.int32)
    lo = jnp.concatenate([pad_end - padded + counts, pad_end[-1:]])
    hi = jnp.concatenate([pad_end, total])
    dest = _dest(idx, rank, cnt, tm)[:, :TOPK_EXPERTS].reshape(-1)
    row_token = _invert(dest, lo, hi, n_tiles * tm)
    return dest, row_token, tile_expert, n_used


def kernel(x, c, w_ada, b_ada, g_pre_mix, g_post_mix, g_pre_ffn, g_post_ffn, w_in, w_out, pool_w, pool_scale,
           conv_w, rel_bias, router_w, router_b, w_gate, b_gate, w_up, b_up, w_down, b_down):
    B, S, D = x.shape
    L = w_ada.shape[0]
    T = B * S
    E = router_w.shape[-1]
    attn_w = D // 2
    heads = attn_w // HEAD_DIM
    moe_tm = 256

    mod = _ada(c, w_ada, b_ada)
    wg_bf, wu_bf, wd_bf = w_gate.astype(BF16), w_up.astype(BF16), w_down.astype(BF16)
    x2 = x.reshape(T, D)
    for l in range(L):
        sh_m, sc_m, gt_m, sh_f, sc_f, gt_f = [mod[l, :, i * D:(i + 1) * D].reshape(B, 1, D) for i in range(N_MOD)]
        row = lambda v: v.reshape(1, D)

        qkv, rest = _inproj(x2, row(g_pre_mix[l]), sc_m, sh_m, w_in[l].astype(BF16), S, attn_w)
        att = _attention(qkv, rel_bias, B, S, heads)
        pc = _poolconv(rest, pool_w[l], pool_scale[l], conv_w[l], B, S)

        wo = w_out[l].astype(BF16)
        rw = jnp.zeros((D, LANES), F32).at[:, :E].set(router_w[l])
        rwh = rw.astype(BF16)
        rwl = (rw - rwh.astype(F32)).astype(BF16)
        rb = jnp.full((1, LANES), NEG_INF, F32).at[0, :E].set(router_b[l])
        x1, h2, idx, gates, rank, cnt = _outproj(
            att, pc, x2, wo[:attn_w], wo[attn_w:], row(g_post_mix[l]), gt_m, row(g_pre_ffn[l]), sc_f, sh_f,
            rwh, rwl, rb, S)

        dest, row_token, tile_expert, n_used = _routing_tables(idx, rank, cnt, E, moe_tm)
        yp = _moe(l, tile_expert, row_token, n_used, h2, wg_bf, b_gate, wu_bf, b_up, wd_bf, b_down, moe_tm)
        x2 = _combine(dest, yp, gates, x1, row(g_post_ffn[l]), gt_f, S)
    return x2.reshape(B, S, D)
```

```python
import functools
import math

import numpy as np
import jax
import jax.numpy as jnp
from jax import lax
from jax.experimental import pallas as pl
from jax.experimental.pallas import tpu as pltpu

F32 = jnp.float32
BF16 = jnp.bfloat16

LANES = 128
HEAD_DIM = 128
MOBA_BLOCK = 256
MOBA_TOPK = 3
REL_BUCKETS = 32
REL_MAX_DISTANCE = 128
POOL_WINDOWS = (2, 4, 8, 16)
GROUP_DIM = 128
CONV_K = 3
TOPK_EXPERTS = 4
SWIGLU_LIMIT = 7.0
SWIGLU_ALPHA = 1.702
N_MOD = 6
RMS_EPS = 1e-6
NEG_INF = -1e30
ATTN_SCALE = HEAD_DIM ** -0.5

VMEM_LIMIT = 56 * 1024 * 1024


def _cparams(sem):
    return pltpu.CompilerParams(dimension_semantics=sem, vmem_limit_bytes=VMEM_LIMIT)


def _rms(x):
    return x * lax.rsqrt(jnp.mean(x * x, axis=-1, keepdims=True) + RMS_EPS)


ROW_SUBLANES = 8


def _bf16_bits(v):
    return lax.bitcast_convert_type(v.astype(BF16).astype(F32), jnp.uint32)


def _pack_words(hi, lo):
    return _bf16_bits(hi) | (_bf16_bits(lo) >> 16)


def _unpack_hi(w):
    return lax.bitcast_convert_type(w & jnp.uint32(0xFFFF0000), F32)


def _unpack_lo(w):
    return lax.bitcast_convert_type(w << 16, F32)


def _ada_kernel(c_ref, w_ref, b_ref, o_ref):
    c = c_ref[...]
    ca = (c * jax.nn.sigmoid(c)).astype(BF16)
    o_ref[0] = jnp.dot(ca, w_ref[0].astype(BF16), preferred_element_type=F32) + b_ref[0]


def _ada(c, w_ada, b_ada):
    L, D, N = w_ada.shape
    B = c.shape[0]
    tn = 1024
    return pl.pallas_call(
        _ada_kernel,
        out_shape=jax.ShapeDtypeStruct((L, B, N), F32),
        grid=(L, N // tn),
        in_specs=[
            pl.BlockSpec((B, D), lambda l, j: (0, 0)),
            pl.BlockSpec((1, D, tn), lambda l, j: (l, 0, j)),
            pl.BlockSpec((1, 1, tn), lambda l, j: (l, 0, j)),
        ],
        out_specs=pl.BlockSpec((1, B, tn), lambda l, j: (l, 0, j)),
        compiler_params=_cparams(("parallel", "parallel")),
        name="ada_mod",
    )(c, w_ada, b_ada.reshape(L, 1, N))


def _inproj_kernel(x_ref, g_ref, sc_ref, sh_ref, w_ref, qkv_ref, rest_ref, h_scr, *, n_qkv):
    j = pl.program_id(1)

    @pl.when(j == 0)
    def _():
        h = _rms(x_ref[...]) * g_ref[...]
        h = h * (1.0 + sc_ref[0]) + sh_ref[0]
        h_scr[...] = h.astype(BF16)

    acc = jnp.dot(h_scr[...], w_ref[...], preferred_element_type=F32)

    @pl.when(j < n_qkv)
    def _():
        qkv_ref[...] = acc.astype(BF16)

    @pl.when(j >= n_qkv)
    def _():
        rest_ref[...] = acc


def _inproj(x2, g, sc, sh, w_bf, seq, attn_w):
    T, D = x2.shape
    N = w_bf.shape[1]
    tm, tn = 1024, 512
    n_qkv = 3 * attn_w // tn
    n_rest = (N - 3 * attn_w) // tn
    per_b = seq // tm
    return pl.pallas_call(
        functools.partial(_inproj_kernel, n_qkv=n_qkv),
        out_shape=(jax.ShapeDtypeStruct((T, 3 * attn_w), BF16),
                   jax.ShapeDtypeStruct((T, N - 3 * attn_w), F32)),
        grid=(T // tm, n_qkv + n_rest),
        in_specs=[
            pl.BlockSpec((tm, D), lambda i, j: (i, 0)),
            pl.BlockSpec((1, D), lambda i, j: (0, 0)),
            pl.BlockSpec((1, 1, D), lambda i, j: (i // per_b, 0, 0)),
            pl.BlockSpec((1, 1, D), lambda i, j: (i // per_b, 0, 0)),
            pl.BlockSpec((D, tn), lambda i, j: (0, j)),
        ],
        out_specs=(
            pl.BlockSpec((tm, tn), lambda i, j: (i, jnp.minimum(j, n_qkv - 1))),
            pl.BlockSpec((tm, tn), lambda i, j: (i, jnp.maximum(j - n_qkv, 0))),
        ),
        scratch_shapes=[pltpu.VMEM((tm, D), BF16)],
        compiler_params=_cparams(("parallel", "arbitrary")),
        name="mixer_in_proj",
    )(x2, g, sc, sh, w_bf)


def _rel_bucket_np(n):
    n = np.maximum(n, 0)
    max_exact = REL_BUCKETS // 2
    nf = np.maximum(n, max_exact).astype(np.float32)
    large = max_exact + (np.log(nf / np.float32(max_exact)) / np.float32(math.log(REL_MAX_DISTANCE / max_exact))
                         * np.float32(REL_BUCKETS - max_exact)).astype(np.int32)
    large = np.minimum(large, REL_BUCKETS - 1)
    return np.where(n < max_exact, n, large).astype(np.int32)


def _bucket_tables(seq):
    qi = np.arange(MOBA_BLOCK)[:, None]
    ki = np.arange(MOBA_BLOCK)[None, :]
    own = np.where(ki <= qi, _rel_bucket_np(qi - ki), -1)
    prev = _rel_bucket_np(qi - ki + MOBA_BLOCK)
    far = _rel_bucket_np(np.arange(MOBA_BLOCK + 1, max(seq, MOBA_BLOCK + 2)))
    assert np.all(far == far[0])
    return np.stack([own, prev]).astype(np.int32), int(far[0])


def _attn_kernel(tab_ref, q_ref, k_ref, v_ref, bkt_ref, o_ref, bias_scr, *, nblk, far_bucket):
    h = pl.program_id(0)
    b = pl.program_id(1)
    L = MOBA_BLOCK

    @pl.when(b == 0)
    def _():
        for m in range(2):
            bk = bkt_ref[m]
            acc = jnp.full((L, L), NEG_INF, F32)
            for r in range(REL_BUCKETS):
                acc = jnp.where(bk == r, tab_ref[h, r], acc)
            bias_scr[m] = acc

    far_bias = tab_ref[h, far_bucket]
    q = q_ref[...]
    k = k_ref[...]
    v = v_ref[...]
    dn = (((1,), (1,)), ((), ()))

    kmean = jnp.mean(k.astype(F32).reshape(nblk, L, HEAD_DIM), axis=1)
    kmean = jnp.concatenate([kmean, jnp.zeros((LANES - nblk, HEAD_DIM), F32)], axis=0).astype(BF16)
    gate = lax.dot_general(q, kmean, dn, preferred_element_type=F32)
    lane = lax.broadcasted_iota(jnp.int32, (L, LANES), 1)

    for a in range(nblk):
        qa = q[a * L:(a + 1) * L]
        n = (a + 1) * L
        s = lax.dot_general(qa, k[:n], dn, preferred_element_type=F32) * ATTN_SCALE
        sel = None
        if a > MOBA_TOPK:
            ga = jnp.where(lane < a, gate[a * L:(a + 1) * L], NEG_INF)
            cnt = jnp.zeros((L, LANES), F32)
            for i in range(a):
                gi = ga[:, i:i + 1]
                ge = jnp.where(gi >= ga, 1.0, 0.0)
                gt = jnp.where(gi > ga, 1.0, 0.0)
                cnt = cnt + jnp.where(lane > i, ge, gt)
            sel = jnp.where((cnt < float(MOBA_TOPK)) & (lane < a), 1.0, 0.0)
        pieces = []
        for j in range(a + 1):
            sj = s[:, j * L:(j + 1) * L]
            if j == a:
                sj = sj + bias_scr[0]
            else:
                sj = sj + (bias_scr[1] if j == a - 1 else far_bias)
                if sel is not None:
                    sj = jnp.where(sel[:, j:j + 1] > 0.5, sj, NEG_INF)
            pieces.append(sj)
        s = pieces[0] if a == 0 else jnp.concatenate(pieces, axis=1)
        m = jnp.max(s, axis=-1, keepdims=True)
        p = jnp.exp(s - m)
        l = jnp.sum(p, axis=-1, keepdims=True)
        o = jnp.dot(p.astype(BF16), v[:n], preferred_element_type=F32) / l
        o_ref[a * L:(a + 1) * L, :] = o.astype(o_ref.dtype)


def _attention(qkv, rel_bias, batch, seq, heads):
    T = qkv.shape[0]
    nblk = seq // MOBA_BLOCK
    bkt, far_bucket = _bucket_tables(seq)
    table = rel_bias.T.astype(F32)
    return pl.pallas_call(
        functools.partial(_attn_kernel, nblk=nblk, far_bucket=far_bucket),
        out_shape=jax.ShapeDtypeStruct((T, heads * HEAD_DIM), BF16),
        grid=(heads, batch),
        in_specs=[
            pl.BlockSpec(memory_space=pltpu.SMEM),
            pl.BlockSpec((seq, HEAD_DIM), lambda h, b: (b, h)),
            pl.BlockSpec((seq, HEAD_DIM), lambda h, b: (b, heads + h)),
            pl.BlockSpec((seq, HEAD_DIM), lambda h, b: (b, 2 * heads + h)),
            pl.BlockSpec((2, MOBA_BLOCK, MOBA_BLOCK), lambda h, b: (0, 0, 0)),
        ],
        out_specs=pl.BlockSpec((seq, HEAD_DIM), lambda h, b: (b, h)),
        scratch_shapes=[pltpu.VMEM((2, MOBA_BLOCK, MOBA_BLOCK), F32)],
        compiler_params=_cparams(("arbitrary", "arbitrary")),
        name="moba_attention",
    )(table, qkv, qkv, qkv, jnp.asarray(bkt))


def _poolconv_kernel(u_ref, gb_ref, gc_ref, hc_ref, pw_ref, ps_ref, cw_ref, o_ref):
    S = u_ref.shape[0]
    G = len(POOL_WINDOWS)
    row = lax.broadcasted_iota(jnp.int32, (S, GROUP_DIM), 0)

    def shift(x, k):
        return jnp.where(row >= k, pltpu.roll(x, k, 0), 0.0)

    for gi, w in enumerate(POOL_WINDOWS):
        cs = slice(gi * GROUP_DIM, (gi + 1) * GROUP_DIM)
        ug = u_ref[:, cs]
        win = ug
        k = 1
        while k < w:
            win = win + shift(win, k)
            k *= 2
        count = jnp.minimum(row + 1, w).astype(F32)
        d = win / count - ug
        y = jnp.dot(d.astype(BF16), pw_ref[gi].astype(BF16), preferred_element_type=F32)
        o_ref[:, cs] = (y * ps_ref[:, cs]).astype(o_ref.dtype)

        uu = gc_ref[:, cs] * hc_ref[:, cs]
        conv = (cw_ref[0:1, cs] * shift(uu, 2) + cw_ref[1:2, cs] * shift(uu, 1)
                + cw_ref[2:3, cs] * uu)
        oc = slice((G + gi) * GROUP_DIM, (G + gi + 1) * GROUP_DIM)
        o_ref[:, oc] = (gb_ref[:, cs] * conv).astype(o_ref.dtype)


def _poolconv(rest, pool_w, pool_scale, conv_w, batch, seq):
    T = rest.shape[0]
    W = len(POOL_WINDOWS) * GROUP_DIM
    spec = lambda c: pl.BlockSpec((seq, W), lambda b: (b, c))
    return pl.pallas_call(
        _poolconv_kernel,
        out_shape=jax.ShapeDtypeStruct((T, 2 * W), BF16),
        grid=(batch,),
        in_specs=[
            spec(0), spec(1), spec(2), spec(3),
            pl.BlockSpec(pool_w.shape, lambda b: (0, 0, 0)),
            pl.BlockSpec((1, W), lambda b: (0, 0)),
            pl.BlockSpec((CONV_K, W), lambda b: (0, 0)),
        ],
        out_specs=pl.BlockSpec((seq, 2 * W), lambda b: (b, 0)),
        compiler_params=_cparams(("parallel",)),
        name="pool_conv_mixers",
    )(rest, rest, rest, rest, pool_w, pool_scale.reshape(1, W), conv_w)


def _outproj_kernel(att_ref, pc_ref, x_ref, wa_ref, wp_ref, gpost_ref, gtm_ref, gpre_ref, scf_ref, shf_ref,
                    rwh_ref, rwl_ref, rb_ref,
                    x1_ref, h2_ref, idx_ref, gate_ref, rank_ref, cnt_ref, carry_scr):
    i = pl.program_id(0)
    tm = x_ref.shape[0]

    @pl.when(i == 0)
    def _():
        carry_scr[...] = jnp.zeros_like(carry_scr)

    mix = (jnp.dot(att_ref[...], wa_ref[...], preferred_element_type=F32)
           + jnp.dot(pc_ref[...], wp_ref[...], preferred_element_type=F32))
    x1 = x_ref[...] + gtm_ref[0] * (_rms(mix) * gpost_ref[...])
    x1_ref[...] = x1
    h2 = (_rms(x1) * gpre_ref[...]) * (1.0 + scf_ref[0]) + shf_ref[0]
    half = h2.shape[1] // 2
    for s in range(ROW_SUBLANES):
        h2_ref[pl.ds(s, tm, stride=ROW_SUBLANES), :] = _pack_words(h2[:, s * LANES:(s + 1) * LANES],
                                      h2[:, half + s * LANES:half + (s + 1) * LANES])

    hi = h2.astype(BF16)
    lo = (h2 - hi.astype(F32)).astype(BF16)
    logits = (jnp.dot(hi, rwh_ref[...], preferred_element_type=F32)
              + jnp.dot(lo, rwh_ref[...], preferred_element_type=F32)
              + jnp.dot(hi, rwl_ref[...], preferred_element_type=F32)) + rb_ref[...]

    lane = lax.broadcasted_iota(jnp.int32, (tm, LANES), 1)
    work = logits
    mem = jnp.zeros((tm, LANES), F32)
    vals, hots = [], []
    idx_out = jnp.zeros((tm, LANES), jnp.int32)
    for kk in range(TOPK_EXPERTS):
        m = jnp.max(work, axis=-1, keepdims=True)
        ik = jnp.min(jnp.where(work == m, lane, LANES), axis=-1, keepdims=True)
        hot = lane == ik
        vals.append(m)
        hots.append(hot)
        idx_out = jnp.where(lane == kk, ik, idx_out)
        mem = jnp.where(hot, 1.0, mem)
        work = jnp.where(hot, -jnp.inf, work)
    idx_ref[...] = idx_out

    es = [jnp.exp(vk - vals[0]) for vk in vals]
    denom = es[0]
    for e in es[1:]:
        denom = denom + e
    gates = jnp.zeros((tm, LANES), F32)
    for kk in range(TOPK_EXPERTS):
        gates = jnp.where(lane == kk, es[kk] / denom, gates)
    gate_ref[...] = gates

    r_i = lax.broadcasted_iota(jnp.int32, (tm, tm), 0)
    c_i = lax.broadcasted_iota(jnp.int32, (tm, tm), 1)
    tri = jnp.where(c_i < r_i, 1.0, 0.0).astype(BF16)
    before = jnp.dot(tri, mem.astype(BF16), preferred_element_type=F32) + carry_scr[0:1, :]
    ranks = jnp.zeros((tm, LANES), jnp.int32)
    for kk in range(TOPK_EXPERTS):
        rk = jnp.sum(jnp.where(hots[kk], before, 0.0), axis=-1, keepdims=True)
        ranks = jnp.where(lane == kk, rk.astype(jnp.int32), ranks)
    rank_ref[...] = ranks
    carry = carry_scr[...] + jnp.sum(mem, axis=0, keepdims=True)
    carry_scr[...] = carry
    cnt_ref[...] = carry


def _outproj(att, pc, x2, wa, wp, gpost, gtm, gpre, scf, shf, rwh, rwl, rb, seq):
    T, D = x2.shape
    tm = 256
    per_b = seq // tm
    row = lambda w: pl.BlockSpec((tm, w), lambda i: (i, 0))
    const = lambda shape: pl.BlockSpec(shape, lambda i: tuple(0 for _ in shape))
    perb = pl.BlockSpec((1, 1, D), lambda i: (i // per_b, 0, 0))
    return pl.pallas_call(
        _outproj_kernel,
        out_shape=(jax.ShapeDtypeStruct((T, D), F32), jax.ShapeDtypeStruct((T * ROW_SUBLANES, LANES), jnp.uint32),
                   jax.ShapeDtypeStruct((T, LANES), jnp.int32), jax.ShapeDtypeStruct((T, LANES), F32),
                   jax.ShapeDtypeStruct((T, LANES), jnp.int32), jax.ShapeDtypeStruct((8, LANES), F32)),
        grid=(T // tm,),
        in_specs=[row(att.shape[1]), row(pc.shape[1]), row(D), const(wa.shape), const(wp.shape),
                  const((1, D)), perb, const((1, D)), perb, perb,
                  const(rwh.shape), const(rwl.shape), const((1, LANES))],
        out_specs=(row(D), pl.BlockSpec((tm * ROW_SUBLANES, LANES), lambda i: (i, 0)),
                   row(LANES), row(LANES), row(LANES), const((8, LANES))),
        scratch_shapes=[pltpu.VMEM((8, LANES), F32)],
        compiler_params=_cparams(("arbitrary",)),
        name="out_proj_router",
    )(att, pc, x2, wa, wp, gpost, gtm, gpre, scf, shf, rwh, rwl, rb)


MOE_F_CHUNK = 256
MOE_D_CHUNK = 512
MOE_AHEAD = 2
MOE_SLOTS = MOE_AHEAD + 1


def _moe_kernel(te_ref, rt_ref, nu_ref, h_hbm, wg_ref, bg_ref, wu_ref, bu_ref, wd_ref, bd_ref, o_ref,
                xbuf, act_scr, hi_scr, sem):
    i = pl.program_id(0)
    tm = act_scr.shape[0]
    Fe = act_scr.shape[1]
    D = wd_ref.shape[-1]
    half = D // 2
    n_used = nu_ref[0]
    last = n_used - 1

    def row_copy(tile_id, r, slot):
        tok = rt_ref[tile_id * tm + r]
        src = h_hbm.at[pl.ds(pl.multiple_of(tok * ROW_SUBLANES, ROW_SUBLANES), ROW_SUBLANES)]
        return pltpu.make_async_copy(src, xbuf.at[slot, pl.ds(r * ROW_SUBLANES, ROW_SUBLANES)], sem.at[slot])

    def tile_wait(slot):
        pltpu.make_async_copy(h_hbm.at[pl.ds(0, tm * ROW_SUBLANES)], xbuf.at[slot], sem.at[slot]).wait()

    @pl.when(i == 0)
    def _():
        for t in range(MOE_AHEAD):
            def issue(r, carry, t=t):
                row_copy(jnp.minimum(t, last), r, t).start()
                return carry
            lax.fori_loop(0, tm, issue, 0)

    @pl.when(i < n_used)
    def _():
        slot = i % MOE_SLOTS
        nslot = (i + MOE_AHEAD) % MOE_SLOTS
        ntile = jnp.minimum(i + MOE_AHEAD, last)
        n_f, n_d = Fe // MOE_F_CHUNK, D // MOE_D_CHUNK
        per = tm // (n_f + n_d)

        def issue_part(p):
            for r in range(p * per, (p + 1) * per):
                row_copy(ntile, r, nslot).start()

        tile_wait(slot)
        words = [xbuf[slot, pl.ds(s, tm, stride=ROW_SUBLANES), :] for s in range(ROW_SUBLANES)]
        x = jnp.concatenate([_unpack_hi(w).astype(BF16) for w in words]
                            + [_unpack_lo(w).astype(BF16) for w in words], axis=1)
        for c in range(n_f):
            issue_part(c)
            cs = slice(c * MOE_F_CHUNK, (c + 1) * MOE_F_CHUNK)
            g = jnp.dot(x, wg_ref[0, 0, :, cs], preferred_element_type=F32) + bg_ref[0, 0, :, cs]
            u = jnp.dot(x, wu_ref[0, 0, :, cs], preferred_element_type=F32) + bu_ref[0, 0, :, cs]
            g = jnp.minimum(g, SWIGLU_LIMIT)
            u = jnp.clip(u, -SWIGLU_LIMIT, SWIGLU_LIMIT)
            act_scr[:, cs] = (g * jax.nn.sigmoid(SWIGLU_ALPHA * g) * (u + 1.0)).astype(BF16)
        act = act_scr[...]
        for c in range(n_d):
            issue_part(n_f + c)
            c0 = c * MOE_D_CHUNK
            y = (jnp.dot(act, wd_ref[0, 0, :, c0:c0 + MOE_D_CHUNK], preferred_element_type=F32)
                 + bd_ref[0, 0, :, c0:c0 + MOE_D_CHUNK])
            for j in range(MOE_D_CHUNK // LANES):
                col = c0 + j * LANES
                yj = y[:, j * LANES:(j + 1) * LANES]
                if col < half:
                    hi_scr[:, col:col + LANES] = _bf16_bits(yj)
                else:
                    s = (col - half) // LANES
                    o_ref[pl.ds(s, tm, stride=ROW_SUBLANES), :] = hi_scr[:, col - half:col - half + LANES] | (_bf16_bits(yj) >> 16)

        @pl.when(i == last)
        def _():
            for t in range(1, MOE_AHEAD + 1):
                tile_wait((i + t) % MOE_SLOTS)

    @pl.when(i >= n_used)
    def _():
        o_ref[...] = jnp.zeros_like(o_ref)


def _moe(layer, tile_expert, row_token, n_used, h2p, wg, bg, wu, bu, wd, bd, tm):
    L, E, D, Fe = wg.shape
    n_tiles = tile_expert.shape[0]
    assert tm % (Fe // MOE_F_CHUNK + D // MOE_D_CHUNK) == 0 and (D // 2) % MOE_D_CHUNK == 0
    wspec = lambda shape: pl.BlockSpec((1, 1) + shape, lambda i, te, rt, nu: (layer, te[i], 0, 0))
    return pl.pallas_call(
        _moe_kernel,
        out_shape=jax.ShapeDtypeStruct((n_tiles * tm * ROW_SUBLANES, LANES), jnp.uint32),
        grid_spec=pltpu.PrefetchScalarGridSpec(
            num_scalar_prefetch=3,
            grid=(n_tiles,),
            in_specs=[pl.BlockSpec(memory_space=pl.ANY),
                      wspec((D, Fe)), wspec((1, Fe)), wspec((D, Fe)), wspec((1, Fe)),
                      wspec((Fe, D)), wspec((1, D))],
            out_specs=pl.BlockSpec((tm * ROW_SUBLANES, LANES), lambda i, te, rt, nu: (i, 0)),
            scratch_shapes=[pltpu.VMEM((MOE_SLOTS, tm * ROW_SUBLANES, LANES), jnp.uint32),
                            pltpu.VMEM((tm, Fe), BF16), pltpu.VMEM((tm, D // 2), jnp.uint32),
                            pltpu.SemaphoreType.DMA((MOE_SLOTS,))],
        ),
        compiler_params=_cparams(("arbitrary",)),
        name="moe_experts",
    )(tile_expert, row_token, n_used, h2p, wg, bg.reshape(L, E, 1, Fe), wu, bu.reshape(L, E, 1, Fe),
      wd, bd.reshape(L, E, 1, D))


def _combine_kernel(dest_ref, yp_hbm, gate_ref, x_ref, gpost_ref, gtf_ref, o_ref, buf, sem):
    i = pl.program_id(0)
    n = pl.num_programs(0)
    tc, D = x_ref.shape
    half = D // 2
    K = TOPK_EXPERTS

    def issue_tile(tile_id, slot):
        base = tile_id * tc * K

        def issue(r, carry):
            for kk in range(K):
                d = dest_ref[base + r * K + kk]
                src = yp_hbm.at[pl.ds(pl.multiple_of(d * ROW_SUBLANES, ROW_SUBLANES), ROW_SUBLANES)]
                dst = buf.at[slot, kk, pl.ds(r * ROW_SUBLANES, ROW_SUBLANES)]
                pltpu.make_async_copy(src, dst, sem.at[slot]).start()
            return carry

        lax.fori_loop(0, tc, issue, 0)

    def tile_wait(slot):
        for kk in range(K):
            pltpu.make_async_copy(yp_hbm.at[pl.ds(0, tc * ROW_SUBLANES)], buf.at[slot, kk], sem.at[slot]).wait()

    @pl.when(i == 0)
    def _():
        issue_tile(0, 0)

    slot = i % 2

    @pl.when(i + 1 < n)
    def _():
        issue_tile(i + 1, 1 - slot)

    tile_wait(slot)
    gates = gate_ref[...]
    gk = [jnp.broadcast_to(gates[:, kk:kk + 1], (tc, LANES)) for kk in range(K)]
    his, los = [], []
    ssq = jnp.zeros((tc, 1), F32)
    for s in range(ROW_SUBLANES):
        hi = lo = None
        for kk in range(K):
            w = buf[slot, kk, pl.ds(s, tc, stride=ROW_SUBLANES), :]
            h, l = gk[kk] * _unpack_hi(w), gk[kk] * _unpack_lo(w)
            hi, lo = (h, l) if kk == 0 else (hi + h, lo + l)
        his.append(hi)
        los.append(lo)
        ssq = ssq + jnp.sum(hi * hi, axis=-1, keepdims=True) + jnp.sum(lo * lo, axis=-1, keepdims=True)
    inv = lax.rsqrt(ssq * (1.0 / D) + RMS_EPS)
    for s in range(ROW_SUBLANES):
        for col, y in ((s * LANES, his[s]), (half + s * LANES, los[s])):
            cs = slice(col, col + LANES)
            o_ref[:, cs] = x_ref[:, cs] + gtf_ref[0, :, cs] * ((y * inv) * gpost_ref[:, cs])


def _combine(dest, yp, gates, x1, gpost, gtf, seq):
    T, D = x1.shape
    tc = 256
    per_b = seq // tc
    return pl.pallas_call(
        _combine_kernel,
        out_shape=jax.ShapeDtypeStruct((T, D), F32),
        grid_spec=pltpu.PrefetchScalarGridSpec(
            num_scalar_prefetch=1,
            grid=(T // tc,),
            in_specs=[pl.BlockSpec(memory_space=pl.ANY),
                      pl.BlockSpec((tc, LANES), lambda i, d: (i, 0)),
                      pl.BlockSpec((tc, D), lambda i, d: (i, 0)),
                      pl.BlockSpec((1, D), lambda i, d: (0, 0)),
                      pl.BlockSpec((1, 1, D), lambda i, d: (i // per_b, 0, 0))],
            out_specs=pl.BlockSpec((tc, D), lambda i, d: (i, 0)),
            scratch_shapes=[pltpu.VMEM((2, TOPK_EXPERTS, tc * ROW_SUBLANES, LANES), jnp.uint32),
                            pltpu.SemaphoreType.DMA((2,))],
        ),
        compiler_params=_cparams(("arbitrary",)),
        name="moe_combine",
    )(dest, yp, gates, x1, gpost, gtf)


def _dest_kernel(idx_ref, rank_ref, cnt_ref, dest_ref, *, tm_rows):
    tm = idx_ref.shape[0]
    tiles = jnp.floor((cnt_ref[0:1, :] + float(tm_rows - 1)) * (1.0 / tm_rows))
    r_i = lax.broadcasted_iota(jnp.int32, (LANES, LANES), 0)
    c_i = lax.broadcasted_iota(jnp.int32, (LANES, LANES), 1)
    before = jnp.where(r_i < c_i, 1.0, 0.0).astype(BF16)
    start = jnp.dot(jnp.broadcast_to(tiles, (8, LANES)).astype(BF16), before,
                    preferred_element_type=F32)[0:1] * float(tm_rows)
    lane = lax.broadcasted_iota(jnp.int32, (tm, LANES), 1)
    idx = idx_ref[...]
    rank = rank_ref[...]
    out = jnp.zeros((tm, LANES), jnp.int32)
    for kk in range(TOPK_EXPERTS):
        base = jnp.sum(jnp.where(lane == idx[:, kk:kk + 1], start, 0.0), axis=-1, keepdims=True)
        out = jnp.where(lane == kk, base.astype(jnp.int32) + rank[:, kk:kk + 1], out)
    dest_ref[...] = out


def _dest(idx, rank, cnt, tm_rows):
    T = idx.shape[0]
    tm = 2048
    assert tm_rows & (tm_rows - 1) == 0
    row = pl.BlockSpec((tm, LANES), lambda i: (i, 0))
    return pl.pallas_call(
        functools.partial(_dest_kernel, tm_rows=tm_rows),
        out_shape=jax.ShapeDtypeStruct((T, LANES), jnp.int32),
        grid=(T // tm,),
        in_specs=[row, row, pl.BlockSpec((8, LANES), lambda i: (0, 0))],
        out_specs=row,
        compiler_params=_cparams(("parallel",)),
        name="route_dest",
    )(idx, rank, cnt)


INVERT_UNROLL = 8


def _invert_kernel(dest_ref, lo_ref, hi_ref, out_ref):
    def fill(j, carry):
        out_ref[j] = 0
        return carry

    for e in range(lo_ref.shape[0]):
        lax.fori_loop(lo_ref[e], hi_ref[e], fill, 0)

    per = INVERT_UNROLL // TOPK_EXPERTS

    def body(b, carry):
        for u in range(INVERT_UNROLL):
            out_ref[dest_ref[b * INVERT_UNROLL + u]] = b * per + u // TOPK_EXPERTS
        return carry

    lax.fori_loop(0, dest_ref.shape[0] // INVERT_UNROLL, body, 0)


def _invert(dest, lo, hi, n_rows):
    smem = pl.BlockSpec(memory_space=pltpu.SMEM)
    return pl.pallas_call(
        _invert_kernel,
        out_shape=jax.ShapeDtypeStruct((n_rows,), jnp.int32),
        in_specs=[smem, smem, smem],
        out_specs=smem,
        name="route_invert",
    )(dest, lo, hi)


def _routing_tables(idx, rank, cnt, n_experts, tm):
    T = idx.shape[0]
    n_tiles = (T * TOPK_EXPERTS) // tm + n_experts
    counts = cnt[0, :n_experts].astype(jnp.int32)
    padded = ((counts + tm - 1) // tm) * tm
    pad_end = jnp.cumsum(padded)
    tile_start = jnp.arange(n_tiles, dtype=jnp.int32) * tm
    tile_expert = jnp.minimum(jnp.sum((pad_end[None, :] <= tile_start[:, None]).astype(jnp.int32), axis=1),
                              n_experts - 1)
    n_used = pad_end[-1:] // tm
    total = jnp.full((1,), n_tiles * tm, jnp.int32)
    lo = jnp.concatenate([pad_end - padded + counts, pad_end[-1:]])
    hi = jnp.concatenate([pad_end, total])
    dest = _dest(idx, rank, cnt, tm)[:, :TOPK_EXPERTS].reshape(-1)
    row_token = _invert(dest, lo, hi, n_tiles * tm)
    return dest, row_token, tile_expert, n_used


def kernel(x, c, w_ada, b_ada, g_pre_mix, g_post_mix, g_pre_ffn, g_post_ffn, w_in, w_out, pool_w, pool_scale,
           conv_w, rel_bias, router_w, router_b, w_gate, b_gate, w_up, b_up, w_down, b_down):
    B, S, D = x.shape
    L = w_ada.shape[0]
    T = B * S
    E = router_w.shape[-1]
    attn_w = D // 2
    heads = attn_w // HEAD_DIM
    moe_tm = 256
    assert D == 2 * ROW_SUBLANES * LANES

    mod = _ada(c, w_ada, b_ada)
    wg_bf, wu_bf, wd_bf = w_gate.astype(BF16), w_up.astype(BF16), w_down.astype(BF16)
    x2 = x.reshape(T, D)
    for l in range(L):
        sh_m, sc_m, gt_m, sh_f, sc_f, gt_f = [mod[l, :, i * D:(i + 1) * D].reshape(B, 1, D) for i in range(N_MOD)]
        row = lambda v: v.reshape(1, D)

        qkv, rest = _inproj(x2, row(g_pre_mix[l]), sc_m, sh_m, w_in[l].astype(BF16), S, attn_w)
        att = _attention(qkv, rel_bias, B, S, heads)
        pc = _poolconv(rest, pool_w[l], pool_scale[l], conv_w[l], B, S)

        wo = w_out[l].astype(BF16)
        rw = jnp.zeros((D, LANES), F32).at[:, :E].set(router_w[l])
        rwh = rw.astype(BF16)
        rwl = (rw - rwh.astype(F32)).astype(BF16)
        rb = jnp.full((1, LANES), NEG_INF, F32).at[0, :E].set(router_b[l])
        x1, h2, idx, gates, rank, cnt = _outproj(
            att, pc, x2, wo[:attn_w], wo[attn_w:], row(g_post_mix[l]), gt_m, row(g_pre_ffn[l]), sc_f, sh_f,
            rwh, rwl, rb, S)

        dest, row_token, tile_expert, n_used = _routing_tables(idx, rank, cnt, E, moe_tm)
        yp = _moe(l, tile_expert, row_token, n_used, h2, wg_bf, b_gate, wu_bf, b_up, wd_bf, b_down, moe_tm)
        x2 = _combine(dest, yp, gates, x1, row(g_post_ffn[l]), gt_f, S)
    return x2.reshape(B, S, D)
```

```python
import functools
import math

import numpy as np
import jax
import jax.numpy as jnp
from jax import lax
from jax.experimental import pallas as pl
from jax.experimental.pallas import tpu as pltpu

F32 = jnp.float32
BF16 = jnp.bfloat16

LANES = 128
HEAD_DIM = 128
MOBA_BLOCK = 256
MOBA_TOPK = 3
REL_BUCKETS = 32
REL_MAX_DISTANCE = 128
POOL_WINDOWS = (2, 4, 8, 16)
GROUP_DIM = 128
CONV_K = 3
TOPK_EXPERTS = 4
SWIGLU_LIMIT = 7.0
SWIGLU_ALPHA = 1.702
N_MOD = 6
RMS_EPS = 1e-6
NEG_INF = -1e30
ATTN_SCALE = HEAD_DIM ** -0.5
LOG2E = math.log2(math.e)

VMEM_LIMIT = 56 * 1024 * 1024


def _cparams(sem):
    return pltpu.CompilerParams(dimension_semantics=sem, vmem_limit_bytes=VMEM_LIMIT)


def _rms(x):
    return x * lax.rsqrt(jnp.mean(x * x, axis=-1, keepdims=True) + RMS_EPS)


ROW_SUBLANES = 8


def _bf16_bits(v):
    return lax.bitcast_convert_type(v.astype(BF16).astype(F32), jnp.uint32)


def _pack_words(hi, lo):
    return _bf16_bits(hi) | (_bf16_bits(lo) >> 16)


def _unpack_hi(w):
    return lax.bitcast_convert_type(w & jnp.uint32(0xFFFF0000), F32)


def _unpack_lo(w):
    return lax.bitcast_convert_type(w << 16, F32)


def _ada_kernel(c_ref, w_ref, b_ref, o_ref):
    c = c_ref[...]
    ca = (c * jax.nn.sigmoid(c)).astype(BF16)
    o_ref[0] = jnp.dot(ca, w_ref[0].astype(BF16), preferred_element_type=F32) + b_ref[0]


def _ada(c, w_ada, b_ada):
    L, D, N = w_ada.shape
    B = c.shape[0]
    tn = 1024
    return pl.pallas_call(
        _ada_kernel,
        out_shape=jax.ShapeDtypeStruct((L, B, N), F32),
        grid=(L, N // tn),
        in_specs=[
            pl.BlockSpec((B, D), lambda l, j: (0, 0)),
            pl.BlockSpec((1, D, tn), lambda l, j: (l, 0, j)),
            pl.BlockSpec((1, 1, tn), lambda l, j: (l, 0, j)),
        ],
        out_specs=pl.BlockSpec((1, B, tn), lambda l, j: (l, 0, j)),
        compiler_params=_cparams(("parallel", "parallel")),
        name="ada_mod",
    )(c, w_ada, b_ada.reshape(L, 1, N))


def _inproj_kernel(x_ref, g_ref, sc_ref, sh_ref, w_ref, qkv_ref, rest_ref, h_scr, *, n_qkv):
    j = pl.program_id(1)

    @pl.when(j == 0)
    def _():
        h = _rms(x_ref[...]) * g_ref[...]
        h = h * (1.0 + sc_ref[0]) + sh_ref[0]
        h_scr[...] = h.astype(BF16)

    acc = jnp.dot(h_scr[...], w_ref[...], preferred_element_type=F32)

    @pl.when(j < n_qkv)
    def _():
        qkv_ref[...] = acc.astype(BF16)

    @pl.when(j >= n_qkv)
    def _():
        rest_ref[...] = acc


def _inproj(x2, g, sc, sh, w_bf, seq, attn_w):
    T, D = x2.shape
    N = w_bf.shape[1]
    tm, tn = 1024, 512
    n_qkv = 3 * attn_w // tn
    n_rest = (N - 3 * attn_w) // tn
    per_b = seq // tm
    return pl.pallas_call(
        functools.partial(_inproj_kernel, n_qkv=n_qkv),
        out_shape=(jax.ShapeDtypeStruct((T, 3 * attn_w), BF16),
                   jax.ShapeDtypeStruct((T, N - 3 * attn_w), F32)),
        grid=(T // tm, n_qkv + n_rest),
        in_specs=[
            pl.BlockSpec((tm, D), lambda i, j: (i, 0)),
            pl.BlockSpec((1, D), lambda i, j: (0, 0)),
            pl.BlockSpec((1, 1, D), lambda i, j: (i // per_b, 0, 0)),
            pl.BlockSpec((1, 1, D), lambda i, j: (i // per_b, 0, 0)),
            pl.BlockSpec((D, tn), lambda i, j: (0, j)),
        ],
        out_specs=(
            pl.BlockSpec((tm, tn), lambda i, j: (i, jnp.minimum(j, n_qkv - 1))),
            pl.BlockSpec((tm, tn), lambda i, j: (i, jnp.maximum(j - n_qkv, 0))),
        ),
        scratch_shapes=[pltpu.VMEM((tm, D), BF16)],
        compiler_params=_cparams(("parallel", "arbitrary")),
        name="mixer_in_proj",
    )(x2, g, sc, sh, w_bf)


def _rel_bucket_np(n):
    n = np.maximum(n, 0)
    max_exact = REL_BUCKETS // 2
    nf = np.maximum(n, max_exact).astype(np.float32)
    large = max_exact + (np.log(nf / np.float32(max_exact)) / np.float32(math.log(REL_MAX_DISTANCE / max_exact))
                         * np.float32(REL_BUCKETS - max_exact)).astype(np.int32)
    large = np.minimum(large, REL_BUCKETS - 1)
    return np.where(n < max_exact, n, large).astype(np.int32)


def _bucket_tables(seq):
    qi = np.arange(MOBA_BLOCK)[:, None]
    ki = np.arange(MOBA_BLOCK)[None, :]
    own = np.where(ki <= qi, _rel_bucket_np(qi - ki), -1)
    prev = _rel_bucket_np(qi - ki + MOBA_BLOCK)
    far = _rel_bucket_np(np.arange(MOBA_BLOCK + 1, max(seq, MOBA_BLOCK + 2)))
    assert np.all(far == far[0])
    return np.stack([own, prev]).astype(np.int32), int(far[0])


def _attn_kernel(tab_ref, q_ref, k_ref, v_ref, bkt_ref, o_ref, bias_scr, *, nblk, far_bucket):
    h = pl.program_id(0)
    b = pl.program_id(1)
    L = MOBA_BLOCK

    @pl.when(b == 0)
    def _():
        for m in range(2):
            bk = bkt_ref[m]
            acc = jnp.full((L, L), NEG_INF, F32)
            for r in range(REL_BUCKETS):
                acc = jnp.where(bk == r, tab_ref[h, r] * LOG2E, acc)
            bias_scr[m] = acc

    far_bias = tab_ref[h, far_bucket] * LOG2E
    q = q_ref[...]
    k = k_ref[...]
    v = v_ref[...]
    dn = (((1,), (1,)), ((), ()))

    kmean = jnp.mean(k.astype(F32).reshape(nblk, L, HEAD_DIM), axis=1)
    kmean = jnp.concatenate([kmean, jnp.zeros((LANES - nblk, HEAD_DIM), F32)], axis=0).astype(BF16)
    gate = lax.dot_general(q, kmean, dn, preferred_element_type=F32)
    lane = lax.broadcasted_iota(jnp.int32, (L, LANES), 1)

    for a in range(nblk):
        qa = q[a * L:(a + 1) * L]
        n = (a + 1) * L
        s = lax.dot_general(qa, k[:n], dn, preferred_element_type=F32) * (ATTN_SCALE * LOG2E)
        hide = None
        if a > MOBA_TOPK:
            ga = jnp.where(lane < a, gate[a * L:(a + 1) * L], NEG_INF)
            cnt = jnp.zeros((L, LANES), F32)
            for i in range(a):
                gi = ga[:, i:i + 1]
                ge = jnp.where(gi >= ga, 1.0, 0.0)
                gt = jnp.where(gi > ga, 1.0, 0.0)
                cnt = cnt + jnp.where(lane > i, ge, gt)
            hide = jnp.where((cnt < float(MOBA_TOPK)) & (lane < a), 0.0, NEG_INF)
            hide_far = hide + far_bias
        pieces = []
        for j in range(a + 1):
            sj = s[:, j * L:(j + 1) * L]
            if j == a:
                sj = sj + bias_scr[0]
            elif j == a - 1:
                sj = sj + bias_scr[1]
                if hide is not None:
                    sj = sj + hide[:, j:j + 1]
            else:
                sj = sj + (far_bias if hide is None else hide_far[:, j:j + 1])
            pieces.append(sj)
        s = pieces[0] if a == 0 else jnp.concatenate(pieces, axis=1)
        m = jnp.max(s, axis=-1, keepdims=True)
        p = jnp.exp2(s - m)
        l = jnp.sum(p, axis=-1, keepdims=True)
        o = jnp.dot(p.astype(BF16), v[:n], preferred_element_type=F32) / l
        o_ref[a * L:(a + 1) * L, :] = o.astype(o_ref.dtype)


def _attention(qkv, rel_bias, batch, seq, heads):
    T = qkv.shape[0]
    nblk = seq // MOBA_BLOCK
    bkt, far_bucket = _bucket_tables(seq)
    table = rel_bias.T.astype(F32)
    return pl.pallas_call(
        functools.partial(_attn_kernel, nblk=nblk, far_bucket=far_bucket),
        out_shape=jax.ShapeDtypeStruct((T, heads * HEAD_DIM), BF16),
        grid=(heads, batch),
        in_specs=[
            pl.BlockSpec(memory_space=pltpu.SMEM),
            pl.BlockSpec((seq, HEAD_DIM), lambda h, b: (b, h)),
            pl.BlockSpec((seq, HEAD_DIM), lambda h, b: (b, heads + h)),
            pl.BlockSpec((seq, HEAD_DIM), lambda h, b: (b, 2 * heads + h)),
            pl.BlockSpec((2, MOBA_BLOCK, MOBA_BLOCK), lambda h, b: (0, 0, 0)),
        ],
        out_specs=pl.BlockSpec((seq, HEAD_DIM), lambda h, b: (b, h)),
        scratch_shapes=[pltpu.VMEM((2, MOBA_BLOCK, MOBA_BLOCK), F32)],
        compiler_params=_cparams(("arbitrary", "arbitrary")),
        name="moba_attention",
    )(table, qkv, qkv, qkv, jnp.asarray(bkt))


def _poolconv_kernel(u_ref, gb_ref, gc_ref, hc_ref, pw_ref, ps_ref, cw_ref, o_ref):
    S = u_ref.shape[0]
    G = len(POOL_WINDOWS)
    row = lax.broadcasted_iota(jnp.int32, (S, GROUP_DIM), 0)

    def shift(x, k):
        return jnp.where(row >= k, pltpu.roll(x, k, 0), 0.0)

    for gi, w in enumerate(POOL_WINDOWS):
        cs = slice(gi * GROUP_DIM, (gi + 1) * GROUP_DIM)
        ug = u_ref[:, cs]
        win = ug
        k = 1
        while k < w:
            win = win + shift(win, k)
            k *= 2
        count = jnp.minimum(row + 1, w).astype(F32)
        d = win / count - ug
        y = jnp.dot(d.astype(BF16), pw_ref[gi].astype(BF16), preferred_element_type=F32)
        o_ref[:, cs] = (y * ps_ref[:, cs]).astype(o_ref.dtype)

        uu = gc_ref[:, cs] * hc_ref[:, cs]
        conv = (cw_ref[0:1, cs] * shift(uu, 2) + cw_ref[1:2, cs] * shift(uu, 1)
                + cw_ref[2:3, cs] * uu)
        oc = slice((G + gi) * GROUP_DIM, (G + gi + 1) * GROUP_DIM)
        o_ref[:, oc] = (gb_ref[:, cs] * conv).astype(o_ref.dtype)


def _poolconv(rest, pool_w, pool_scale, conv_w, batch, seq):
    T = rest.shape[0]
    W = len(POOL_WINDOWS) * GROUP_DIM
    spec = lambda c: pl.BlockSpec((seq, W), lambda b: (b, c))
    return pl.pallas_call(
        _poolconv_kernel,
        out_shape=jax.ShapeDtypeStruct((T, 2 * W), BF16),
        grid=(batch,),
        in_specs=[
            spec(0), spec(1), spec(2), spec(3),
            pl.BlockSpec(pool_w.shape, lambda b: (0, 0, 0)),
            pl.BlockSpec((1, W), lambda b: (0, 0)),
            pl.BlockSpec((CONV_K, W), lambda b: (0, 0)),
        ],
        out_specs=pl.BlockSpec((seq, 2 * W), lambda b: (b, 0)),
        compiler_params=_cparams(("parallel",)),
        name="pool_conv_mixers",
    )(rest, rest, rest, rest, pool_w, pool_scale.reshape(1, W), conv_w)


def _outproj_kernel(att_ref, pc_ref, x_ref, wa_ref, wp_ref, gpost_ref, gtm_ref, gpre_ref, scf_ref, shf_ref,
                    rwh_ref, rwl_ref, rb_ref,
                    x1_ref, h2_ref, idx_ref, gate_ref, rank_ref, cnt_ref, carry_scr):
    i = pl.program_id(0)
    tm = x_ref.shape[0]

    @pl.when(i == 0)
    def _():
        carry_scr[...] = jnp.zeros_like(carry_scr)

    mix = (jnp.dot(att_ref[...], wa_ref[...], preferred_element_type=F32)
           + jnp.dot(pc_ref[...], wp_ref[...], preferred_element_type=F32))
    x1 = x_ref[...] + gtm_ref[0] * (_rms(mix) * gpost_ref[...])
    x1_ref[...] = x1
    h2 = (_rms(x1) * gpre_ref[...]) * (1.0 + scf_ref[0]) + shf_ref[0]
    half = h2.shape[1] // 2
    for s in range(ROW_SUBLANES):
        h2_ref[pl.ds(s, tm, stride=ROW_SUBLANES), :] = _pack_words(h2[:, s * LANES:(s + 1) * LANES],
                                      h2[:, half + s * LANES:half + (s + 1) * LANES])

    hi = h2.astype(BF16)
    lo = (h2 - hi.astype(F32)).astype(BF16)
    logits = (jnp.dot(hi, rwh_ref[...], preferred_element_type=F32)
              + jnp.dot(lo, rwh_ref[...], preferred_element_type=F32)
              + jnp.dot(hi, rwl_ref[...], preferred_element_type=F32)) + rb_ref[...]

    lane = lax.broadcasted_iota(jnp.int32, (tm, LANES), 1)
    work = logits
    mem = jnp.zeros((tm, LANES), F32)
    vals, hots = [], []
    idx_out = jnp.zeros((tm, LANES), jnp.int32)
    for kk in range(TOPK_EXPERTS):
        m = jnp.max(work, axis=-1, keepdims=True)
        ik = jnp.min(jnp.where(work == m, lane, LANES), axis=-1, keepdims=True)
        hot = lane == ik
        vals.append(m)
        hots.append(hot)
        idx_out = jnp.where(lane == kk, ik, idx_out)
        mem = jnp.where(hot, 1.0, mem)
        work = jnp.where(hot, -jnp.inf, work)
    idx_ref[...] = idx_out

    es = [jnp.exp(vk - vals[0]) for vk in vals]
    denom = es[0]
    for e in es[1:]:
        denom = denom + e
    gates = jnp.zeros((tm, LANES), F32)
    for kk in range(TOPK_EXPERTS):
        gates = jnp.where(lane == kk, es[kk] / denom, gates)
    gate_ref[...] = gates

    r_i = lax.broadcasted_iota(jnp.int32, (tm, tm), 0)
    c_i = lax.broadcasted_iota(jnp.int32, (tm, tm), 1)
    tri = jnp.where(c_i < r_i, 1.0, 0.0).astype(BF16)
    before = jnp.dot(tri, mem.astype(BF16), preferred_element_type=F32) + carry_scr[0:1, :]
    ranks = jnp.zeros((tm, LANES), jnp.int32)
    for kk in range(TOPK_EXPERTS):
        rk = jnp.sum(jnp.where(hots[kk], before, 0.0), axis=-1, keepdims=True)
        ranks = jnp.where(lane == kk, rk.astype(jnp.int32), ranks)
    rank_ref[...] = ranks
    carry = carry_scr[...] + jnp.sum(mem, axis=0, keepdims=True)
    carry_scr[...] = carry
    cnt_ref[...] = carry


def _outproj(att, pc, x2, wa, wp, gpost, gtm, gpre, scf, shf, rwh, rwl, rb, seq):
    T, D = x2.shape
    tm = 512
    per_b = seq // tm
    row = lambda w: pl.BlockSpec((tm, w), lambda i: (i, 0))
    const = lambda shape: pl.BlockSpec(shape, lambda i: tuple(0 for _ in shape), pipeline_mode=pl.Buffered(1))
    perb = pl.BlockSpec((1, 1, D), lambda i: (i // per_b, 0, 0))
    return pl.pallas_call(
        _outproj_kernel,
        out_shape=(jax.ShapeDtypeStruct((T, D), F32), jax.ShapeDtypeStruct((T * ROW_SUBLANES, LANES), jnp.uint32),
                   jax.ShapeDtypeStruct((T, LANES), jnp.int32), jax.ShapeDtypeStruct((T, LANES), F32),
                   jax.ShapeDtypeStruct((T, LANES), jnp.int32), jax.ShapeDtypeStruct((8, LANES), F32)),
        grid=(T // tm,),
        in_specs=[row(att.shape[1]), row(pc.shape[1]), row(D), const(wa.shape), const(wp.shape),
                  const((1, D)), perb, const((1, D)), perb, perb,
                  const(rwh.shape), const(rwl.shape), const((1, LANES))],
        out_specs=(row(D), pl.BlockSpec((tm * ROW_SUBLANES, LANES), lambda i: (i, 0)),
                   row(LANES), row(LANES), row(LANES), pl.BlockSpec((8, LANES), lambda i: (0, 0))),
        scratch_shapes=[pltpu.VMEM((8, LANES), F32)],
        compiler_params=_cparams(("arbitrary",)),
        name="out_proj_router",
    )(att, pc, x2, wa, wp, gpost, gtm, gpre, scf, shf, rwh, rwl, rb)


MOE_F_CHUNK = 256
MOE_D_CHUNK = 512
MOE_AHEAD = 2
MOE_SLOTS = MOE_AHEAD + 1


def _moe_kernel(te_ref, rt_ref, nu_ref, h_hbm, wg_ref, bg_ref, wu_ref, bu_ref, wd_ref, bd_ref, o_ref,
                xbuf, act_scr, hi_scr, sem):
    i = pl.program_id(0)
    tm = act_scr.shape[0]
    Fe = act_scr.shape[1]
    D = wd_ref.shape[-1]
    half = D // 2
    n_used = nu_ref[0]
    last = n_used - 1

    def row_copy(tile_id, r, slot):
        tok = rt_ref[tile_id * tm + r]
        src = h_hbm.at[pl.ds(pl.multiple_of(tok * ROW_SUBLANES, ROW_SUBLANES), ROW_SUBLANES)]
        return pltpu.make_async_copy(src, xbuf.at[slot, pl.ds(r * ROW_SUBLANES, ROW_SUBLANES)], sem.at[slot])

    def tile_wait(slot):
        pltpu.make_async_copy(h_hbm.at[pl.ds(0, tm * ROW_SUBLANES)], xbuf.at[slot], sem.at[slot]).wait()

    @pl.when(i == 0)
    def _():
        for t in range(MOE_AHEAD):
            def issue(r, carry, t=t):
                row_copy(jnp.minimum(t, last), r, t).start()
                return carry
            lax.fori_loop(0, tm, issue, 0)

    @pl.when(i < n_used)
    def _():
        slot = i % MOE_SLOTS
        nslot = (i + MOE_AHEAD) % MOE_SLOTS
        ntile = jnp.minimum(i + MOE_AHEAD, last)
        n_f, n_d = Fe // MOE_F_CHUNK, D // MOE_D_CHUNK
        per = tm // (n_f + n_d)

        def issue_part(p):
            for r in range(p * per, (p + 1) * per):
                row_copy(ntile, r, nslot).start()

        tile_wait(slot)
        words = [xbuf[slot, pl.ds(s, tm, stride=ROW_SUBLANES), :] for s in range(ROW_SUBLANES)]
        x = jnp.concatenate([_unpack_hi(w).astype(BF16) for w in words]
                            + [_unpack_lo(w).astype(BF16) for w in words], axis=1)
        for c in range(n_f):
            issue_part(c)
            cs = slice(c * MOE_F_CHUNK, (c + 1) * MOE_F_CHUNK)
            g = jnp.dot(x, wg_ref[0, 0, :, cs], preferred_element_type=F32) + bg_ref[0, 0, :, cs]
            u = jnp.dot(x, wu_ref[0, 0, :, cs], preferred_element_type=F32) + bu_ref[0, 0, :, cs]
            g = jnp.minimum(g, SWIGLU_LIMIT)
            u = jnp.clip(u, -SWIGLU_LIMIT, SWIGLU_LIMIT)
            act_scr[:, cs] = (g * jax.nn.sigmoid(SWIGLU_ALPHA * g) * (u + 1.0)).astype(BF16)
        act = act_scr[...]
        for c in range(n_d):
            issue_part(n_f + c)
            c0 = c * MOE_D_CHUNK
            y = (jnp.dot(act, wd_ref[0, 0, :, c0:c0 + MOE_D_CHUNK], preferred_element_type=F32)
                 + bd_ref[0, 0, :, c0:c0 + MOE_D_CHUNK])
            for j in range(MOE_D_CHUNK // LANES):
                col = c0 + j * LANES
                yj = y[:, j * LANES:(j + 1) * LANES]
                if col < half:
                    hi_scr[:, col:col + LANES] = _bf16_bits(yj)
                else:
                    s = (col - half) // LANES
                    o_ref[pl.ds(s, tm, stride=ROW_SUBLANES), :] = hi_scr[:, col - half:col - half + LANES] | (_bf16_bits(yj) >> 16)

        @pl.when(i == last)
        def _():
            for t in range(1, MOE_AHEAD + 1):
                tile_wait((i + t) % MOE_SLOTS)

    @pl.when(i >= n_used)
    def _():
        o_ref[...] = jnp.zeros_like(o_ref)


def _moe(layer, tile_expert, row_token, n_used, h2p, wg, bg, wu, bu, wd, bd, tm):
    L, E, D, Fe = wg.shape
    n_tiles = tile_expert.shape[0]
    assert tm % (Fe // MOE_F_CHUNK + D // MOE_D_CHUNK) == 0 and (D // 2) % MOE_D_CHUNK == 0
    wspec = lambda shape: pl.BlockSpec((1, 1) + shape, lambda i, te, rt, nu: (layer, te[i], 0, 0))
    return pl.pallas_call(
        _moe_kernel,
        out_shape=jax.ShapeDtypeStruct((n_tiles * tm * ROW_SUBLANES, LANES), jnp.uint32),
        grid_spec=pltpu.PrefetchScalarGridSpec(
            num_scalar_prefetch=3,
            grid=(n_tiles,),
            in_specs=[pl.BlockSpec(memory_space=pl.ANY),
                      wspec((D, Fe)), wspec((1, Fe)), wspec((D, Fe)), wspec((1, Fe)),
                      wspec((Fe, D)), wspec((1, D))],
            out_specs=pl.BlockSpec((tm * ROW_SUBLANES, LANES), lambda i, te, rt, nu: (i, 0)),
            scratch_shapes=[pltpu.VMEM((MOE_SLOTS, tm * ROW_SUBLANES, LANES), jnp.uint32),
                            pltpu.VMEM((tm, Fe), BF16), pltpu.VMEM((tm, D // 2), jnp.uint32),
                            pltpu.SemaphoreType.DMA((MOE_SLOTS,))],
        ),
        compiler_params=_cparams(("arbitrary",)),
        name="moe_experts",
    )(tile_expert, row_token, n_used, h2p, wg, bg.reshape(L, E, 1, Fe), wu, bu.reshape(L, E, 1, Fe),
      wd, bd.reshape(L, E, 1, D))


def _combine_kernel(dest_ref, yp_hbm, gate_ref, x_ref, gpost_ref, gtf_ref, o_ref, buf, sem):
    i = pl.program_id(0)
    n = pl.num_programs(0)
    tc, D = x_ref.shape
    half = D // 2
    K = TOPK_EXPERTS

    def issue_tile(tile_id, slot):
        base = tile_id * tc * K

        def issue(r, carry):
            for kk in range(K):
                d = dest_ref[base + r * K + kk]
                src = yp_hbm.at[pl.ds(pl.multiple_of(d * ROW_SUBLANES, ROW_SUBLANES), ROW_SUBLANES)]
                dst = buf.at[slot, kk, pl.ds(r * ROW_SUBLANES, ROW_SUBLANES)]
                pltpu.make_async_copy(src, dst, sem.at[slot]).start(priority=kk % 2)
            return carry

        lax.fori_loop(0, tc, issue, 0)

    def tile_wait(slot):
        for kk in range(K):
            pltpu.make_async_copy(yp_hbm.at[pl.ds(0, tc * ROW_SUBLANES)], buf.at[slot, kk], sem.at[slot]).wait()

    @pl.when(i == 0)
    def _():
        issue_tile(0, 0)

    slot = i % 2

    @pl.when(i + 1 < n)
    def _():
        issue_tile(i + 1, 1 - slot)

    tile_wait(slot)
    gates = gate_ref[...]
    gk = [jnp.broadcast_to(gates[:, kk:kk + 1], (tc, LANES)) for kk in range(K)]
    his, los = [], []
    ssq = jnp.zeros((tc, 1), F32)
    for s in range(ROW_SUBLANES):
        hi = lo = None
        for kk in range(K):
            w = buf[slot, kk, pl.ds(s, tc, stride=ROW_SUBLANES), :]
            h, l = gk[kk] * _unpack_hi(w), gk[kk] * _unpack_lo(w)
            hi, lo = (h, l) if kk == 0 else (hi + h, lo + l)
        his.append(hi)
        los.append(lo)
        ssq = ssq + jnp.sum(hi * hi, axis=-1, keepdims=True) + jnp.sum(lo * lo, axis=-1, keepdims=True)
    inv = lax.rsqrt(ssq * (1.0 / D) + RMS_EPS)
    for s in range(ROW_SUBLANES):
        for col, y in ((s * LANES, his[s]), (half + s * LANES, los[s])):
            cs = slice(col, col + LANES)
            o_ref[:, cs] = x_ref[:, cs] + gtf_ref[0, :, cs] * ((y * inv) * gpost_ref[:, cs])


def _combine(dest, yp, gates, x1, gpost, gtf, seq):
    T, D = x1.shape
    tc = 256
    per_b = seq // tc
    return pl.pallas_call(
        _combine_kernel,
        out_shape=jax.ShapeDtypeStruct((T, D), F32),
        grid_spec=pltpu.PrefetchScalarGridSpec(
            num_scalar_prefetch=1,
            grid=(T // tc,),
            in_specs=[pl.BlockSpec(memory_space=pl.ANY),
                      pl.BlockSpec((tc, LANES), lambda i, d: (i, 0)),
                      pl.BlockSpec((tc, D), lambda i, d: (i, 0)),
                      pl.BlockSpec((1, D), lambda i, d: (0, 0)),
                      pl.BlockSpec((1, 1, D), lambda i, d: (i // per_b, 0, 0))],
            out_specs=pl.BlockSpec((tc, D), lambda i, d: (i, 0)),
            scratch_shapes=[pltpu.VMEM((2, TOPK_EXPERTS, tc * ROW_SUBLANES, LANES), jnp.uint32),
                            pltpu.SemaphoreType.DMA((2,))],
        ),
        compiler_params=_cparams(("arbitrary",)),
        name="moe_combine",
    )(dest, yp, gates, x1, gpost, gtf)


def _dest_kernel(idx_ref, rank_ref, cnt_ref, dest_ref, *, tm_rows):
    tm = idx_ref.shape[0]
    tiles = jnp.floor((cnt_ref[0:1, :] + float(tm_rows - 1)) * (1.0 / tm_rows))
    r_i = lax.broadcasted_iota(jnp.int32, (LANES, LANES), 0)
    c_i = lax.broadcasted_iota(jnp.int32, (LANES, LANES), 1)
    before = jnp.where(r_i < c_i, 1.0, 0.0).astype(BF16)
    start = jnp.dot(jnp.broadcast_to(tiles, (8, LANES)).astype(BF16), before,
                    preferred_element_type=F32)[0:1] * float(tm_rows)
    lane = lax.broadcasted_iota(jnp.int32, (tm, LANES), 1)
    idx = idx_ref[...]
    rank = rank_ref[...]
    out = jnp.zeros((tm, LANES), jnp.int32)
    for kk in range(TOPK_EXPERTS):
        base = jnp.sum(jnp.where(lane == idx[:, kk:kk + 1], start, 0.0), axis=-1, keepdims=True)
        out = jnp.where(lane == kk, base.astype(jnp.int32) + rank[:, kk:kk + 1], out)
    dest_ref[...] = out


def _dest(idx, rank, cnt, tm_rows):
    T = idx.shape[0]
    tm = 2048
    assert tm_rows & (tm_rows - 1) == 0
    row = pl.BlockSpec((tm, LANES), lambda i: (i, 0))
    return pl.pallas_call(
        functools.partial(_dest_kernel, tm_rows=tm_rows),
        out_shape=jax.ShapeDtypeStruct((T, LANES), jnp.int32),
        grid=(T // tm,),
        in_specs=[row, row, pl.BlockSpec((8, LANES), lambda i: (0, 0))],
        out_specs=row,
        compiler_params=_cparams(("parallel",)),
        name="route_dest",
    )(idx, rank, cnt)


INVERT_UNROLL = 8


def _invert_kernel(dest_ref, lo_ref, hi_ref, out_ref):
    def fill(j, carry):
        out_ref[j] = 0
        return carry

    for e in range(lo_ref.shape[0]):
        lax.fori_loop(lo_ref[e], hi_ref[e], fill, 0)

    per = INVERT_UNROLL // TOPK_EXPERTS

    def body(b, carry):
        for u in range(INVERT_UNROLL):
            out_ref[dest_ref[b * INVERT_UNROLL + u]] = b * per + u // TOPK_EXPERTS
        return carry

    lax.fori_loop(0, dest_ref.shape[0] // INVERT_UNROLL, body, 0)


def _invert(dest, lo, hi, n_rows):
    smem = pl.BlockSpec(memory_space=pltpu.SMEM)
    return pl.pallas_call(
        _invert_kernel,
        out_shape=jax.ShapeDtypeStruct((n_rows,), jnp.int32),
        in_specs=[smem, smem, smem],
        out_specs=smem,
        name="route_invert",
    )(dest, lo, hi)


def _routing_tables(idx, rank, cnt, n_experts, tm):
    T = idx.shape[0]
    n_tiles = (T * TOPK_EXPERTS) // tm + n_experts
    counts = cnt[0, :n_experts].astype(jnp.int32)
    padded = ((counts + tm - 1) // tm) * tm
    pad_end = jnp.cumsum(padded)
    tile_start = jnp.arange(n_tiles, dtype=jnp.int32) * tm
    tile_expert = jnp.minimum(jnp.sum((pad_end[None, :] <= tile_start[:, None]).astype(jnp.int32), axis=1),
                              n_experts - 1)
    n_used = pad_end[-1:] // tm
    total = jnp.full((1,), n_tiles * tm, jnp.int32)
    lo = jnp.concatenate([pad_end - padded + counts, pad_end[-1:]])
    hi = jnp.concatenate([pad_end, total])
    dest = _dest(idx, rank, cnt, tm)[:, :TOPK_EXPERTS].reshape(-1)
    row_token = _invert(dest, lo, hi, n_tiles * tm)
    return dest, row_token, tile_expert, n_used


def kernel(x, c, w_ada, b_ada, g_pre_mix, g_post_mix, g_pre_ffn, g_post_ffn, w_in, w_out, pool_w, pool_scale,
           conv_w, rel_bias, router_w, router_b, w_gate, b_gate, w_up, b_up, w_down, b_down):
    B, S, D = x.shape
    L = w_ada.shape[0]
    T = B * S
    E = router_w.shape[-1]
    attn_w = D // 2
    heads = attn_w // HEAD_DIM
    moe_tm = 256
    assert D == 2 * ROW_SUBLANES * LANES

    mod = _ada(c, w_ada, b_ada)
    wg_bf, wu_bf, wd_bf = w_gate.astype(BF16), w_up.astype(BF16), w_down.astype(BF16)
    x2 = x.reshape(T, D)
    for l in range(L):
        sh_m, sc_m, gt_m, sh_f, sc_f, gt_f = [mod[l, :, i * D:(i + 1) * D].reshape(B, 1, D) for i in range(N_MOD)]
        row = lambda v: v.reshape(1, D)

        qkv, rest = _inproj(x2, row(g_pre_mix[l]), sc_m, sh_m, w_in[l].astype(BF16), S, attn_w)
        att = _attention(qkv, rel_bias, B, S, heads)
        pc = _poolconv(rest, pool_w[l], pool_scale[l], conv_w[l], B, S)

        wo = w_out[l].astype(BF16)
        rw = jnp.zeros((D, LANES), F32).at[:, :E].set(router_w[l])
        rwh = rw.astype(BF16)
        rwl = (rw - rwh.astype(F32)).astype(BF16)
        rb = jnp.full((1, LANES), NEG_INF, F32).at[0, :E].set(router_b[l])
        x1, h2, idx, gates, rank, cnt = _outproj(
            att, pc, x2, wo[:attn_w], wo[attn_w:], row(g_post_mix[l]), gt_m, row(g_pre_ffn[l]), sc_f, sh_f,
            rwh, rwl, rb, S)

        dest, row_token, tile_expert, n_used = _routing_tables(idx, rank, cnt, E, moe_tm)
        yp = _moe(l, tile_expert, row_token, n_used, h2, wg_bf, b_gate, wu_bf, b_up, wd_bf, b_down, moe_tm)
        x2 = _combine(dest, yp, gates, x1, row(g_post_ffn[l]), gt_f, S)
    return x2.reshape(B, S, D)
```

```python
import functools
import math

import numpy as np
import jax
import jax.numpy as jnp
from jax import lax
from jax.experimental import pallas as pl
from jax.experimental.pallas import tpu as pltpu

F32 = jnp.float32
BF16 = jnp.bfloat16

LANES = 128
HEAD_DIM = 128
MOBA_BLOCK = 256
MOBA_TOPK = 3
REL_BUCKETS = 32
REL_MAX_DISTANCE = 128
POOL_WINDOWS = (2, 4, 8, 16)
GROUP_DIM = 128
CONV_K = 3
TOPK_EXPERTS = 4
SWIGLU_LIMIT = 7.0
SWIGLU_ALPHA = 1.702
N_MOD = 6
RMS_EPS = 1e-6
NEG_INF = -1e30
ATTN_SCALE = HEAD_DIM ** -0.5
LOG2E = math.log2(math.e)

VMEM_LIMIT = 56 * 1024 * 1024


def _cparams(sem):
    return pltpu.CompilerParams(dimension_semantics=sem, vmem_limit_bytes=VMEM_LIMIT)


def _rms(x):
    return x * lax.rsqrt(jnp.mean(x * x, axis=-1, keepdims=True) + RMS_EPS)


ROW_SUBLANES = 8


def _bf16_bits(v):
    return lax.bitcast_convert_type(v.astype(BF16).astype(F32), jnp.uint32)


def _pack_words(hi, lo):
    return _bf16_bits(hi) | (_bf16_bits(lo) >> 16)


def _unpack_hi(w):
    return lax.bitcast_convert_type(w & jnp.uint32(0xFFFF0000), F32)


def _unpack_lo(w):
    return lax.bitcast_convert_type(w << 16, F32)


def _ada_kernel(c_ref, w_ref, b_ref, o_ref):
    c = c_ref[...]
    ca = (c * jax.nn.sigmoid(c)).astype(BF16)
    o_ref[0] = jnp.dot(ca, w_ref[0].astype(BF16), preferred_element_type=F32) + b_ref[0]


def _ada(c, w_ada, b_ada):
    L, D, N = w_ada.shape
    B = c.shape[0]
    tn = 1024
    return pl.pallas_call(
        _ada_kernel,
        out_shape=jax.ShapeDtypeStruct((L, B, N), F32),
        grid=(L, N // tn),
        in_specs=[
            pl.BlockSpec((B, D), lambda l, j: (0, 0)),
            pl.BlockSpec((1, D, tn), lambda l, j: (l, 0, j)),
            pl.BlockSpec((1, 1, tn), lambda l, j: (l, 0, j)),
        ],
        out_specs=pl.BlockSpec((1, B, tn), lambda l, j: (l, 0, j)),
        compiler_params=_cparams(("parallel", "parallel")),
        name="ada_mod",
    )(c, w_ada, b_ada.reshape(L, 1, N))


def _inproj_kernel(x_ref, g_ref, sc_ref, sh_ref, w_ref, qkv_ref, rest_ref, h_scr, *, n_qkv):
    j = pl.program_id(1)

    @pl.when(j == 0)
    def _():
        h = _rms(x_ref[...]) * g_ref[...]
        h = h * (1.0 + sc_ref[0]) + sh_ref[0]
        h_scr[...] = h.astype(BF16)

    acc = jnp.dot(h_scr[...], w_ref[...], preferred_element_type=F32)

    @pl.when(j < n_qkv)
    def _():
        qkv_ref[...] = acc.astype(BF16)

    @pl.when(j >= n_qkv)
    def _():
        rest_ref[...] = acc


def _inproj(x2, g, sc, sh, w_bf, seq, attn_w):
    T, D = x2.shape
    N = w_bf.shape[1]
    tm, tn = 1024, 512
    n_qkv = 3 * attn_w // tn
    n_rest = (N - 3 * attn_w) // tn
    per_b = seq // tm
    return pl.pallas_call(
        functools.partial(_inproj_kernel, n_qkv=n_qkv),
        out_shape=(jax.ShapeDtypeStruct((T, 3 * attn_w), BF16),
                   jax.ShapeDtypeStruct((T, N - 3 * attn_w), F32)),
        grid=(T // tm, n_qkv + n_rest),
        in_specs=[
            pl.BlockSpec((tm, D), lambda i, j: (i, 0)),
            pl.BlockSpec((1, D), lambda i, j: (0, 0)),
            pl.BlockSpec((1, 1, D), lambda i, j: (i // per_b, 0, 0)),
            pl.BlockSpec((1, 1, D), lambda i, j: (i // per_b, 0, 0)),
            pl.BlockSpec((D, tn), lambda i, j: (0, j)),
        ],
        out_specs=(
            pl.BlockSpec((tm, tn), lambda i, j: (i, jnp.minimum(j, n_qkv - 1))),
            pl.BlockSpec((tm, tn), lambda i, j: (i, jnp.maximum(j - n_qkv, 0))),
        ),
        scratch_shapes=[pltpu.VMEM((tm, D), BF16)],
        compiler_params=_cparams(("parallel", "arbitrary")),
        name="mixer_in_proj",
    )(x2, g, sc, sh, w_bf)


def _rel_bucket_np(n):
    n = np.maximum(n, 0)
    max_exact = REL_BUCKETS // 2
    nf = np.maximum(n, max_exact).astype(np.float32)
    large = max_exact + (np.log(nf / np.float32(max_exact)) / np.float32(math.log(REL_MAX_DISTANCE / max_exact))
                         * np.float32(REL_BUCKETS - max_exact)).astype(np.int32)
    large = np.minimum(large, REL_BUCKETS - 1)
    return np.where(n < max_exact, n, large).astype(np.int32)


def _bucket_tables(seq):
    qi = np.arange(MOBA_BLOCK)[:, None]
    ki = np.arange(MOBA_BLOCK)[None, :]
    own = np.where(ki <= qi, _rel_bucket_np(qi - ki), -1)
    prev = _rel_bucket_np(qi - ki + MOBA_BLOCK)
    far = _rel_bucket_np(np.arange(MOBA_BLOCK + 1, max(seq, MOBA_BLOCK + 2)))
    assert np.all(far == far[0])
    return np.stack([own, prev]).astype(np.int32), int(far[0])


def _attn_kernel(tab_ref, q_ref, k_ref, v_ref, bkt_ref, o_ref, bias_scr, *, nblk, far_bucket):
    h = pl.program_id(0)
    b = pl.program_id(1)
    L = MOBA_BLOCK

    @pl.when(b == 0)
    def _():
        for m in range(2):
            bk = bkt_ref[m]
            acc = jnp.full((L, L), NEG_INF, F32)
            for r in range(REL_BUCKETS):
                acc = jnp.where(bk == r, tab_ref[h, r] * LOG2E, acc)
            bias_scr[m] = acc

    far_bias = tab_ref[h, far_bucket] * LOG2E
    q = q_ref[...]
    k = k_ref[...]
    v = v_ref[...]
    dn = (((1,), (1,)), ((), ()))

    kmean = jnp.mean(k.astype(F32).reshape(nblk, L, HEAD_DIM), axis=1)
    kmean = jnp.concatenate([kmean, jnp.zeros((LANES - nblk, HEAD_DIM), F32)], axis=0).astype(BF16)
    gate = lax.dot_general(q, kmean, dn, preferred_element_type=F32)
    lane = lax.broadcasted_iota(jnp.int32, (L, LANES), 1)

    for a in range(nblk):
        qa = q[a * L:(a + 1) * L]
        n = (a + 1) * L
        s = lax.dot_general(qa, k[:n], dn, preferred_element_type=F32) * (ATTN_SCALE * LOG2E)
        hide = None
        if a > MOBA_TOPK:
            ga = jnp.where(lane < a, gate[a * L:(a + 1) * L], NEG_INF)
            cnt = jnp.zeros((L, LANES), F32)
            for i in range(a):
                gi = ga[:, i:i + 1]
                ge = jnp.where(gi >= ga, 1.0, 0.0)
                gt = jnp.where(gi > ga, 1.0, 0.0)
                cnt = cnt + jnp.where(lane > i, ge, gt)
            hide = jnp.where((cnt < float(MOBA_TOPK)) & (lane < a), 0.0, NEG_INF)
            hide_far = hide + far_bias
        pieces = []
        for j in range(a + 1):
            sj = s[:, j * L:(j + 1) * L]
            if j == a:
                sj = sj + bias_scr[0]
            elif j == a - 1:
                sj = sj + bias_scr[1]
                if hide is not None:
                    sj = sj + hide[:, j:j + 1]
            else:
                sj = sj + (far_bias if hide is None else hide_far[:, j:j + 1])
            pieces.append(sj)
        s = pieces[0] if a == 0 else jnp.concatenate(pieces, axis=1)
        m = jnp.max(s, axis=-1, keepdims=True)
        p = jnp.exp2(s - m)
        l = jnp.sum(p, axis=-1, keepdims=True)
        o = jnp.dot(p.astype(BF16), v[:n], preferred_element_type=F32) / l
        o_ref[a * L:(a + 1) * L, :] = o.astype(o_ref.dtype)


def _attention(qkv, rel_bias, batch, seq, heads):
    T = qkv.shape[0]
    nblk = seq // MOBA_BLOCK
    bkt, far_bucket = _bucket_tables(seq)
    table = rel_bias.T.astype(F32)
    return pl.pallas_call(
        functools.partial(_attn_kernel, nblk=nblk, far_bucket=far_bucket),
        out_shape=jax.ShapeDtypeStruct((T, heads * HEAD_DIM), BF16),
        grid=(heads, batch),
        in_specs=[
            pl.BlockSpec(memory_space=pltpu.SMEM),
            pl.BlockSpec((seq, HEAD_DIM), lambda h, b: (b, h)),
            pl.BlockSpec((seq, HEAD_DIM), lambda h, b: (b, heads + h)),
            pl.BlockSpec((seq, HEAD_DIM), lambda h, b: (b, 2 * heads + h)),
            pl.BlockSpec((2, MOBA_BLOCK, MOBA_BLOCK), lambda h, b: (0, 0, 0)),
        ],
        out_specs=pl.BlockSpec((seq, HEAD_DIM), lambda h, b: (b, h)),
        scratch_shapes=[pltpu.VMEM((2, MOBA_BLOCK, MOBA_BLOCK), F32)],
        compiler_params=_cparams(("arbitrary", "arbitrary")),
        name="moba_attention",
    )(table, qkv, qkv, qkv, jnp.asarray(bkt))


def _poolconv_kernel(u_ref, gb_ref, gc_ref, hc_ref, pw_ref, ps_ref, cw_ref, o_ref):
    S = u_ref.shape[0]
    G = len(POOL_WINDOWS)
    row = lax.broadcasted_iota(jnp.int32, (S, GROUP_DIM), 0)

    def shift(x, k):
        return jnp.where(row >= k, pltpu.roll(x, k, 0), 0.0)

    for gi, w in enumerate(POOL_WINDOWS):
        cs = slice(gi * GROUP_DIM, (gi + 1) * GROUP_DIM)
        ug = u_ref[:, cs]
        win = ug
        k = 1
        while k < w:
            win = win + shift(win, k)
            k *= 2
        count = jnp.minimum(row + 1, w).astype(F32)
        d = win / count - ug
        y = jnp.dot(d.astype(BF16), pw_ref[gi].astype(BF16), preferred_element_type=F32)
        o_ref[:, cs] = (y * ps_ref[:, cs]).astype(o_ref.dtype)

        uu = gc_ref[:, cs] * hc_ref[:, cs]
        conv = (cw_ref[0:1, cs] * shift(uu, 2) + cw_ref[1:2, cs] * shift(uu, 1)
                + cw_ref[2:3, cs] * uu)
        oc = slice((G + gi) * GROUP_DIM, (G + gi + 1) * GROUP_DIM)
        o_ref[:, oc] = (gb_ref[:, cs] * conv).astype(o_ref.dtype)


def _poolconv(rest, pool_w, pool_scale, conv_w, batch, seq):
    T = rest.shape[0]
    W = len(POOL_WINDOWS) * GROUP_DIM
    spec = lambda c: pl.BlockSpec((seq, W), lambda b: (b, c))
    return pl.pallas_call(
        _poolconv_kernel,
        out_shape=jax.ShapeDtypeStruct((T, 2 * W), BF16),
        grid=(batch,),
        in_specs=[
            spec(0), spec(1), spec(2), spec(3),
            pl.BlockSpec(pool_w.shape, lambda b: (0, 0, 0)),
            pl.BlockSpec((1, W), lambda b: (0, 0)),
            pl.BlockSpec((CONV_K, W), lambda b: (0, 0)),
        ],
        out_specs=pl.BlockSpec((seq, 2 * W), lambda b: (b, 0)),
        compiler_params=_cparams(("parallel",)),
        name="pool_conv_mixers",
    )(rest, rest, rest, rest, pool_w, pool_scale.reshape(1, W), conv_w)


def _outproj_kernel(att_ref, pc_ref, x_ref, wa_ref, wp_ref, gpost_ref, gtm_ref, gpre_ref, scf_ref, shf_ref,
                    rwh_ref, rwl_ref, rb_ref,
                    x1_ref, h2_ref, idx_ref, gate_ref, rank_ref, cnt_ref, carry_scr):
    i = pl.program_id(0)
    tm = x_ref.shape[0]

    @pl.when(i == 0)
    def _():
        carry_scr[...] = jnp.zeros_like(carry_scr)

    mix = (jnp.dot(att_ref[...], wa_ref[...], preferred_element_type=F32)
           + jnp.dot(pc_ref[...], wp_ref[...], preferred_element_type=F32))
    x1 = x_ref[...] + gtm_ref[0] * (_rms(mix) * gpost_ref[...])
    x1_ref[...] = x1
    h2 = (_rms(x1) * gpre_ref[...]) * (1.0 + scf_ref[0]) + shf_ref[0]
    half = h2.shape[1] // 2
    for s in range(ROW_SUBLANES):
        h2_ref[pl.ds(s, tm, stride=ROW_SUBLANES), :] = _pack_words(h2[:, s * LANES:(s + 1) * LANES],
                                      h2[:, half + s * LANES:half + (s + 1) * LANES])

    hi = h2.astype(BF16)
    lo = (h2 - hi.astype(F32)).astype(BF16)
    logits = (jnp.dot(hi, rwh_ref[...], preferred_element_type=F32)
              + jnp.dot(lo, rwh_ref[...], preferred_element_type=F32)
              + jnp.dot(hi, rwl_ref[...], preferred_element_type=F32)) + rb_ref[...]

    lane = lax.broadcasted_iota(jnp.int32, (tm, LANES), 1)
    work = logits
    mem = jnp.zeros((tm, LANES), F32)
    vals, hots = [], []
    idx_out = jnp.zeros((tm, LANES), jnp.int32)
    for kk in range(TOPK_EXPERTS):
        m = jnp.max(work, axis=-1, keepdims=True)
        ik = jnp.min(jnp.where(work == m, lane, LANES), axis=-1, keepdims=True)
        hot = lane == ik
        vals.append(m)
        hots.append(hot)
        idx_out = jnp.where(lane == kk, ik, idx_out)
        mem = jnp.where(hot, 1.0, mem)
        work = jnp.where(hot, -jnp.inf, work)
    idx_ref[...] = idx_out

    es = [jnp.exp(vk - vals[0]) for vk in vals]
    denom = es[0]
    for e in es[1:]:
        denom = denom + e
    gates = jnp.zeros((tm, LANES), F32)
    for kk in range(TOPK_EXPERTS):
        gates = jnp.where(lane == kk, es[kk] / denom, gates)
    gate_ref[...] = gates

    r_i = lax.broadcasted_iota(jnp.int32, (tm, tm), 0)
    c_i = lax.broadcasted_iota(jnp.int32, (tm, tm), 1)
    tri = jnp.where(c_i < r_i, 1.0, 0.0).astype(BF16)
    before = jnp.dot(tri, mem.astype(BF16), preferred_element_type=F32) + carry_scr[0:1, :]
    ranks = jnp.zeros((tm, LANES), jnp.int32)
    for kk in range(TOPK_EXPERTS):
        rk = jnp.sum(jnp.where(hots[kk], before, 0.0), axis=-1, keepdims=True)
        ranks = jnp.where(lane == kk, rk.astype(jnp.int32), ranks)
    rank_ref[...] = ranks
    carry = carry_scr[...] + jnp.sum(mem, axis=0, keepdims=True)
    carry_scr[...] = carry
    cnt_ref[...] = carry


def _outproj(att, pc, x2, wa, wp, gpost, gtm, gpre, scf, shf, rwh, rwl, rb, seq):
    T, D = x2.shape
    tm = 512
    per_b = seq // tm
    row = lambda w: pl.BlockSpec((tm, w), lambda i: (i, 0))
    const = lambda shape: pl.BlockSpec(shape, lambda i: tuple(0 for _ in shape), pipeline_mode=pl.Buffered(1))
    perb = pl.BlockSpec((1, 1, D), lambda i: (i // per_b, 0, 0))
    return pl.pallas_call(
        _outproj_kernel,
        out_shape=(jax.ShapeDtypeStruct((T, D), F32), jax.ShapeDtypeStruct((T * ROW_SUBLANES, LANES), jnp.uint32),
                   jax.ShapeDtypeStruct((T, LANES), jnp.int32), jax.ShapeDtypeStruct((T, LANES), F32),
                   jax.ShapeDtypeStruct((T, LANES), jnp.int32), jax.ShapeDtypeStruct((8, LANES), F32)),
        grid=(T // tm,),
        in_specs=[row(att.shape[1]), row(pc.shape[1]), row(D), const(wa.shape), const(wp.shape),
                  const((1, D)), perb, const((1, D)), perb, perb,
                  const(rwh.shape), const(rwl.shape), const((1, LANES))],
        out_specs=(row(D), pl.BlockSpec((tm * ROW_SUBLANES, LANES), lambda i: (i, 0)),
                   row(LANES), row(LANES), row(LANES), pl.BlockSpec((8, LANES), lambda i: (0, 0))),
        scratch_shapes=[pltpu.VMEM((8, LANES), F32)],
        compiler_params=_cparams(("arbitrary",)),
        name="out_proj_router",
    )(att, pc, x2, wa, wp, gpost, gtm, gpre, scf, shf, rwh, rwl, rb)


MOE_F_CHUNK = 256
MOE_D_CHUNK = 512
MOE_AHEAD = 2
MOE_SLOTS = MOE_AHEAD + 1


MOE_W_PHASES = 6


def _moe_kernel(te_ref, rt_ref, nu_ref, par_ref, pos_ref, nxt_ref, pn_ref,
                h_hbm, wg_hbm, bg_ref, wu_hbm, bu_ref, wd_hbm, bd_ref, o_ref,
                xbuf, act_scr, hi_scr, wg_bf, wu_bf, wd_bf, stg_a, stg_b, sem, wsem, *, layer):
    i = pl.program_id(0)
    tm = act_scr.shape[0]
    Fe = act_scr.shape[1]
    D = wd_bf.shape[-1]
    half = D // 2
    n_used = nu_ref[0]
    last = n_used - 1

    def piece(ph, expert):
        if ph < 4:
            src_ref, dst = (wg_hbm, stg_a) if ph < 2 else (wu_hbm, stg_a)
            rows = D // 2
        else:
            src_ref, dst, rows = wd_hbm, stg_b, Fe // 2
        r0 = (ph % 2) * rows
        return pltpu.make_async_copy(src_ref.at[layer, expert, pl.ds(r0, rows), :], dst, wsem)

    def round_piece(ph, slot):
        if ph < 4:
            rows = D // 2
            dst = wg_bf if ph < 2 else wu_bf
            dst[slot, (ph % 2) * rows:(ph % 2 + 1) * rows, :] = stg_a[...].astype(BF16)
        else:
            rows = Fe // 2
            wd_bf[slot, (ph % 2) * rows:(ph % 2 + 1) * rows, :] = stg_b[...].astype(BF16)

    def row_copy(tile_id, r, slot):
        tok = rt_ref[tile_id * tm + r]
        src = h_hbm.at[pl.ds(pl.multiple_of(tok * ROW_SUBLANES, ROW_SUBLANES), ROW_SUBLANES)]
        return pltpu.make_async_copy(src, xbuf.at[slot, pl.ds(r * ROW_SUBLANES, ROW_SUBLANES)], sem.at[slot])

    def tile_wait(slot):
        pltpu.make_async_copy(h_hbm.at[pl.ds(0, tm * ROW_SUBLANES)], xbuf.at[slot], sem.at[slot]).wait()

    @pl.when(i == 0)
    def _():
        for t in range(MOE_AHEAD):
            def issue(r, carry, t=t):
                row_copy(jnp.minimum(t, last), r, t).start()
                return carry
            lax.fori_loop(0, tm, issue, 0)

    @pl.when(i < n_used)
    def _():
        slot = i % MOE_SLOTS
        nslot = (i + MOE_AHEAD) % MOE_SLOTS
        ntile = jnp.minimum(i + MOE_AHEAD, last)
        n_f, n_d = Fe // MOE_F_CHUNK, D // MOE_D_CHUNK
        per = tm // (n_f + n_d)

        def issue_part(p):
            for r in range(p * per, (p + 1) * per):
                row_copy(ntile, r, nslot).start()

        expert = te_ref[i]
        wslot = par_ref[i]
        k = pos_ref[i]
        nxt = nxt_ref[i]

        @pl.when(i == 0)
        def _():
            for ph in range(MOE_W_PHASES):
                piece(ph, expert).start()
                piece(ph, expert).wait()
                round_piece(ph, wslot)

        @pl.when((k == 0) & (i > 0))
        def _():
            done = jnp.minimum(pn_ref[i] - 1, MOE_W_PHASES)
            for ph in range(MOE_W_PHASES):
                @pl.when(ph >= done)
                def _(ph=ph):
                    @pl.when(ph > done)
                    def _():
                        piece(ph, expert).start()
                    piece(ph, expert).wait()
                    round_piece(ph, wslot)

        for ph in range(MOE_W_PHASES):
            @pl.when((k == ph + 1) & (nxt >= 0))
            def _(ph=ph):
                piece(ph, nxt).wait()
                round_piece(ph, 1 - wslot)

        for ph in range(MOE_W_PHASES):
            @pl.when((k == ph) & (nxt >= 0))
            def _(ph=ph):
                piece(ph, nxt).start()

        tile_wait(slot)
        words = [xbuf[slot, pl.ds(s, tm, stride=ROW_SUBLANES), :] for s in range(ROW_SUBLANES)]
        x = jnp.concatenate([_unpack_hi(w).astype(BF16) for w in words]
                            + [_unpack_lo(w).astype(BF16) for w in words], axis=1)
        for c in range(n_f):
            issue_part(c)
            cs = slice(c * MOE_F_CHUNK, (c + 1) * MOE_F_CHUNK)
            g = jnp.dot(x, wg_bf[wslot, :, cs], preferred_element_type=F32) + bg_ref[0, 0, :, cs]
            u = jnp.dot(x, wu_bf[wslot, :, cs], preferred_element_type=F32) + bu_ref[0, 0, :, cs]
            g = jnp.minimum(g, SWIGLU_LIMIT)
            u = jnp.clip(u, -SWIGLU_LIMIT, SWIGLU_LIMIT)
            act_scr[:, cs] = (g * jax.nn.sigmoid(SWIGLU_ALPHA * g) * (u + 1.0)).astype(BF16)
        act = act_scr[...]
        for c in range(n_d):
            issue_part(n_f + c)
            c0 = c * MOE_D_CHUNK
            y = (jnp.dot(act, wd_bf[wslot, :, c0:c0 + MOE_D_CHUNK], preferred_element_type=F32)
                 + bd_ref[0, 0, :, c0:c0 + MOE_D_CHUNK])
            for j in range(MOE_D_CHUNK // LANES):
                col = c0 + j * LANES
                yj = y[:, j * LANES:(j + 1) * LANES]
                if col < half:
                    hi_scr[:, col:col + LANES] = _bf16_bits(yj)
                else:
                    s = (col - half) // LANES
                    o_ref[pl.ds(s, tm, stride=ROW_SUBLANES), :] = hi_scr[:, col - half:col - half + LANES] | (_bf16_bits(yj) >> 16)

        @pl.when(i == last)
        def _():
            for t in range(1, MOE_AHEAD + 1):
                tile_wait((i + t) % MOE_SLOTS)

    @pl.when(i >= n_used)
    def _():
        o_ref[...] = jnp.zeros_like(o_ref)


def _group_tables(tile_expert, n_used):
    n = tile_expert.shape[0]
    idx = jnp.arange(n, dtype=jnp.int32)
    first = jnp.concatenate([jnp.ones((1,), bool), tile_expert[1:] != tile_expert[:-1]])
    par = (jnp.cumsum(first.astype(jnp.int32)) - 1) % 2
    pos = idx - lax.cummax(jnp.where(first, idx, 0))
    later = (idx[None, :] > idx[:, None]) & (idx[None, :] < n_used[0]) & (tile_expert[None, :] != tile_expert[:, None])
    nxt_idx = jnp.min(jnp.where(later, idx[None, :], n), axis=1)
    nxt = jnp.where(nxt_idx < n, tile_expert[jnp.minimum(nxt_idx, n - 1)], -1)
    pn = jnp.concatenate([jnp.zeros((1,), jnp.int32), pos[:-1] + 1])
    return par.astype(jnp.int32), pos.astype(jnp.int32), nxt.astype(jnp.int32), pn.astype(jnp.int32)


def _moe(layer, tile_expert, row_token, n_used, h2p, wg, bg, wu, bu, wd, bd, tm):
    L, E, D, Fe = wg.shape
    n_tiles = tile_expert.shape[0]
    assert tm % (Fe // MOE_F_CHUNK + D // MOE_D_CHUNK) == 0 and (D // 2) % MOE_D_CHUNK == 0
    par, pos, nxt, pn = _group_tables(tile_expert, n_used)
    bspec = lambda w: pl.BlockSpec((1, 1, 1, w), lambda i, te, *_: (layer, te[i], 0, 0))
    hbm = pl.BlockSpec(memory_space=pl.ANY)
    return pl.pallas_call(
        functools.partial(_moe_kernel, layer=layer),
        out_shape=jax.ShapeDtypeStruct((n_tiles * tm * ROW_SUBLANES, LANES), jnp.uint32),
        grid_spec=pltpu.PrefetchScalarGridSpec(
            num_scalar_prefetch=7,
            grid=(n_tiles,),
            in_specs=[hbm, hbm, bspec(Fe), hbm, bspec(Fe), hbm, bspec(D)],
            out_specs=pl.BlockSpec((tm * ROW_SUBLANES, LANES), lambda i, *_: (i, 0)),
            scratch_shapes=[pltpu.VMEM((MOE_SLOTS, tm * ROW_SUBLANES, LANES), jnp.uint32),
                            pltpu.VMEM((tm, Fe), BF16), pltpu.VMEM((tm, D // 2), jnp.uint32),
                            pltpu.VMEM((2, D, Fe), BF16), pltpu.VMEM((2, D, Fe), BF16),
                            pltpu.VMEM((2, Fe, D), BF16),
                            pltpu.VMEM((D // 2, Fe), F32), pltpu.VMEM((Fe // 2, D), F32),
                            pltpu.SemaphoreType.DMA((MOE_SLOTS,)), pltpu.SemaphoreType.DMA],
        ),
        compiler_params=_cparams(("arbitrary",)),
        name="moe_experts",
    )(tile_expert, row_token, n_used, par, pos, nxt, pn, h2p, wg, bg.reshape(L, E, 1, Fe),
      wu, bu.reshape(L, E, 1, Fe), wd, bd.reshape(L, E, 1, D))


def _combine_kernel(dest_ref, yp_hbm, gate_ref, x_ref, gpost_ref, gtf_ref, o_ref, buf, sem):
    i = pl.program_id(0)
    n = pl.num_programs(0)
    tc, D = x_ref.shape
    half = D // 2
    K = TOPK_EXPERTS

    def issue_tile(tile_id, slot):
        base = tile_id * tc * K

        def issue(r, carry):
            for kk in range(K):
                d = dest_ref[base + r * K + kk]
                src = yp_hbm.at[pl.ds(pl.multiple_of(d * ROW_SUBLANES, ROW_SUBLANES), ROW_SUBLANES)]
                dst = buf.at[slot, kk, pl.ds(r * ROW_SUBLANES, ROW_SUBLANES)]
                pltpu.make_async_copy(src, dst, sem.at[slot]).start(priority=kk % 2)
            return carry

        lax.fori_loop(0, tc, issue, 0)

    def tile_wait(slot):
        for kk in range(K):
            pltpu.make_async_copy(yp_hbm.at[pl.ds(0, tc * ROW_SUBLANES)], buf.at[slot, kk], sem.at[slot]).wait()

    @pl.when(i == 0)
    def _():
        issue_tile(0, 0)

    slot = i % 2

    @pl.when(i + 1 < n)
    def _():
        issue_tile(i + 1, 1 - slot)

    tile_wait(slot)
    gates = gate_ref[...]
    gk = [jnp.broadcast_to(gates[:, kk:kk + 1], (tc, LANES)) for kk in range(K)]
    his, los = [], []
    ssq = jnp.zeros((tc, 1), F32)
    for s in range(ROW_SUBLANES):
        hi = lo = None
        for kk in range(K):
            w = buf[slot, kk, pl.ds(s, tc, stride=ROW_SUBLANES), :]
            h, l = gk[kk] * _unpack_hi(w), gk[kk] * _unpack_lo(w)
            hi, lo = (h, l) if kk == 0 else (hi + h, lo + l)
        his.append(hi)
        los.append(lo)
        ssq = ssq + jnp.sum(hi * hi, axis=-1, keepdims=True) + jnp.sum(lo * lo, axis=-1, keepdims=True)
    inv = lax.rsqrt(ssq * (1.0 / D) + RMS_EPS)
    for s in range(ROW_SUBLANES):
        for col, y in ((s * LANES, his[s]), (half + s * LANES, los[s])):
            cs = slice(col, col + LANES)
            o_ref[:, cs] = x_ref[:, cs] + gtf_ref[0, :, cs] * ((y * inv) * gpost_ref[:, cs])


def _combine(dest, yp, gates, x1, gpost, gtf, seq):
    T, D = x1.shape
    tc = 256
    per_b = seq // tc
    return pl.pallas_call(
        _combine_kernel,
        out_shape=jax.ShapeDtypeStruct((T, D), F32),
        grid_spec=pltpu.PrefetchScalarGridSpec(
            num_scalar_prefetch=1,
            grid=(T // tc,),
            in_specs=[pl.BlockSpec(memory_space=pl.ANY),
                      pl.BlockSpec((tc, LANES), lambda i, d: (i, 0)),
                      pl.BlockSpec((tc, D), lambda i, d: (i, 0)),
                      pl.BlockSpec((1, D), lambda i, d: (0, 0)),
                      pl.BlockSpec((1, 1, D), lambda i, d: (i // per_b, 0, 0))],
            out_specs=pl.BlockSpec((tc, D), lambda i, d: (i, 0)),
            scratch_shapes=[pltpu.VMEM((2, TOPK_EXPERTS, tc * ROW_SUBLANES, LANES), jnp.uint32),
                            pltpu.SemaphoreType.DMA((2,))],
        ),
        compiler_params=_cparams(("arbitrary",)),
        name="moe_combine",
    )(dest, yp, gates, x1, gpost, gtf)


def _dest_kernel(idx_ref, rank_ref, cnt_ref, dest_ref, *, tm_rows):
    tm = idx_ref.shape[0]
    tiles = jnp.floor((cnt_ref[0:1, :] + float(tm_rows - 1)) * (1.0 / tm_rows))
    r_i = lax.broadcasted_iota(jnp.int32, (LANES, LANES), 0)
    c_i = lax.broadcasted_iota(jnp.int32, (LANES, LANES), 1)
    before = jnp.where(r_i < c_i, 1.0, 0.0).astype(BF16)
    start = jnp.dot(jnp.broadcast_to(tiles, (8, LANES)).astype(BF16), before,
                    preferred_element_type=F32)[0:1] * float(tm_rows)
    lane = lax.broadcasted_iota(jnp.int32, (tm, LANES), 1)
    idx = idx_ref[...]
    rank = rank_ref[...]
    out = jnp.zeros((tm, LANES), jnp.int32)
    for kk in range(TOPK_EXPERTS):
        base = jnp.sum(jnp.where(lane == idx[:, kk:kk + 1], start, 0.0), axis=-1, keepdims=True)
        out = jnp.where(lane == kk, base.astype(jnp.int32) + rank[:, kk:kk + 1], out)
    dest_ref[...] = out


def _dest(idx, rank, cnt, tm_rows):
    T = idx.shape[0]
    tm = 2048
    assert tm_rows & (tm_rows - 1) == 0
    row = pl.BlockSpec((tm, LANES), lambda i: (i, 0))
    return pl.pallas_call(
        functools.partial(_dest_kernel, tm_rows=tm_rows),
        out_shape=jax.ShapeDtypeStruct((T, LANES), jnp.int32),
        grid=(T // tm,),
        in_specs=[row, row, pl.BlockSpec((8, LANES), lambda i: (0, 0))],
        out_specs=row,
        compiler_params=_cparams(("parallel",)),
        name="route_dest",
    )(idx, rank, cnt)


INVERT_UNROLL = 8


def _invert_kernel(dest_ref, lo_ref, hi_ref, out_ref):
    def fill(j, carry):
        out_ref[j] = 0
        return carry

    for e in range(lo_ref.shape[0]):
        lax.fori_loop(lo_ref[e], hi_ref[e], fill, 0)

    per = INVERT_UNROLL // TOPK_EXPERTS

    def body(b, carry):
        for u in range(INVERT_UNROLL):
            out_ref[dest_ref[b * INVERT_UNROLL + u]] = b * per + u // TOPK_EXPERTS
        return carry

    lax.fori_loop(0, dest_ref.shape[0] // INVERT_UNROLL, body, 0)


def _invert(dest, lo, hi, n_rows):
    smem = pl.BlockSpec(memory_space=pltpu.SMEM)
    return pl.pallas_call(
        _invert_kernel,
        out_shape=jax.ShapeDtypeStruct((n_rows,), jnp.int32),
        in_specs=[smem, smem, smem],
        out_specs=smem,
        name="route_invert",
    )(dest, lo, hi)


def _routing_tables(idx, rank, cnt, n_experts, tm):
    T = idx.shape[0]
    n_tiles = (T * TOPK_EXPERTS) // tm + n_experts
    counts = cnt[0, :n_experts].astype(jnp.int32)
    padded = ((counts + tm - 1) // tm) * tm
    pad_end = jnp.cumsum(padded)
    tile_start = jnp.arange(n_tiles, dtype=jnp.int32) * tm
    tile_expert = jnp.minimum(jnp.sum((pad_end[None, :] <= tile_start[:, None]).astype(jnp.int32), axis=1),
                              n_experts - 1)
    n_used = pad_end[-1:] // tm
    total = jnp.full((1,), n_tiles * tm, jnp.int32)
    lo = jnp.concatenate([pad_end - padded + counts, pad_end[-1:]])
    hi = jnp.concatenate([pad_end, total])
    dest = _dest(idx, rank, cnt, tm)[:, :TOPK_EXPERTS].reshape(-1)
    row_token = _invert(dest, lo, hi, n_tiles * tm)
    return dest, row_token, tile_expert, n_used


def kernel(x, c, w_ada, b_ada, g_pre_mix, g_post_mix, g_pre_ffn, g_post_ffn, w_in, w_out, pool_w, pool_scale,
           conv_w, rel_bias, router_w, router_b, w_gate, b_gate, w_up, b_up, w_down, b_down):
    B, S, D = x.shape
    L = w_ada.shape[0]
    T = B * S
    E = router_w.shape[-1]
    attn_w = D // 2
    heads = attn_w // HEAD_DIM
    moe_tm = 256
    assert D == 2 * ROW_SUBLANES * LANES

    mod = _ada(c, w_ada, b_ada)
    x2 = x.reshape(T, D)
    for l in range(L):
        sh_m, sc_m, gt_m, sh_f, sc_f, gt_f = [mod[l, :, i * D:(i + 1) * D].reshape(B, 1, D) for i in range(N_MOD)]
        row = lambda v: v.reshape(1, D)

        qkv, rest = _inproj(x2, row(g_pre_mix[l]), sc_m, sh_m, w_in[l].astype(BF16), S, attn_w)
        att = _attention(qkv, rel_bias, B, S, heads)
        pc = _poolconv(rest, pool_w[l], pool_scale[l], conv_w[l], B, S)

        wo = w_out[l].astype(BF16)
        rw = jnp.zeros((D, LANES), F32).at[:, :E].set(router_w[l])
        rwh = rw.astype(BF16)
        rwl = (rw - rwh.astype(F32)).astype(BF16)
        rb = jnp.full((1, LANES), NEG_INF, F32).at[0, :E].set(router_b[l])
        x1, h2, idx, gates, rank, cnt = _outproj(
            att, pc, x2, wo[:attn_w], wo[attn_w:], row(g_post_mix[l]), gt_m, row(g_pre_ffn[l]), sc_f, sh_f,
            rwh, rwl, rb, S)

        dest, row_token, tile_expert, n_used = _routing_tables(idx, rank, cnt, E, moe_tm)
        yp = _moe(l, tile_expert, row_token, n_used, h2, w_gate, b_gate, w_up, b_up, w_down, b_down, moe_tm)
        x2 = _combine(dest, yp, gates, x1, row(g_post_ffn[l]), gt_f, S)
    return x2.reshape(B, S, D)
```

```python
import functools
import math

import numpy as np
import jax
import jax.numpy as jnp
from jax import lax
from jax.experimental import pallas as pl
from jax.experimental.pallas import tpu as pltpu

F32 = jnp.float32
BF16 = jnp.bfloat16

LANES = 128
HEAD_DIM = 128
MOBA_BLOCK = 256
MOBA_TOPK = 3
REL_BUCKETS = 32
REL_MAX_DISTANCE = 128
POOL_WINDOWS = (2, 4, 8, 16)
GROUP_DIM = 128
CONV_K = 3
TOPK_EXPERTS = 4
SWIGLU_LIMIT = 7.0
SWIGLU_ALPHA = 1.702
N_MOD = 6
RMS_EPS = 1e-6
NEG_INF = -1e30
ATTN_SCALE = HEAD_DIM ** -0.5
LOG2E = math.log2(math.e)

VMEM_LIMIT = 56 * 1024 * 1024


def _cparams(sem):
    return pltpu.CompilerParams(dimension_semantics=sem, vmem_limit_bytes=VMEM_LIMIT)


def _rms(x):
    return x * lax.rsqrt(jnp.mean(x * x, axis=-1, keepdims=True) + RMS_EPS)


ROW_SUBLANES = 8


def _bf16_bits(v):
    return lax.bitcast_convert_type(v.astype(BF16).astype(F32), jnp.uint32)


def _pack_words(hi, lo):
    return _bf16_bits(hi) | (_bf16_bits(lo) >> 16)


def _unpack_hi(w):
    return lax.bitcast_convert_type(w & jnp.uint32(0xFFFF0000), F32)


def _unpack_lo(w):
    return lax.bitcast_convert_type(w << 16, F32)


def _ada_kernel(c_ref, w_ref, b_ref, o_ref):
    c = c_ref[...]
    ca = (c * jax.nn.sigmoid(c)).astype(BF16)
    o_ref[0] = jnp.dot(ca, w_ref[0].astype(BF16), preferred_element_type=F32) + b_ref[0]


def _ada(c, w_ada, b_ada):
    L, D, N = w_ada.shape
    B = c.shape[0]
    tn = 1024
    return pl.pallas_call(
        _ada_kernel,
        out_shape=jax.ShapeDtypeStruct((L, B, N), F32),
        grid=(L, N // tn),
        in_specs=[
            pl.BlockSpec((B, D), lambda l, j: (0, 0)),
            pl.BlockSpec((1, D, tn), lambda l, j: (l, 0, j)),
            pl.BlockSpec((1, 1, tn), lambda l, j: (l, 0, j)),
        ],
        out_specs=pl.BlockSpec((1, B, tn), lambda l, j: (l, 0, j)),
        compiler_params=_cparams(("parallel", "parallel")),
        name="ada_mod",
    )(c, w_ada, b_ada.reshape(L, 1, N))


def _inproj_kernel(x_ref, g_ref, sc_ref, sh_ref, w_ref, qkv_ref, rest_ref, h_scr, *, n_qkv):
    j = pl.program_id(1)

    @pl.when(j == 0)
    def _():
        h = _rms(x_ref[...]) * g_ref[...]
        h = h * (1.0 + sc_ref[0]) + sh_ref[0]
        h_scr[...] = h.astype(BF16)

    acc = jnp.dot(h_scr[...], w_ref[...], preferred_element_type=F32)

    @pl.when(j < n_qkv)
    def _():
        qkv_ref[...] = acc.astype(BF16)

    @pl.when(j >= n_qkv)
    def _():
        rest_ref[...] = acc


def _inproj(x2, g, sc, sh, w_bf, seq, attn_w):
    T, D = x2.shape
    N = w_bf.shape[1]
    tm, tn = 1024, 512
    n_qkv = 3 * attn_w // tn
    n_rest = (N - 3 * attn_w) // tn
    per_b = seq // tm
    return pl.pallas_call(
        functools.partial(_inproj_kernel, n_qkv=n_qkv),
        out_shape=(jax.ShapeDtypeStruct((T, 3 * attn_w), BF16),
                   jax.ShapeDtypeStruct((T, N - 3 * attn_w), F32)),
        grid=(T // tm, n_qkv + n_rest),
        in_specs=[
            pl.BlockSpec((tm, D), lambda i, j: (i, 0)),
            pl.BlockSpec((1, D), lambda i, j: (0, 0)),
            pl.BlockSpec((1, 1, D), lambda i, j: (i // per_b, 0, 0)),
            pl.BlockSpec((1, 1, D), lambda i, j: (i // per_b, 0, 0)),
            pl.BlockSpec((D, tn), lambda i, j: (0, j)),
        ],
        out_specs=(
            pl.BlockSpec((tm, tn), lambda i, j: (i, jnp.minimum(j, n_qkv - 1))),
            pl.BlockSpec((tm, tn), lambda i, j: (i, jnp.maximum(j - n_qkv, 0))),
        ),
        scratch_shapes=[pltpu.VMEM((tm, D), BF16)],
        compiler_params=_cparams(("parallel", "arbitrary")),
        name="mixer_in_proj",
    )(x2, g, sc, sh, w_bf)


def _rel_bucket_np(n):
    n = np.maximum(n, 0)
    max_exact = REL_BUCKETS // 2
    nf = np.maximum(n, max_exact).astype(np.float32)
    large = max_exact + (np.log(nf / np.float32(max_exact)) / np.float32(math.log(REL_MAX_DISTANCE / max_exact))
                         * np.float32(REL_BUCKETS - max_exact)).astype(np.int32)
    large = np.minimum(large, REL_BUCKETS - 1)
    return np.where(n < max_exact, n, large).astype(np.int32)


def _bucket_tables(seq):
    qi = np.arange(MOBA_BLOCK)[:, None]
    ki = np.arange(MOBA_BLOCK)[None, :]
    own = np.where(ki <= qi, _rel_bucket_np(qi - ki), -1)
    prev = _rel_bucket_np(qi - ki + MOBA_BLOCK)
    far = _rel_bucket_np(np.arange(MOBA_BLOCK + 1, max(seq, MOBA_BLOCK + 2)))
    assert np.all(far == far[0])
    return np.stack([own, prev]).astype(np.int32), int(far[0])


def _attn_kernel(tab_ref, q_ref, k_ref, v_ref, bkt_ref, o_ref, bias_scr, *, nblk, far_bucket):
    h = pl.program_id(0)
    b = pl.program_id(1)
    L = MOBA_BLOCK

    @pl.when(b == 0)
    def _():
        for m in range(2):
            bk = bkt_ref[m]
            acc = jnp.full((L, L), NEG_INF, F32)
            for r in range(REL_BUCKETS):
                acc = jnp.where(bk == r, tab_ref[h, r] * LOG2E, acc)
            bias_scr[m] = acc

    far_bias = tab_ref[h, far_bucket] * LOG2E
    q = q_ref[...]
    k = k_ref[...]
    v = v_ref[...]
    dn = (((1,), (1,)), ((), ()))

    kmean = jnp.mean(k.astype(F32).reshape(nblk, L, HEAD_DIM), axis=1)
    kmean = jnp.concatenate([kmean, jnp.zeros((LANES - nblk, HEAD_DIM), F32)], axis=0).astype(BF16)
    gate = lax.dot_general(q, kmean, dn, preferred_element_type=F32)
    lane = lax.broadcasted_iota(jnp.int32, (L, LANES), 1)

    for a in range(nblk):
        qa = q[a * L:(a + 1) * L]
        n = (a + 1) * L
        s = lax.dot_general(qa, k[:n], dn, preferred_element_type=F32) * (ATTN_SCALE * LOG2E)
        hide = None
        if a > MOBA_TOPK:
            ga = jnp.where(lane < a, gate[a * L:(a + 1) * L], NEG_INF)
            cnt = jnp.zeros((L, LANES), F32)
            for i in range(a):
                gi = ga[:, i:i + 1]
                ge = jnp.where(gi >= ga, 1.0, 0.0)
                gt = jnp.where(gi > ga, 1.0, 0.0)
                cnt = cnt + jnp.where(lane > i, ge, gt)
            hide = jnp.where((cnt < float(MOBA_TOPK)) & (lane < a), 0.0, NEG_INF)
            hide_far = hide + far_bias
        pieces = []
        for j in range(a + 1):
            sj = s[:, j * L:(j + 1) * L]
            if j == a:
                sj = sj + bias_scr[0]
            elif j == a - 1:
                sj = sj + bias_scr[1]
                if hide is not None:
                    sj = sj + hide[:, j:j + 1]
            else:
                sj = sj + (far_bias if hide is None else hide_far[:, j:j + 1])
            pieces.append(sj)
        s = pieces[0] if a == 0 else jnp.concatenate(pieces, axis=1)
        m = jnp.max(s, axis=-1, keepdims=True)
        p = jnp.exp2(s - m)
        l = jnp.sum(p, axis=-1, keepdims=True)
        o = jnp.dot(p.astype(BF16), v[:n], preferred_element_type=F32) / l
        o_ref[a * L:(a + 1) * L, :] = o.astype(o_ref.dtype)


def _attention(qkv, rel_bias, batch, seq, heads):
    T = qkv.shape[0]
    nblk = seq // MOBA_BLOCK
    bkt, far_bucket = _bucket_tables(seq)
    table = rel_bias.T.astype(F32)
    return pl.pallas_call(
        functools.partial(_attn_kernel, nblk=nblk, far_bucket=far_bucket),
        out_shape=jax.ShapeDtypeStruct((T, heads * HEAD_DIM), BF16),
        grid=(heads, batch),
        in_specs=[
            pl.BlockSpec(memory_space=pltpu.SMEM),
            pl.BlockSpec((seq, HEAD_DIM), lambda h, b: (b, h)),
            pl.BlockSpec((seq, HEAD_DIM), lambda h, b: (b, heads + h)),
            pl.BlockSpec((seq, HEAD_DIM), lambda h, b: (b, 2 * heads + h)),
            pl.BlockSpec((2, MOBA_BLOCK, MOBA_BLOCK), lambda h, b: (0, 0, 0)),
        ],
        out_specs=pl.BlockSpec((seq, HEAD_DIM), lambda h, b: (b, h)),
        scratch_shapes=[pltpu.VMEM((2, MOBA_BLOCK, MOBA_BLOCK), F32)],
        compiler_params=_cparams(("arbitrary", "arbitrary")),
        name="moba_attention",
    )(table, qkv, qkv, qkv, jnp.asarray(bkt))


def _poolconv_kernel(u_ref, gb_ref, gc_ref, hc_ref, pw_ref, ps_ref, cw_ref, o_ref):
    S = u_ref.shape[0]
    G = len(POOL_WINDOWS)
    row = lax.broadcasted_iota(jnp.int32, (S, GROUP_DIM), 0)

    def shift(x, k):
        return jnp.where(row >= k, pltpu.roll(x, k, 0), 0.0)

    for gi, w in enumerate(POOL_WINDOWS):
        cs = slice(gi * GROUP_DIM, (gi + 1) * GROUP_DIM)
        ug = u_ref[:, cs]
        win = ug
        k = 1
        while k < w:
            win = win + shift(win, k)
            k *= 2
        count = jnp.minimum(row + 1, w).astype(F32)
        d = win / count - ug
        y = jnp.dot(d.astype(BF16), pw_ref[gi].astype(BF16), preferred_element_type=F32)
        o_ref[:, cs] = (y * ps_ref[:, cs]).astype(o_ref.dtype)

        uu = gc_ref[:, cs] * hc_ref[:, cs]
        conv = (cw_ref[0:1, cs] * shift(uu, 2) + cw_ref[1:2, cs] * shift(uu, 1)
                + cw_ref[2:3, cs] * uu)
        oc = slice((G + gi) * GROUP_DIM, (G + gi + 1) * GROUP_DIM)
        o_ref[:, oc] = (gb_ref[:, cs] * conv).astype(o_ref.dtype)


def _poolconv(rest, pool_w, pool_scale, conv_w, batch, seq):
    T = rest.shape[0]
    W = len(POOL_WINDOWS) * GROUP_DIM
    spec = lambda c: pl.BlockSpec((seq, W), lambda b: (b, c))
    return pl.pallas_call(
        _poolconv_kernel,
        out_shape=jax.ShapeDtypeStruct((T, 2 * W), BF16),
        grid=(batch,),
        in_specs=[
            spec(0), spec(1), spec(2), spec(3),
            pl.BlockSpec(pool_w.shape, lambda b: (0, 0, 0)),
            pl.BlockSpec((1, W), lambda b: (0, 0)),
            pl.BlockSpec((CONV_K, W), lambda b: (0, 0)),
        ],
        out_specs=pl.BlockSpec((seq, 2 * W), lambda b: (b, 0)),
        compiler_params=_cparams(("parallel",)),
        name="pool_conv_mixers",
    )(rest, rest, rest, rest, pool_w, pool_scale.reshape(1, W), conv_w)


def _outproj_kernel(att_ref, pc_ref, x_ref, wa_ref, wp_ref, gpost_ref, gtm_ref, gpre_ref, scf_ref, shf_ref,
                    rwh_ref, rwl_ref, rb_ref,
                    x1_ref, h2_ref, idx_ref, gate_ref, rank_ref, cnt_ref, carry_scr):
    i = pl.program_id(0)
    tm = x_ref.shape[0]

    @pl.when(i == 0)
    def _():
        carry_scr[...] = jnp.zeros_like(carry_scr)

    mix = (jnp.dot(att_ref[...], wa_ref[...], preferred_element_type=F32)
           + jnp.dot(pc_ref[...], wp_ref[...], preferred_element_type=F32))
    x1 = x_ref[...] + gtm_ref[0] * (_rms(mix) * gpost_ref[...])
    x1_ref[...] = x1
    h2 = (_rms(x1) * gpre_ref[...]) * (1.0 + scf_ref[0]) + shf_ref[0]
    half = h2.shape[1] // 2
    for s in range(ROW_SUBLANES):
        h2_ref[pl.ds(s, tm, stride=ROW_SUBLANES), :] = _pack_words(h2[:, s * LANES:(s + 1) * LANES],
                                      h2[:, half + s * LANES:half + (s + 1) * LANES])

    hi = h2.astype(BF16)
    lo = (h2 - hi.astype(F32)).astype(BF16)
    r2 = jnp.dot(hi, jnp.concatenate([rwh_ref[...], rwl_ref[...]], axis=1), preferred_element_type=F32)
    logits = (r2[:, :LANES] + r2[:, LANES:]
              + jnp.dot(lo, rwh_ref[...], preferred_element_type=F32)) + rb_ref[...]

    lane = lax.broadcasted_iota(jnp.int32, (tm, LANES), 1)
    work = logits
    mem = jnp.zeros((tm, LANES), F32)
    vals, hots = [], []
    idx_out = jnp.zeros((tm, LANES), jnp.int32)
    for kk in range(TOPK_EXPERTS):
        m = jnp.max(work, axis=-1, keepdims=True)
        ik = jnp.min(jnp.where(work == m, lane, LANES), axis=-1, keepdims=True)
        hot = lane == ik
        vals.append(m)
        hots.append(hot)
        idx_out = jnp.where(lane == kk, ik, idx_out)
        mem = jnp.where(hot, 1.0, mem)
        work = jnp.where(hot, -jnp.inf, work)
    idx_ref[...] = idx_out

    es = [jnp.exp(vk - vals[0]) for vk in vals]
    denom = es[0]
    for e in es[1:]:
        denom = denom + e
    gates = jnp.zeros((tm, LANES), F32)
    for kk in range(TOPK_EXPERTS):
        gates = jnp.where(lane == kk, es[kk] / denom, gates)
    gate_ref[...] = gates

    r_i = lax.broadcasted_iota(jnp.int32, (tm, tm), 0)
    c_i = lax.broadcasted_iota(jnp.int32, (tm, tm), 1)
    tri = jnp.where(c_i < r_i, 1.0, 0.0).astype(BF16)
    before = jnp.dot(tri, mem.astype(BF16), preferred_element_type=F32) + carry_scr[0:1, :]
    ranks = jnp.zeros((tm, LANES), jnp.int32)
    for kk in range(TOPK_EXPERTS):
        rk = jnp.sum(jnp.where(hots[kk], before, 0.0), axis=-1, keepdims=True)
        ranks = jnp.where(lane == kk, rk.astype(jnp.int32), ranks)
    rank_ref[...] = ranks
    carry = carry_scr[...] + jnp.sum(mem, axis=0, keepdims=True)
    carry_scr[...] = carry
    cnt_ref[...] = carry


def _outproj(att, pc, x2, wa, wp, gpost, gtm, gpre, scf, shf, rwh, rwl, rb, seq):
    T, D = x2.shape
    tm = 512
    per_b = seq // tm
    row = lambda w: pl.BlockSpec((tm, w), lambda i: (i, 0))
    const = lambda shape: pl.BlockSpec(shape, lambda i: tuple(0 for _ in shape), pipeline_mode=pl.Buffered(1))
    perb = pl.BlockSpec((1, 1, D), lambda i: (i // per_b, 0, 0))
    return pl.pallas_call(
        _outproj_kernel,
        out_shape=(jax.ShapeDtypeStruct((T, D), F32), jax.ShapeDtypeStruct((T * ROW_SUBLANES, LANES), jnp.uint32),
                   jax.ShapeDtypeStruct((T, LANES), jnp.int32), jax.ShapeDtypeStruct((T, LANES), F32),
                   jax.ShapeDtypeStruct((T, LANES), jnp.int32), jax.ShapeDtypeStruct((8, LANES), F32)),
        grid=(T // tm,),
        in_specs=[row(att.shape[1]), row(pc.shape[1]), row(D), const(wa.shape), const(wp.shape),
                  const((1, D)), perb, const((1, D)), perb, perb,
                  const(rwh.shape), const(rwl.shape), const((1, LANES))],
        out_specs=(row(D), pl.BlockSpec((tm * ROW_SUBLANES, LANES), lambda i: (i, 0)),
                   row(LANES), row(LANES), row(LANES), pl.BlockSpec((8, LANES), lambda i: (0, 0))),
        scratch_shapes=[pltpu.VMEM((8, LANES), F32)],
        compiler_params=_cparams(("arbitrary",)),
        name="out_proj_router",
    )(att, pc, x2, wa, wp, gpost, gtm, gpre, scf, shf, rwh, rwl, rb)


MOE_F_CHUNK = 256
MOE_D_CHUNK = 512
MOE_AHEAD = 2
MOE_SLOTS = MOE_AHEAD + 1


MOE_W_PHASES = 6


def _moe_kernel(te_ref, rt_ref, nu_ref, par_ref, pos_ref, nxt_ref, pn_ref,
                h_hbm, wg_hbm, bg_ref, wu_hbm, bu_ref, wd_hbm, bd_ref, o_ref,
                xbuf, act_scr, hi_scr, wg_bf, wu_bf, wd_bf, stg_a, stg_b, sem, wsem, *, layer):
    i = pl.program_id(0)
    tm = act_scr.shape[0]
    Fe = act_scr.shape[1]
    D = wd_bf.shape[-1]
    half = D // 2
    n_used = nu_ref[0]
    last = n_used - 1

    def piece(ph, expert):
        if ph < 4:
            src_ref, dst = (wg_hbm, stg_a) if ph < 2 else (wu_hbm, stg_a)
            rows = D // 2
        else:
            src_ref, dst, rows = wd_hbm, stg_b, Fe // 2
        r0 = (ph % 2) * rows
        return pltpu.make_async_copy(src_ref.at[layer, expert, pl.ds(r0, rows), :], dst, wsem)

    def round_piece(ph, slot):
        if ph < 4:
            rows = D // 2
            dst = wg_bf if ph < 2 else wu_bf
            dst[slot, (ph % 2) * rows:(ph % 2 + 1) * rows, :] = stg_a[...].astype(BF16)
        else:
            rows = Fe // 2
            wd_bf[slot, (ph % 2) * rows:(ph % 2 + 1) * rows, :] = stg_b[...].astype(BF16)

    def row_copy(tile_id, r, slot):
        tok = rt_ref[tile_id * tm + r]
        src = h_hbm.at[pl.ds(pl.multiple_of(tok * ROW_SUBLANES, ROW_SUBLANES), ROW_SUBLANES)]
        return pltpu.make_async_copy(src, xbuf.at[slot, pl.ds(r * ROW_SUBLANES, ROW_SUBLANES)], sem.at[slot])

    def tile_wait(slot):
        pltpu.make_async_copy(h_hbm.at[pl.ds(0, tm * ROW_SUBLANES)], xbuf.at[slot], sem.at[slot]).wait()

    @pl.when(i == 0)
    def _():
        for t in range(MOE_AHEAD):
            def issue(r, carry, t=t):
                row_copy(jnp.minimum(t, last), r, t).start()
                return carry
            lax.fori_loop(0, tm, issue, 0)

    @pl.when(i < n_used)
    def _():
        slot = i % MOE_SLOTS
        nslot = (i + MOE_AHEAD) % MOE_SLOTS
        ntile = jnp.minimum(i + MOE_AHEAD, last)
        n_f, n_d = Fe // MOE_F_CHUNK, D // MOE_D_CHUNK
        per = tm // (n_f + n_d)

        def issue_part(p):
            for r in range(p * per, (p + 1) * per):
                row_copy(ntile, r, nslot).start()

        expert = te_ref[i]
        wslot = par_ref[i]
        k = pos_ref[i]
        nxt = nxt_ref[i]

        @pl.when(i == 0)
        def _():
            for ph in range(MOE_W_PHASES):
                piece(ph, expert).start()
                piece(ph, expert).wait()
                round_piece(ph, wslot)

        @pl.when((k == 0) & (i > 0))
        def _():
            done = jnp.minimum(pn_ref[i] - 1, MOE_W_PHASES)
            for ph in range(MOE_W_PHASES):
                @pl.when(ph >= done)
                def _(ph=ph):
                    @pl.when(ph > done)
                    def _():
                        piece(ph, expert).start()
                    piece(ph, expert).wait()
                    round_piece(ph, wslot)

        for ph in range(MOE_W_PHASES):
            @pl.when((k == ph + 1) & (nxt >= 0))
            def _(ph=ph):
                piece(ph, nxt).wait()
                round_piece(ph, 1 - wslot)

        for ph in range(MOE_W_PHASES):
            @pl.when((k == ph) & (nxt >= 0))
            def _(ph=ph):
                piece(ph, nxt).start()

        tile_wait(slot)
        words = [xbuf[slot, pl.ds(s, tm, stride=ROW_SUBLANES), :] for s in range(ROW_SUBLANES)]
        x = jnp.concatenate([_unpack_hi(w).astype(BF16) for w in words]
                            + [_unpack_lo(w).astype(BF16) for w in words], axis=1)
        for c in range(n_f):
            issue_part(c)
            cs = slice(c * MOE_F_CHUNK, (c + 1) * MOE_F_CHUNK)
            g = jnp.dot(x, wg_bf[wslot, :, cs], preferred_element_type=F32) + bg_ref[0, 0, :, cs]
            u = jnp.dot(x, wu_bf[wslot, :, cs], preferred_element_type=F32) + bu_ref[0, 0, :, cs]
            g = jnp.minimum(g, SWIGLU_LIMIT)
            u = jnp.clip(u, -SWIGLU_LIMIT, SWIGLU_LIMIT)
            act_scr[:, cs] = (g * jax.nn.sigmoid(SWIGLU_ALPHA * g) * (u + 1.0)).astype(BF16)
        act = act_scr[...]
        for c in range(n_d):
            issue_part(n_f + c)
            c0 = c * MOE_D_CHUNK
            y = (jnp.dot(act, wd_bf[wslot, :, c0:c0 + MOE_D_CHUNK], preferred_element_type=F32)
                 + bd_ref[0, 0, :, c0:c0 + MOE_D_CHUNK])
            for j in range(MOE_D_CHUNK // LANES):
                col = c0 + j * LANES
                yj = y[:, j * LANES:(j + 1) * LANES]
                if col < half:
                    hi_scr[:, col:col + LANES] = _bf16_bits(yj)
                else:
                    s = (col - half) // LANES
                    o_ref[pl.ds(s, tm, stride=ROW_SUBLANES), :] = hi_scr[:, col - half:col - half + LANES] | (_bf16_bits(yj) >> 16)

        @pl.when(i == last)
        def _():
            for t in range(1, MOE_AHEAD + 1):
                tile_wait((i + t) % MOE_SLOTS)

    @pl.when(i >= n_used)
    def _():
        o_ref[...] = jnp.zeros_like(o_ref)


def _group_tables(tile_expert, n_used):
    n = tile_expert.shape[0]
    idx = jnp.arange(n, dtype=jnp.int32)
    first = jnp.concatenate([jnp.ones((1,), bool), tile_expert[1:] != tile_expert[:-1]])
    par = (jnp.cumsum(first.astype(jnp.int32)) - 1) % 2
    pos = idx - lax.cummax(jnp.where(first, idx, 0))
    later = (idx[None, :] > idx[:, None]) & (idx[None, :] < n_used[0]) & (tile_expert[None, :] != tile_expert[:, None])
    nxt_idx = jnp.min(jnp.where(later, idx[None, :], n), axis=1)
    nxt = jnp.where(nxt_idx < n, tile_expert[jnp.minimum(nxt_idx, n - 1)], -1)
    pn = jnp.concatenate([jnp.zeros((1,), jnp.int32), pos[:-1] + 1])
    return par.astype(jnp.int32), pos.astype(jnp.int32), nxt.astype(jnp.int32), pn.astype(jnp.int32)


def _moe(layer, tile_expert, row_token, n_used, h2p, wg, bg, wu, bu, wd, bd, tm):
    L, E, D, Fe = wg.shape
    n_tiles = tile_expert.shape[0]
    assert tm % (Fe // MOE_F_CHUNK + D // MOE_D_CHUNK) == 0 and (D // 2) % MOE_D_CHUNK == 0
    par, pos, nxt, pn = _group_tables(tile_expert, n_used)
    bspec = lambda w: pl.BlockSpec((1, 1, 1, w), lambda i, te, *_: (layer, te[i], 0, 0))
    hbm = pl.BlockSpec(memory_space=pl.ANY)
    return pl.pallas_call(
        functools.partial(_moe_kernel, layer=layer),
        out_shape=jax.ShapeDtypeStruct((n_tiles * tm * ROW_SUBLANES, LANES), jnp.uint32),
        grid_spec=pltpu.PrefetchScalarGridSpec(
            num_scalar_prefetch=7,
            grid=(n_tiles,),
            in_specs=[hbm, hbm, bspec(Fe), hbm, bspec(Fe), hbm, bspec(D)],
            out_specs=pl.BlockSpec((tm * ROW_SUBLANES, LANES), lambda i, *_: (i, 0)),
            scratch_shapes=[pltpu.VMEM((MOE_SLOTS, tm * ROW_SUBLANES, LANES), jnp.uint32),
                            pltpu.VMEM((tm, Fe), BF16), pltpu.VMEM((tm, D // 2), jnp.uint32),
                            pltpu.VMEM((2, D, Fe), BF16), pltpu.VMEM((2, D, Fe), BF16),
                            pltpu.VMEM((2, Fe, D), BF16),
                            pltpu.VMEM((D // 2, Fe), F32), pltpu.VMEM((Fe // 2, D), F32),
                            pltpu.SemaphoreType.DMA((MOE_SLOTS,)), pltpu.SemaphoreType.DMA],
        ),
        compiler_params=_cparams(("arbitrary",)),
        name="moe_experts",
    )(tile_expert, row_token, n_used, par, pos, nxt, pn, h2p, wg, bg.reshape(L, E, 1, Fe),
      wu, bu.reshape(L, E, 1, Fe), wd, bd.reshape(L, E, 1, D))


COMBINE_AHEAD = 2
COMBINE_SLOTS = COMBINE_AHEAD + 1


def _combine_kernel(dest_ref, yp_hbm, gate_ref, x_ref, gpost_ref, gtf_ref, o_ref, buf, sem):
    i = pl.program_id(0)
    n = pl.num_programs(0)
    tc, D = x_ref.shape
    half = D // 2
    K = TOPK_EXPERTS

    def row_copies(tile_id, r, slot):
        base = tile_id * tc * K
        for kk in range(K):
            d = dest_ref[base + r * K + kk]
            src = yp_hbm.at[pl.ds(pl.multiple_of(d * ROW_SUBLANES, ROW_SUBLANES), ROW_SUBLANES)]
            dst = buf.at[slot, kk, pl.ds(r * ROW_SUBLANES, ROW_SUBLANES)]
            pltpu.make_async_copy(src, dst, sem.at[slot]).start(priority=kk % 2)

    def tile_wait(slot):
        for kk in range(K):
            pltpu.make_async_copy(yp_hbm.at[pl.ds(0, tc * ROW_SUBLANES)], buf.at[slot, kk], sem.at[slot]).wait()

    @pl.when(i == 0)
    def _():
        for t in range(COMBINE_AHEAD):
            def issue(r, carry, t=t):
                row_copies(jnp.minimum(t, n - 1), r, t)
                return carry
            lax.fori_loop(0, tc, issue, 0)

    slot = i % COMBINE_SLOTS
    nslot = (i + COMBINE_AHEAD) % COMBINE_SLOTS
    ntile = jnp.minimum(i + COMBINE_AHEAD, n - 1)
    per = tc // ROW_SUBLANES

    tile_wait(slot)
    gates = gate_ref[...]
    gk = [jnp.broadcast_to(gates[:, kk:kk + 1], (tc, LANES)) for kk in range(K)]
    his, los = [], []
    ssq = jnp.zeros((tc, 1), F32)
    for s in range(ROW_SUBLANES):
        for r in range(s * per, (s + 1) * per):
            row_copies(ntile, r, nslot)
        hi = lo = None
        for kk in range(K):
            w = buf[slot, kk, pl.ds(s, tc, stride=ROW_SUBLANES), :]
            h, l = gk[kk] * _unpack_hi(w), gk[kk] * _unpack_lo(w)
            hi, lo = (h, l) if kk == 0 else (hi + h, lo + l)
        his.append(hi)
        los.append(lo)
        ssq = ssq + jnp.sum(hi * hi, axis=-1, keepdims=True) + jnp.sum(lo * lo, axis=-1, keepdims=True)
    inv = lax.rsqrt(ssq * (1.0 / D) + RMS_EPS)
    for s in range(ROW_SUBLANES):
        for col, y in ((s * LANES, his[s]), (half + s * LANES, los[s])):
            cs = slice(col, col + LANES)
            o_ref[:, cs] = x_ref[:, cs] + gtf_ref[0, :, cs] * ((y * inv) * gpost_ref[:, cs])

    @pl.when(i == n - 1)
    def _():
        for t in range(1, COMBINE_AHEAD + 1):
            tile_wait((i + t) % COMBINE_SLOTS)


def _combine(dest, yp, gates, x1, gpost, gtf, seq):
    T, D = x1.shape
    tc = 256
    per_b = seq // tc
    return pl.pallas_call(
        _combine_kernel,
        out_shape=jax.ShapeDtypeStruct((T, D), F32),
        grid_spec=pltpu.PrefetchScalarGridSpec(
            num_scalar_prefetch=1,
            grid=(T // tc,),
            in_specs=[pl.BlockSpec(memory_space=pl.ANY),
                      pl.BlockSpec((tc, LANES), lambda i, d: (i, 0)),
                      pl.BlockSpec((tc, D), lambda i, d: (i, 0)),
                      pl.BlockSpec((1, D), lambda i, d: (0, 0)),
                      pl.BlockSpec((1, 1, D), lambda i, d: (i // per_b, 0, 0))],
            out_specs=pl.BlockSpec((tc, D), lambda i, d: (i, 0)),
            scratch_shapes=[pltpu.VMEM((COMBINE_SLOTS, TOPK_EXPERTS, tc * ROW_SUBLANES, LANES), jnp.uint32),
                            pltpu.SemaphoreType.DMA((COMBINE_SLOTS,))],
        ),
        compiler_params=_cparams(("arbitrary",)),
        name="moe_combine",
    )(dest, yp, gates, x1, gpost, gtf)


def _dest_kernel(idx_ref, rank_ref, cnt_ref, dest_ref, *, tm_rows):
    tm = idx_ref.shape[0]
    tiles = jnp.floor((cnt_ref[0:1, :] + float(tm_rows - 1)) * (1.0 / tm_rows))
    r_i = lax.broadcasted_iota(jnp.int32, (LANES, LANES), 0)
    c_i = lax.broadcasted_iota(jnp.int32, (LANES, LANES), 1)
    before = jnp.where(r_i < c_i, 1.0, 0.0).astype(BF16)
    start = jnp.dot(jnp.broadcast_to(tiles, (8, LANES)).astype(BF16), before,
                    preferred_element_type=F32)[0:1] * float(tm_rows)
    lane = lax.broadcasted_iota(jnp.int32, (tm, LANES), 1)
    idx = idx_ref[...]
    rank = rank_ref[...]
    out = jnp.zeros((tm, LANES), jnp.int32)
    for kk in range(TOPK_EXPERTS):
        base = jnp.sum(jnp.where(lane == idx[:, kk:kk + 1], start, 0.0), axis=-1, keepdims=True)
        out = jnp.where(lane == kk, base.astype(jnp.int32) + rank[:, kk:kk + 1], out)
    dest_ref[...] = out


def _dest(idx, rank, cnt, tm_rows):
    T = idx.shape[0]
    tm = 2048
    assert tm_rows & (tm_rows - 1) == 0
    row = pl.BlockSpec((tm, LANES), lambda i: (i, 0))
    return pl.pallas_call(
        functools.partial(_dest_kernel, tm_rows=tm_rows),
        out_shape=jax.ShapeDtypeStruct((T, LANES), jnp.int32),
        grid=(T // tm,),
        in_specs=[row, row, pl.BlockSpec((8, LANES), lambda i: (0, 0))],
        out_specs=row,
        compiler_params=_cparams(("parallel",)),
        name="route_dest",
    )(idx, rank, cnt)


INVERT_UNROLL = 8


def _invert_kernel(dest_ref, lo_ref, hi_ref, out_ref):
    def fill(j, carry):
        out_ref[j] = 0
        return carry

    for e in range(lo_ref.shape[0]):
        lax.fori_loop(lo_ref[e], hi_ref[e], fill, 0)

    per = INVERT_UNROLL // TOPK_EXPERTS

    def body(b, carry):
        for u in range(INVERT_UNROLL):
            out_ref[dest_ref[b * INVERT_UNROLL + u]] = b * per + u // TOPK_EXPERTS
        return carry

    lax.fori_loop(0, dest_ref.shape[0] // INVERT_UNROLL, body, 0)


def _invert(dest, lo, hi, n_rows):
    smem = pl.BlockSpec(memory_space=pltpu.SMEM)
    return pl.pallas_call(
        _invert_kernel,
        out_shape=jax.ShapeDtypeStruct((n_rows,), jnp.int32),
        in_specs=[smem, smem, smem],
        out_specs=smem,
        name="route_invert",
    )(dest, lo, hi)


def _routing_tables(idx, rank, cnt, n_experts, tm):
    T = idx.shape[0]
    n_tiles = (T * TOPK_EXPERTS) // tm + n_experts
    counts = cnt[0, :n_experts].astype(jnp.int32)
    padded = ((counts + tm - 1) // tm) * tm
    pad_end = jnp.cumsum(padded)
    tile_start = jnp.arange(n_tiles, dtype=jnp.int32) * tm
    tile_expert = jnp.minimum(jnp.sum((pad_end[None, :] <= tile_start[:, None]).astype(jnp.int32), axis=1),
                              n_experts - 1)
    n_used = pad_end[-1:] // tm
    total = jnp.full((1,), n_tiles * tm, jnp.int32)
    lo = jnp.concatenate([pad_end - padded + counts, pad_end[-1:]])
    hi = jnp.concatenate([pad_end, total])
    dest = _dest(idx, rank, cnt, tm)[:, :TOPK_EXPERTS].reshape(-1)
    row_token = _invert(dest, lo, hi, n_tiles * tm)
    return dest, row_token, tile_expert, n_used


def kernel(x, c, w_ada, b_ada, g_pre_mix, g_post_mix, g_pre_ffn, g_post_ffn, w_in, w_out, pool_w, pool_scale,
           conv_w, rel_bias, router_w, router_b, w_gate, b_gate, w_up, b_up, w_down, b_down):
    B, S, D = x.shape
    L = w_ada.shape[0]
    T = B * S
    E = router_w.shape[-1]
    attn_w = D // 2
    heads = attn_w // HEAD_DIM
    moe_tm = 256
    assert D == 2 * ROW_SUBLANES * LANES

    mod = _ada(c, w_ada, b_ada)
    x2 = x.reshape(T, D)
    for l in range(L):
        sh_m, sc_m, gt_m, sh_f, sc_f, gt_f = [mod[l, :, i * D:(i + 1) * D].reshape(B, 1, D) for i in range(N_MOD)]
        row = lambda v: v.reshape(1, D)

        qkv, rest = _inproj(x2, row(g_pre_mix[l]), sc_m, sh_m, w_in[l].astype(BF16), S, attn_w)
        att = _attention(qkv, rel_bias, B, S, heads)
        pc = _poolconv(rest, pool_w[l], pool_scale[l], conv_w[l], B, S)

        wo = w_out[l].astype(BF16)
        rw = jnp.zeros((D, LANES), F32).at[:, :E].set(router_w[l])
        rwh = rw.astype(BF16)
        rwl = (rw - rwh.astype(F32)).astype(BF16)
        rb = jnp.full((1, LANES), NEG_INF, F32).at[0, :E].set(router_b[l])
        x1, h2, idx, gates, rank, cnt = _outproj(
            att, pc, x2, wo[:attn_w], wo[attn_w:], row(g_post_mix[l]), gt_m, row(g_pre_ffn[l]), sc_f, sh_f,
            rwh, rwl, rb, S)

        dest, row_token, tile_expert, n_used = _routing_tables(idx, rank, cnt, E, moe_tm)
        yp = _moe(l, tile_expert, row_token, n_used, h2, w_gate, b_gate, w_up, b_up, w_down, b_down, moe_tm)
        x2 = _combine(dest, yp, gates, x1, row(g_post_ffn[l]), gt_f, S)
    return x2.reshape(B, S, D)
```

```python
import functools
import math

import numpy as np
import jax
import jax.numpy as jnp
from jax import lax
from jax.experimental import pallas as pl
from jax.experimental.pallas import tpu as pltpu

F32 = jnp.float32
BF16 = jnp.bfloat16

LANES = 128
HEAD_DIM = 128
MOBA_BLOCK = 256
MOBA_TOPK = 3
REL_BUCKETS = 32
REL_MAX_DISTANCE = 128
POOL_WINDOWS = (2, 4, 8, 16)
GROUP_DIM = 128
CONV_K = 3
TOPK_EXPERTS = 4
SWIGLU_LIMIT = 7.0
SWIGLU_ALPHA = 1.702
N_MOD = 6
RMS_EPS = 1e-6
NEG_INF = -1e30
ATTN_SCALE = HEAD_DIM ** -0.5
LOG2E = math.log2(math.e)

VMEM_LIMIT = 56 * 1024 * 1024


def _cparams(sem):
    return pltpu.CompilerParams(dimension_semantics=sem, vmem_limit_bytes=VMEM_LIMIT)


def _rms(x):
    return x * lax.rsqrt(jnp.mean(x * x, axis=-1, keepdims=True) + RMS_EPS)


ROW_SUBLANES = 8


def _bf16_bits(v):
    return lax.bitcast_convert_type(v.astype(BF16).astype(F32), jnp.uint32)


def _pack_words(hi, lo):
    return _bf16_bits(hi) | (_bf16_bits(lo) >> 16)


def _unpack_hi(w):
    return lax.bitcast_convert_type(w & jnp.uint32(0xFFFF0000), F32)


def _unpack_lo(w):
    return lax.bitcast_convert_type(w << 16, F32)


def _ada_kernel(c_ref, w_ref, b_ref, o_ref):
    c = c_ref[...]
    ca = (c * jax.nn.sigmoid(c)).astype(BF16)
    o_ref[0] = jnp.dot(ca, w_ref[0].astype(BF16), preferred_element_type=F32) + b_ref[0]


def _ada(c, w_ada, b_ada):
    L, D, N = w_ada.shape
    B = c.shape[0]
    tn = 1024
    return pl.pallas_call(
        _ada_kernel,
        out_shape=jax.ShapeDtypeStruct((L, B, N), F32),
        grid=(L, N // tn),
        in_specs=[
            pl.BlockSpec((B, D), lambda l, j: (0, 0)),
            pl.BlockSpec((1, D, tn), lambda l, j: (l, 0, j)),
            pl.BlockSpec((1, 1, tn), lambda l, j: (l, 0, j)),
        ],
        out_specs=pl.BlockSpec((1, B, tn), lambda l, j: (l, 0, j)),
        compiler_params=_cparams(("parallel", "parallel")),
        name="ada_mod",
    )(c, w_ada, b_ada.reshape(L, 1, N))


def _inproj_kernel(x_ref, g_ref, sc_ref, sh_ref, w_ref, qkv_ref, rest_ref, h_scr, *, n_qkv):
    j = pl.program_id(1)

    @pl.when(j == 0)
    def _():
        h = _rms(x_ref[...]) * g_ref[...]
        h = h * (1.0 + sc_ref[0]) + sh_ref[0]
        h_scr[...] = h.astype(BF16)

    acc = jnp.dot(h_scr[...], w_ref[...], preferred_element_type=F32)

    @pl.when(j < n_qkv)
    def _():
        qkv_ref[...] = acc.astype(BF16)

    @pl.when(j >= n_qkv)
    def _():
        rest_ref[...] = acc


def _inproj(x2, g, sc, sh, w_bf, seq, attn_w):
    T, D = x2.shape
    N = w_bf.shape[1]
    tm, tn = 1024, 1024
    n_qkv = 3 * attn_w // tn
    n_rest = (N - 3 * attn_w) // tn
    per_b = seq // tm
    return pl.pallas_call(
        functools.partial(_inproj_kernel, n_qkv=n_qkv),
        out_shape=(jax.ShapeDtypeStruct((T, 3 * attn_w), BF16),
                   jax.ShapeDtypeStruct((T, N - 3 * attn_w), F32)),
        grid=(T // tm, n_qkv + n_rest),
        in_specs=[
            pl.BlockSpec((tm, D), lambda i, j: (i, 0)),
            pl.BlockSpec((1, D), lambda i, j: (0, 0)),
            pl.BlockSpec((1, 1, D), lambda i, j: (i // per_b, 0, 0)),
            pl.BlockSpec((1, 1, D), lambda i, j: (i // per_b, 0, 0)),
            pl.BlockSpec((D, tn), lambda i, j: (0, j)),
        ],
        out_specs=(
            pl.BlockSpec((tm, tn), lambda i, j: (i, jnp.minimum(j, n_qkv - 1))),
            pl.BlockSpec((tm, tn), lambda i, j: (i, jnp.maximum(j - n_qkv, 0))),
        ),
        scratch_shapes=[pltpu.VMEM((tm, D), BF16)],
        compiler_params=_cparams(("parallel", "arbitrary")),
        name="mixer_in_proj",
    )(x2, g, sc, sh, w_bf)


def _rel_bucket_np(n):
    n = np.maximum(n, 0)
    max_exact = REL_BUCKETS // 2
    nf = np.maximum(n, max_exact).astype(np.float32)
    large = max_exact + (np.log(nf / np.float32(max_exact)) / np.float32(math.log(REL_MAX_DISTANCE / max_exact))
                         * np.float32(REL_BUCKETS - max_exact)).astype(np.int32)
    large = np.minimum(large, REL_BUCKETS - 1)
    return np.where(n < max_exact, n, large).astype(np.int32)


def _bucket_tables(seq):
    qi = np.arange(MOBA_BLOCK)[:, None]
    ki = np.arange(MOBA_BLOCK)[None, :]
    own = np.where(ki <= qi, _rel_bucket_np(qi - ki), -1)
    prev = _rel_bucket_np(qi - ki + MOBA_BLOCK)
    far = _rel_bucket_np(np.arange(MOBA_BLOCK + 1, max(seq, MOBA_BLOCK + 2)))
    assert np.all(far == far[0])
    return np.stack([own, prev]).astype(np.int32), int(far[0])


ATTN_HEADS_PER_STEP = 2


def _attn_kernel(tab_ref, q_ref, k_ref, v_ref, bkt_ref, o_ref, bias_scr, *, nblk, far_bucket):
    hg = pl.program_id(0)
    b = pl.program_id(1)
    L = MOBA_BLOCK
    G = ATTN_HEADS_PER_STEP

    @pl.when(b == 0)
    def _():
        for g in range(G):
            for m in range(2):
                bk = bkt_ref[m]
                acc = jnp.full((L, L), NEG_INF, F32)
                for r in range(REL_BUCKETS):
                    acc = jnp.where(bk == r, tab_ref[hg * G + g, r] * LOG2E, acc)
                bias_scr[g, m] = acc

    dn = (((1,), (1,)), ((), ()))
    lane = lax.broadcasted_iota(jnp.int32, (L, LANES), 1)
    heads = []
    for g in range(G):
        cols = slice(g * HEAD_DIM, (g + 1) * HEAD_DIM)
        q, k, v = q_ref[:, cols], k_ref[:, cols], v_ref[:, cols]
        kmean = jnp.mean(k.astype(F32).reshape(nblk, L, HEAD_DIM), axis=1)
        kmean = jnp.concatenate([kmean, jnp.zeros((LANES - nblk, HEAD_DIM), F32)], axis=0).astype(BF16)
        gate = lax.dot_general(q, kmean, dn, preferred_element_type=F32)
        heads.append((q, k, v, gate, tab_ref[hg * G + g, far_bucket] * LOG2E))

    for a, g in [(a, g) for a in range(nblk) for g in range(G)]:
        q, k, v, gate, far_bias = heads[g]
        cols = slice(g * HEAD_DIM, (g + 1) * HEAD_DIM)
        qa = q[a * L:(a + 1) * L]
        n = (a + 1) * L
        s = lax.dot_general(qa, k[:n], dn, preferred_element_type=F32) * (ATTN_SCALE * LOG2E)
        hide = None
        if a > MOBA_TOPK:
            ga = jnp.where(lane < a, gate[a * L:(a + 1) * L], NEG_INF)
            cnt = jnp.zeros((L, LANES), F32)
            for i in range(a):
                gi = ga[:, i:i + 1]
                ge = jnp.where(gi >= ga, 1.0, 0.0)
                gt = jnp.where(gi > ga, 1.0, 0.0)
                cnt = cnt + jnp.where(lane > i, ge, gt)
            hide = jnp.where((cnt < float(MOBA_TOPK)) & (lane < a), 0.0, NEG_INF)
            hide_far = hide + far_bias
        pieces = []
        for j in range(a + 1):
            sj = s[:, j * L:(j + 1) * L]
            if j == a:
                sj = sj + bias_scr[g, 0]
            elif j == a - 1:
                sj = sj + bias_scr[g, 1]
                if hide is not None:
                    sj = sj + hide[:, j:j + 1]
            else:
                sj = sj + (far_bias if hide is None else hide_far[:, j:j + 1])
            pieces.append(sj)
        s = pieces[0] if a == 0 else jnp.concatenate(pieces, axis=1)
        m = jnp.max(s, axis=-1, keepdims=True)
        p = jnp.exp2(s - m)
        l = jnp.sum(p, axis=-1, keepdims=True)
        o = jnp.dot(p.astype(BF16), v[:n], preferred_element_type=F32) / l
        o_ref[a * L:(a + 1) * L, cols] = o.astype(o_ref.dtype)


def _attention(qkv, rel_bias, batch, seq, heads):
    T = qkv.shape[0]
    nblk = seq // MOBA_BLOCK
    G = ATTN_HEADS_PER_STEP
    assert heads % G == 0
    ng = heads // G
    bkt, far_bucket = _bucket_tables(seq)
    table = rel_bias.T.astype(F32)
    return pl.pallas_call(
        functools.partial(_attn_kernel, nblk=nblk, far_bucket=far_bucket),
        out_shape=jax.ShapeDtypeStruct((T, heads * HEAD_DIM), BF16),
        grid=(ng, batch),
        in_specs=[
            pl.BlockSpec(memory_space=pltpu.SMEM),
            pl.BlockSpec((seq, G * HEAD_DIM), lambda h, b: (b, h)),
            pl.BlockSpec((seq, G * HEAD_DIM), lambda h, b: (b, ng + h)),
            pl.BlockSpec((seq, G * HEAD_DIM), lambda h, b: (b, 2 * ng + h)),
            pl.BlockSpec((2, MOBA_BLOCK, MOBA_BLOCK), lambda h, b: (0, 0, 0)),
        ],
        out_specs=pl.BlockSpec((seq, G * HEAD_DIM), lambda h, b: (b, h)),
        scratch_shapes=[pltpu.VMEM((G, 2, MOBA_BLOCK, MOBA_BLOCK), F32)],
        compiler_params=_cparams(("arbitrary", "arbitrary")),
        name="moba_attention",
    )(table, qkv, qkv, qkv, jnp.asarray(bkt))


def _poolconv_kernel(u_ref, gb_ref, gc_ref, hc_ref, pw_ref, ps_ref, cw_ref, o_ref):
    S = u_ref.shape[0]
    G = len(POOL_WINDOWS)
    row = lax.broadcasted_iota(jnp.int32, (S, GROUP_DIM), 0)

    def shift(x, k):
        return jnp.where(row >= k, pltpu.roll(x, k, 0), 0.0)

    for gi, w in enumerate(POOL_WINDOWS):
        cs = slice(gi * GROUP_DIM, (gi + 1) * GROUP_DIM)
        ug = u_ref[:, cs]
        win = ug
        k = 1
        while k < w:
            win = win + shift(win, k)
            k *= 2
        count = jnp.minimum(row + 1, w).astype(F32)
        d = win / count - ug
        y = jnp.dot(d.astype(BF16), pw_ref[gi].astype(BF16), preferred_element_type=F32)
        o_ref[:, cs] = (y * ps_ref[:, cs]).astype(o_ref.dtype)

        uu = gc_ref[:, cs] * hc_ref[:, cs]
        conv = (cw_ref[0:1, cs] * shift(uu, 2) + cw_ref[1:2, cs] * shift(uu, 1)
                + cw_ref[2:3, cs] * uu)
        oc = slice((G + gi) * GROUP_DIM, (G + gi + 1) * GROUP_DIM)
        o_ref[:, oc] = (gb_ref[:, cs] * conv).astype(o_ref.dtype)


def _poolconv(rest, pool_w, pool_scale, conv_w, batch, seq):
    T = rest.shape[0]
    W = len(POOL_WINDOWS) * GROUP_DIM
    spec = lambda c: pl.BlockSpec((seq, W), lambda b: (b, c))
    return pl.pallas_call(
        _poolconv_kernel,
        out_shape=jax.ShapeDtypeStruct((T, 2 * W), BF16),
        grid=(batch,),
        in_specs=[
            spec(0), spec(1), spec(2), spec(3),
            pl.BlockSpec(pool_w.shape, lambda b: (0, 0, 0)),
            pl.BlockSpec((1, W), lambda b: (0, 0)),
            pl.BlockSpec((CONV_K, W), lambda b: (0, 0)),
        ],
        out_specs=pl.BlockSpec((seq, 2 * W), lambda b: (b, 0)),
        compiler_params=_cparams(("parallel",)),
        name="pool_conv_mixers",
    )(rest, rest, rest, rest, pool_w, pool_scale.reshape(1, W), conv_w)


def _outproj_kernel(att_ref, pc_ref, x_ref, wa_ref, wp_ref, gpost_ref, gtm_ref, gpre_ref, scf_ref, shf_ref,
                    rwh_ref, rwl_ref, rb_ref,
                    x1_ref, h2_ref, idx_ref, gate_ref, rank_ref, cnt_ref, carry_scr):
    i = pl.program_id(0)
    tm = x_ref.shape[0]

    @pl.when(i == 0)
    def _():
        carry_scr[...] = jnp.zeros_like(carry_scr)

    mix = (jnp.dot(att_ref[...], wa_ref[...], preferred_element_type=F32)
           + jnp.dot(pc_ref[...], wp_ref[...], preferred_element_type=F32))
    x1 = x_ref[...] + gtm_ref[0] * (_rms(mix) * gpost_ref[...])
    x1_ref[...] = x1
    h2 = (_rms(x1) * gpre_ref[...]) * (1.0 + scf_ref[0]) + shf_ref[0]
    half = h2.shape[1] // 2
    for s in range(ROW_SUBLANES):
        h2_ref[pl.ds(s, tm, stride=ROW_SUBLANES), :] = _pack_words(h2[:, s * LANES:(s + 1) * LANES],
                                      h2[:, half + s * LANES:half + (s + 1) * LANES])

    hi = h2.astype(BF16)
    lo = (h2 - hi.astype(F32)).astype(BF16)
    r2 = jnp.dot(hi, jnp.concatenate([rwh_ref[...], rwl_ref[...]], axis=1), preferred_element_type=F32)
    logits = (r2[:, :LANES] + r2[:, LANES:]
              + jnp.dot(lo, rwh_ref[...], preferred_element_type=F32)) + rb_ref[...]

    lane = lax.broadcasted_iota(jnp.int32, (tm, LANES), 1)
    work = logits
    mem = jnp.zeros((tm, LANES), F32)
    vals, hots = [], []
    idx_out = jnp.zeros((tm, LANES), jnp.int32)
    for kk in range(TOPK_EXPERTS):
        m = jnp.max(work, axis=-1, keepdims=True)
        ik = jnp.min(jnp.where(work == m, lane, LANES), axis=-1, keepdims=True)
        hot = lane == ik
        vals.append(m)
        hots.append(hot)
        idx_out = jnp.where(lane == kk, ik, idx_out)
        mem = jnp.where(hot, 1.0, mem)
        work = jnp.where(hot, -jnp.inf, work)
    idx_ref[...] = idx_out

    es = [jnp.exp(vk - vals[0]) for vk in vals]
    denom = es[0]
    for e in es[1:]:
        denom = denom + e
    gates = jnp.zeros((tm, LANES), F32)
    for kk in range(TOPK_EXPERTS):
        gates = jnp.where(lane == kk, es[kk] / denom, gates)
    gate_ref[...] = gates

    r_i = lax.broadcasted_iota(jnp.int32, (tm, tm), 0)
    c_i = lax.broadcasted_iota(jnp.int32, (tm, tm), 1)
    tri = jnp.where(c_i < r_i, 1.0, 0.0).astype(BF16)
    before = jnp.dot(tri, mem.astype(BF16), preferred_element_type=F32) + carry_scr[0:1, :]
    ranks = jnp.zeros((tm, LANES), jnp.int32)
    for kk in range(TOPK_EXPERTS):
        rk = jnp.sum(jnp.where(hots[kk], before, 0.0), axis=-1, keepdims=True)
        ranks = jnp.where(lane == kk, rk.astype(jnp.int32), ranks)
    rank_ref[...] = ranks
    carry = carry_scr[...] + jnp.sum(mem, axis=0, keepdims=True)
    carry_scr[...] = carry
    cnt_ref[...] = carry


def _outproj(att, pc, x2, wa, wp, gpost, gtm, gpre, scf, shf, rwh, rwl, rb, seq):
    T, D = x2.shape
    tm = 512
    per_b = seq // tm
    row = lambda w: pl.BlockSpec((tm, w), lambda i: (i, 0))
    const = lambda shape: pl.BlockSpec(shape, lambda i: tuple(0 for _ in shape), pipeline_mode=pl.Buffered(1))
    perb = pl.BlockSpec((1, 1, D), lambda i: (i // per_b, 0, 0))
    return pl.pallas_call(
        _outproj_kernel,
        out_shape=(jax.ShapeDtypeStruct((T, D), F32), jax.ShapeDtypeStruct((T * ROW_SUBLANES, LANES), jnp.uint32),
                   jax.ShapeDtypeStruct((T, LANES), jnp.int32), jax.ShapeDtypeStruct((T, LANES), F32),
                   jax.ShapeDtypeStruct((T, LANES), jnp.int32), jax.ShapeDtypeStruct((8, LANES), F32)),
        grid=(T // tm,),
        in_specs=[row(att.shape[1]), row(pc.shape[1]), row(D), const(wa.shape), const(wp.shape),
                  const((1, D)), perb, const((1, D)), perb, perb,
                  const(rwh.shape), const(rwl.shape), const((1, LANES))],
        out_specs=(row(D), pl.BlockSpec((tm * ROW_SUBLANES, LANES), lambda i: (i, 0)),
                   row(LANES), row(LANES), row(LANES), pl.BlockSpec((8, LANES), lambda i: (0, 0))),
        scratch_shapes=[pltpu.VMEM((8, LANES), F32)],
        compiler_params=_cparams(("arbitrary",)),
        name="out_proj_router",
    )(att, pc, x2, wa, wp, gpost, gtm, gpre, scf, shf, rwh, rwl, rb)


MOE_F_CHUNK = 256
MOE_D_CHUNK = 512
MOE_AHEAD = 2
MOE_SLOTS = MOE_AHEAD + 1


MOE_W_PHASES = 6


def _moe_kernel(te_ref, dest_ref, nu_ref, par_ref, pos_ref, nxt_ref, pn_ref, lo_ref, hi_ref,
                h_hbm, wg_hbm, bg_ref, wu_hbm, bu_ref, wd_hbm, bd_ref, o_ref,
                xbuf, act_scr, hi_scr, wg_bf, wu_bf, wd_bf, stg_a, stg_b, rt_ref, sem, wsem, *, layer):
    i = pl.program_id(0)
    tm = act_scr.shape[0]
    Fe = act_scr.shape[1]
    D = wd_bf.shape[-1]
    half = D // 2
    n_used = nu_ref[0]
    last = n_used - 1

    def piece(ph, expert):
        if ph < 4:
            src_ref, dst = (wg_hbm, stg_a) if ph < 2 else (wu_hbm, stg_a)
            rows = D // 2
        else:
            src_ref, dst, rows = wd_hbm, stg_b, Fe // 2
        r0 = (ph % 2) * rows
        return pltpu.make_async_copy(src_ref.at[layer, expert, pl.ds(r0, rows), :], dst, wsem)

    def round_piece(ph, slot):
        if ph < 4:
            rows = D // 2
            dst = wg_bf if ph < 2 else wu_bf
            dst[slot, (ph % 2) * rows:(ph % 2 + 1) * rows, :] = stg_a[...].astype(BF16)
        else:
            rows = Fe // 2
            wd_bf[slot, (ph % 2) * rows:(ph % 2 + 1) * rows, :] = stg_b[...].astype(BF16)

    def row_copy(tile_id, r, slot):
        tok = rt_ref[tile_id * tm + r]
        src = h_hbm.at[pl.ds(pl.multiple_of(tok * ROW_SUBLANES, ROW_SUBLANES), ROW_SUBLANES)]
        return pltpu.make_async_copy(src, xbuf.at[slot, pl.ds(r * ROW_SUBLANES, ROW_SUBLANES)], sem.at[slot])

    def tile_wait(slot):
        pltpu.make_async_copy(h_hbm.at[pl.ds(0, tm * ROW_SUBLANES)], xbuf.at[slot], sem.at[slot]).wait()

    @pl.when(i == 0)
    def _():
        _invert_rows(dest_ref, lo_ref, hi_ref, rt_ref)
        for t in range(MOE_AHEAD):
            def issue(r, carry, t=t):
                row_copy(jnp.minimum(t, last), r, t).start()
                return carry
            lax.fori_loop(0, tm, issue, 0)

    @pl.when(i < n_used)
    def _():
        slot = i % MOE_SLOTS
        nslot = (i + MOE_AHEAD) % MOE_SLOTS
        ntile = jnp.minimum(i + MOE_AHEAD, last)
        n_f, n_d = Fe // MOE_F_CHUNK, D // MOE_D_CHUNK
        per = tm // (n_f + n_d)

        def issue_part(p):
            for r in range(p * per, (p + 1) * per):
                row_copy(ntile, r, nslot).start()

        expert = te_ref[i]
        wslot = par_ref[i]
        k = pos_ref[i]
        nxt = nxt_ref[i]

        @pl.when(i == 0)
        def _():
            for ph in range(MOE_W_PHASES):
                piece(ph, expert).start()
                piece(ph, expert).wait()
                round_piece(ph, wslot)

        @pl.when((k == 0) & (i > 0))
        def _():
            done = jnp.minimum(pn_ref[i] - 1, MOE_W_PHASES)
            for ph in range(MOE_W_PHASES):
                @pl.when(ph >= done)
                def _(ph=ph):
                    @pl.when(ph > done)
                    def _():
                        piece(ph, expert).start()
                    piece(ph, expert).wait()
                    round_piece(ph, wslot)

        for ph in range(MOE_W_PHASES):
            @pl.when((k == ph + 1) & (nxt >= 0))
            def _(ph=ph):
                piece(ph, nxt).wait()
                round_piece(ph, 1 - wslot)

        for ph in range(MOE_W_PHASES):
            @pl.when((k == ph) & (nxt >= 0))
            def _(ph=ph):
                piece(ph, nxt).start()

        tile_wait(slot)
        words = [xbuf[slot, pl.ds(s, tm, stride=ROW_SUBLANES), :] for s in range(ROW_SUBLANES)]
        x = jnp.concatenate([_unpack_hi(w).astype(BF16) for w in words]
                            + [_unpack_lo(w).astype(BF16) for w in words], axis=1)
        for c in range(n_f):
            issue_part(c)
            cs = slice(c * MOE_F_CHUNK, (c + 1) * MOE_F_CHUNK)
            g = jnp.dot(x, wg_bf[wslot, :, cs], preferred_element_type=F32) + bg_ref[0, 0, :, cs]
            u = jnp.dot(x, wu_bf[wslot, :, cs], preferred_element_type=F32) + bu_ref[0, 0, :, cs]
            g = jnp.minimum(g, SWIGLU_LIMIT)
            u = jnp.clip(u, -SWIGLU_LIMIT, SWIGLU_LIMIT)
            act_scr[:, cs] = (g * jax.nn.sigmoid(SWIGLU_ALPHA * g) * (u + 1.0)).astype(BF16)
        act = act_scr[...]
        for c in range(n_d):
            issue_part(n_f + c)
            c0 = c * MOE_D_CHUNK
            y = (jnp.dot(act, wd_bf[wslot, :, c0:c0 + MOE_D_CHUNK], preferred_element_type=F32)
                 + bd_ref[0, 0, :, c0:c0 + MOE_D_CHUNK])
            for j in range(MOE_D_CHUNK // LANES):
                col = c0 + j * LANES
                yj = y[:, j * LANES:(j + 1) * LANES]
                if col < half:
                    hi_scr[:, col:col + LANES] = _bf16_bits(yj)
                else:
                    s = (col - half) // LANES
                    o_ref[pl.ds(s, tm, stride=ROW_SUBLANES), :] = hi_scr[:, col - half:col - half + LANES] | (_bf16_bits(yj) >> 16)

        @pl.when(i == last)
        def _():
            for t in range(1, MOE_AHEAD + 1):
                tile_wait((i + t) % MOE_SLOTS)

    @pl.when(i >= n_used)
    def _():
        o_ref[...] = jnp.zeros_like(o_ref)


def _group_tables(tile_expert, n_used):
    n = tile_expert.shape[0]
    idx = jnp.arange(n, dtype=jnp.int32)
    first = jnp.concatenate([jnp.ones((1,), bool), tile_expert[1:] != tile_expert[:-1]])
    par = (jnp.cumsum(first.astype(jnp.int32)) - 1) % 2
    pos = idx - lax.cummax(jnp.where(first, idx, 0))
    later = (idx[None, :] > idx[:, None]) & (idx[None, :] < n_used[0]) & (tile_expert[None, :] != tile_expert[:, None])
    nxt_idx = jnp.min(jnp.where(later, idx[None, :], n), axis=1)
    nxt = jnp.where(nxt_idx < n, tile_expert[jnp.minimum(nxt_idx, n - 1)], -1)
    pn = jnp.concatenate([jnp.zeros((1,), jnp.int32), pos[:-1] + 1])
    return par.astype(jnp.int32), pos.astype(jnp.int32), nxt.astype(jnp.int32), pn.astype(jnp.int32)


def _moe(layer, tile_expert, dest, pad_rows, n_used, h2p, wg, bg, wu, bu, wd, bd, tm):
    L, E, D, Fe = wg.shape
    n_tiles = tile_expert.shape[0]
    lo, hi = pad_rows
    assert tm % (Fe // MOE_F_CHUNK + D // MOE_D_CHUNK) == 0 and (D // 2) % MOE_D_CHUNK == 0
    par, pos, nxt, pn = _group_tables(tile_expert, n_used)
    bspec = lambda w: pl.BlockSpec((1, 1, 1, w), lambda i, te, *_: (layer, te[i], 0, 0))
    hbm = pl.BlockSpec(memory_space=pl.ANY)
    return pl.pallas_call(
        functools.partial(_moe_kernel, layer=layer),
        out_shape=jax.ShapeDtypeStruct((n_tiles * tm * ROW_SUBLANES, LANES), jnp.uint32),
        grid_spec=pltpu.PrefetchScalarGridSpec(
            num_scalar_prefetch=9,
            grid=(n_tiles,),
            in_specs=[hbm, hbm, bspec(Fe), hbm, bspec(Fe), hbm, bspec(D)],
            out_specs=pl.BlockSpec((tm * ROW_SUBLANES, LANES), lambda i, *_: (i, 0)),
            scratch_shapes=[pltpu.VMEM((MOE_SLOTS, tm * ROW_SUBLANES, LANES), jnp.uint32),
                            pltpu.VMEM((tm, Fe), BF16), pltpu.VMEM((tm, D // 2), jnp.uint32),
                            pltpu.VMEM((2, D, Fe), BF16), pltpu.VMEM((2, D, Fe), BF16),
                            pltpu.VMEM((2, Fe, D), BF16),
                            pltpu.VMEM((D // 2, Fe), F32), pltpu.VMEM((Fe // 2, D), F32),
                            pltpu.SMEM((n_tiles * tm,), jnp.int32),
                            pltpu.SemaphoreType.DMA((MOE_SLOTS,)), pltpu.SemaphoreType.DMA],
        ),
        compiler_params=_cparams(("arbitrary",)),
        name="moe_experts",
    )(tile_expert, dest, n_used, par, pos, nxt, pn, lo, hi, h2p, wg, bg.reshape(L, E, 1, Fe),
      wu, bu.reshape(L, E, 1, Fe), wd, bd.reshape(L, E, 1, D))


COMBINE_AHEAD = 2
COMBINE_SLOTS = COMBINE_AHEAD + 1


def _combine_kernel(dest_ref, yp_hbm, gate_ref, x_ref, gpost_ref, gtf_ref, o_ref, buf, sem):
    i = pl.program_id(0)
    n = pl.num_programs(0)
    tc, D = x_ref.shape
    half = D // 2
    K = TOPK_EXPERTS

    def row_copies(tile_id, r, slot):
        base = tile_id * tc * K
        for kk in range(K):
            d = dest_ref[base + r * K + kk]
            src = yp_hbm.at[pl.ds(pl.multiple_of(d * ROW_SUBLANES, ROW_SUBLANES), ROW_SUBLANES)]
            dst = buf.at[slot, kk, pl.ds(r * ROW_SUBLANES, ROW_SUBLANES)]
            pltpu.make_async_copy(src, dst, sem.at[slot]).start(priority=kk % 2)

    def tile_wait(slot):
        for kk in range(K):
            pltpu.make_async_copy(yp_hbm.at[pl.ds(0, tc * ROW_SUBLANES)], buf.at[slot, kk], sem.at[slot]).wait()

    @pl.when(i == 0)
    def _():
        for t in range(COMBINE_AHEAD):
            def issue(r, carry, t=t):
                row_copies(jnp.minimum(t, n - 1), r, t)
                return carry
            lax.fori_loop(0, tc, issue, 0)

    slot = i % COMBINE_SLOTS
    nslot = (i + COMBINE_AHEAD) % COMBINE_SLOTS
    ntile = jnp.minimum(i + COMBINE_AHEAD, n - 1)
    per = tc // ROW_SUBLANES

    tile_wait(slot)
    gates = gate_ref[...]
    gk = [jnp.broadcast_to(gates[:, kk:kk + 1], (tc, LANES)) for kk in range(K)]
    his, los = [], []
    ssq = jnp.zeros((tc, 1), F32)
    for s in range(ROW_SUBLANES):
        for r in range(s * per, (s + 1) * per):
            row_copies(ntile, r, nslot)
        hi = lo = None
        for kk in range(K):
            w = buf[slot, kk, pl.ds(s, tc, stride=ROW_SUBLANES), :]
            h, l = gk[kk] * _unpack_hi(w), gk[kk] * _unpack_lo(w)
            hi, lo = (h, l) if kk == 0 else (hi + h, lo + l)
        his.append(hi)
        los.append(lo)
        ssq = ssq + jnp.sum(hi * hi, axis=-1, keepdims=True) + jnp.sum(lo * lo, axis=-1, keepdims=True)
    inv = lax.rsqrt(ssq * (1.0 / D) + RMS_EPS)
    for s in range(ROW_SUBLANES):
        for col, y in ((s * LANES, his[s]), (half + s * LANES, los[s])):
            cs = slice(col, col + LANES)
            o_ref[:, cs] = x_ref[:, cs] + gtf_ref[0, :, cs] * ((y * inv) * gpost_ref[:, cs])

    @pl.when(i == n - 1)
    def _():
        for t in range(1, COMBINE_AHEAD + 1):
            tile_wait((i + t) % COMBINE_SLOTS)


def _combine(dest, yp, gates, x1, gpost, gtf, seq):
    T, D = x1.shape
    tc = 256
    per_b = seq // tc
    return pl.pallas_call(
        _combine_kernel,
        out_shape=jax.ShapeDtypeStruct((T, D), F32),
        grid_spec=pltpu.PrefetchScalarGridSpec(
            num_scalar_prefetch=1,
            grid=(T // tc,),
            in_specs=[pl.BlockSpec(memory_space=pl.ANY),
                      pl.BlockSpec((tc, LANES), lambda i, d: (i, 0)),
                      pl.BlockSpec((tc, D), lambda i, d: (i, 0)),
                      pl.BlockSpec((1, D), lambda i, d: (0, 0)),
                      pl.BlockSpec((1, 1, D), lambda i, d: (i // per_b, 0, 0))],
            out_specs=pl.BlockSpec((tc, D), lambda i, d: (i, 0)),
            scratch_shapes=[pltpu.VMEM((COMBINE_SLOTS, TOPK_EXPERTS, tc * ROW_SUBLANES, LANES), jnp.uint32),
                            pltpu.SemaphoreType.DMA((COMBINE_SLOTS,))],
        ),
        compiler_params=_cparams(("arbitrary",)),
        name="moe_combine",
    )(dest, yp, gates, x1, gpost, gtf)


def _dest_kernel(idx_ref, rank_ref, cnt_ref, dest_ref, *, tm_rows):
    tm = idx_ref.shape[0]
    tiles = jnp.floor((cnt_ref[0:1, :] + float(tm_rows - 1)) * (1.0 / tm_rows))
    r_i = lax.broadcasted_iota(jnp.int32, (LANES, LANES), 0)
    c_i = lax.broadcasted_iota(jnp.int32, (LANES, LANES), 1)
    before = jnp.where(r_i < c_i, 1.0, 0.0).astype(BF16)
    start = jnp.dot(jnp.broadcast_to(tiles, (8, LANES)).astype(BF16), before,
                    preferred_element_type=F32)[0:1] * float(tm_rows)
    lane = lax.broadcasted_iota(jnp.int32, (tm, LANES), 1)
    idx = idx_ref[...]
    rank = rank_ref[...]
    out = jnp.zeros((tm, LANES), jnp.int32)
    for kk in range(TOPK_EXPERTS):
        base = jnp.sum(jnp.where(lane == idx[:, kk:kk + 1], start, 0.0), axis=-1, keepdims=True)
        out = jnp.where(lane == kk, base.astype(jnp.int32) + rank[:, kk:kk + 1], out)
    dest_ref[...] = out


def _dest(idx, rank, cnt, tm_rows):
    T = idx.shape[0]
    tm = 2048
    assert tm_rows & (tm_rows - 1) == 0
    row = pl.BlockSpec((tm, LANES), lambda i: (i, 0))
    return pl.pallas_call(
        functools.partial(_dest_kernel, tm_rows=tm_rows),
        out_shape=jax.ShapeDtypeStruct((T, LANES), jnp.int32),
        grid=(T // tm,),
        in_specs=[row, row, pl.BlockSpec((8, LANES), lambda i: (0, 0))],
        out_specs=row,
        compiler_params=_cparams(("parallel",)),
        name="route_dest",
    )(idx, rank, cnt)


INVERT_UNROLL = 8


def _invert_rows(dest_ref, lo_ref, hi_ref, out_ref):
    def fill(j, carry):
        out_ref[j] = 0
        return carry

    for e in range(lo_ref.shape[0]):
        lax.fori_loop(lo_ref[e], hi_ref[e], fill, 0)

    per = INVERT_UNROLL // TOPK_EXPERTS

    def body(b, carry):
        for u in range(INVERT_UNROLL):
            out_ref[dest_ref[b * INVERT_UNROLL + u]] = b * per + u // TOPK_EXPERTS
        return carry

    lax.fori_loop(0, dest_ref.shape[0] // INVERT_UNROLL, body, 0)


def _routing_tables(idx, rank, cnt, n_experts, tm):
    T = idx.shape[0]
    n_tiles = (T * TOPK_EXPERTS) // tm + n_experts
    counts = cnt[0, :n_experts].astype(jnp.int32)
    padded = ((counts + tm - 1) // tm) * tm
    pad_end = jnp.cumsum(padded)
    tile_start = jnp.arange(n_tiles, dtype=jnp.int32) * tm
    tile_expert = jnp.minimum(jnp.sum((pad_end[None, :] <= tile_start[:, None]).astype(jnp.int32), axis=1),
                              n_experts - 1)
    n_used = pad_end[-1:] // tm
    pad_rows = (pad_end - padded + counts, pad_end)
    dest = _dest(idx, rank, cnt, tm)[:, :TOPK_EXPERTS].reshape(-1)
    return dest, pad_rows, tile_expert, n_used


def kernel(x, c, w_ada, b_ada, g_pre_mix, g_post_mix, g_pre_ffn, g_post_ffn, w_in, w_out, pool_w, pool_scale,
           conv_w, rel_bias, router_w, router_b, w_gate, b_gate, w_up, b_up, w_down, b_down):
    B, S, D = x.shape
    L = w_ada.shape[0]
    T = B * S
    E = router_w.shape[-1]
    attn_w = D // 2
    heads = attn_w // HEAD_DIM
    moe_tm = 256
    assert D == 2 * ROW_SUBLANES * LANES

    mod = _ada(c, w_ada, b_ada)
    x2 = x.reshape(T, D)
    for l in range(L):
        sh_m, sc_m, gt_m, sh_f, sc_f, gt_f = [mod[l, :, i * D:(i + 1) * D].reshape(B, 1, D) for i in range(N_MOD)]
        row = lambda v: v.reshape(1, D)

        qkv, rest = _inproj(x2, row(g_pre_mix[l]), sc_m, sh_m, w_in[l].astype(BF16), S, attn_w)
        att = _attention(qkv, rel_bias, B, S, heads)
        pc = _poolconv(rest, pool_w[l], pool_scale[l], conv_w[l], B, S)

        wo = w_out[l].astype(BF16)
        rw = jnp.zeros((D, LANES), F32).at[:, :E].set(router_w[l])
        rwh = rw.astype(BF16)
        rwl = (rw - rwh.astype(F32)).astype(BF16)
        rb = jnp.full((1, LANES), NEG_INF, F32).at[0, :E].set(router_b[l])
        x1, h2, idx, gates, rank, cnt = _outproj(
            att, pc, x2, wo[:attn_w], wo[attn_w:], row(g_post_mix[l]), gt_m, row(g_pre_ffn[l]), sc_f, sh_f,
            rwh, rwl, rb, S)

        dest, pad_rows, tile_expert, n_used = _routing_tables(idx, rank, cnt, E, moe_tm)
        yp = _moe(l, tile_expert, dest, pad_rows, n_used, h2, w_gate, b_gate, w_up, b_up, w_down, b_down, moe_tm)
        x2 = _combine(dest, yp, gates, x1, row(g_post_ffn[l]), gt_f, S)
    return x2.reshape(B, S, D)
```

```python
import functools
import math

import numpy as np
import jax
import jax.numpy as jnp
from jax import lax
from jax.experimental import pallas as pl
from jax.experimental.pallas import tpu as pltpu

F32 = jnp.float32
BF16 = jnp.bfloat16

LANES = 128
HEAD_DIM = 128
MOBA_BLOCK = 256
MOBA_TOPK = 3
REL_BUCKETS = 32
REL_MAX_DISTANCE = 128
POOL_WINDOWS = (2, 4, 8, 16)
GROUP_DIM = 128
CONV_K = 3
TOPK_EXPERTS = 4
SWIGLU_LIMIT = 7.0
SWIGLU_ALPHA = 1.702
N_MOD = 6
RMS_EPS = 1e-6
NEG_INF = -1e30
ATTN_SCALE = HEAD_DIM ** -0.5
LOG2E = math.log2(math.e)

VMEM_LIMIT = 56 * 1024 * 1024


def _cparams(sem):
    return pltpu.CompilerParams(dimension_semantics=sem, vmem_limit_bytes=VMEM_LIMIT)


def _rms(x):
    return x * lax.rsqrt(jnp.mean(x * x, axis=-1, keepdims=True) + RMS_EPS)


ROW_SUBLANES = 8


def _bf16_bits(v):
    return lax.bitcast_convert_type(v.astype(BF16).astype(F32), jnp.uint32)


def _pack_words(hi, lo):
    return _bf16_bits(hi) | (_bf16_bits(lo) >> 16)


def _unpack_hi(w):
    return lax.bitcast_convert_type(w & jnp.uint32(0xFFFF0000), F32)


def _unpack_lo(w):
    return lax.bitcast_convert_type(w << 16, F32)


def _ada_kernel(c_ref, w_ref, b_ref, o_ref):
    c = c_ref[...]
    ca = (c * jax.nn.sigmoid(c)).astype(BF16)
    o_ref[0] = jnp.dot(ca, w_ref[0].astype(BF16), preferred_element_type=F32) + b_ref[0]


def _ada(c, w_ada, b_ada):
    L, D, N = w_ada.shape
    B = c.shape[0]
    tn = 1024
    return pl.pallas_call(
        _ada_kernel,
        out_shape=jax.ShapeDtypeStruct((L, B, N), F32),
        grid=(L, N // tn),
        in_specs=[
            pl.BlockSpec((B, D), lambda l, j: (0, 0)),
            pl.BlockSpec((1, D, tn), lambda l, j: (l, 0, j)),
            pl.BlockSpec((1, 1, tn), lambda l, j: (l, 0, j)),
        ],
        out_specs=pl.BlockSpec((1, B, tn), lambda l, j: (l, 0, j)),
        compiler_params=_cparams(("parallel", "parallel")),
        name="ada_mod",
    )(c, w_ada, b_ada.reshape(L, 1, N))


def _inproj_kernel(x_ref, g_ref, sc_ref, sh_ref, w_ref, qkv_ref, rest_ref, h_scr, *, n_qkv):
    j = pl.program_id(1)

    @pl.when(j == 0)
    def _():
        h = _rms(x_ref[...]) * g_ref[...]
        h = h * (1.0 + sc_ref[0]) + sh_ref[0]
        h_scr[...] = h.astype(BF16)

    acc = jnp.dot(h_scr[...], w_ref[...], preferred_element_type=F32)

    @pl.when(j < n_qkv)
    def _():
        qkv_ref[...] = acc.astype(BF16)

    @pl.when(j >= n_qkv)
    def _():
        rest_ref[...] = acc


def _inproj(x2, g, sc, sh, w_bf, seq, attn_w):
    T, D = x2.shape
    N = w_bf.shape[1]
    tm, tn = 1024, 1024
    n_qkv = 3 * attn_w // tn
    n_rest = (N - 3 * attn_w) // tn
    per_b = seq // tm
    return pl.pallas_call(
        functools.partial(_inproj_kernel, n_qkv=n_qkv),
        out_shape=(jax.ShapeDtypeStruct((T, 3 * attn_w), BF16),
                   jax.ShapeDtypeStruct((T, N - 3 * attn_w), F32)),
        grid=(T // tm, n_qkv + n_rest),
        in_specs=[
            pl.BlockSpec((tm, D), lambda i, j: (i, 0)),
            pl.BlockSpec((1, D), lambda i, j: (0, 0)),
            pl.BlockSpec((1, 1, D), lambda i, j: (i // per_b, 0, 0)),
            pl.BlockSpec((1, 1, D), lambda i, j: (i // per_b, 0, 0)),
            pl.BlockSpec((D, tn), lambda i, j: (0, j)),
        ],
        out_specs=(
            pl.BlockSpec((tm, tn), lambda i, j: (i, jnp.minimum(j, n_qkv - 1))),
            pl.BlockSpec((tm, tn), lambda i, j: (i, jnp.maximum(j - n_qkv, 0))),
        ),
        scratch_shapes=[pltpu.VMEM((tm, D), BF16)],
        compiler_params=_cparams(("parallel", "arbitrary")),
        name="mixer_in_proj",
    )(x2, g, sc, sh, w_bf)


def _rel_bucket_np(n):
    n = np.maximum(n, 0)
    max_exact = REL_BUCKETS // 2
    nf = np.maximum(n, max_exact).astype(np.float32)
    large = max_exact + (np.log(nf / np.float32(max_exact)) / np.float32(math.log(REL_MAX_DISTANCE / max_exact))
                         * np.float32(REL_BUCKETS - max_exact)).astype(np.int32)
    large = np.minimum(large, REL_BUCKETS - 1)
    return np.where(n < max_exact, n, large).astype(np.int32)


def _bucket_tables(seq):
    qi = np.arange(MOBA_BLOCK)[:, None]
    ki = np.arange(MOBA_BLOCK)[None, :]
    own = np.where(ki <= qi, _rel_bucket_np(qi - ki), -1)
    prev = _rel_bucket_np(qi - ki + MOBA_BLOCK)
    far = _rel_bucket_np(np.arange(MOBA_BLOCK + 1, max(seq, MOBA_BLOCK + 2)))
    assert np.all(far == far[0])
    return np.stack([own, prev]).astype(np.int32), int(far[0])


ATTN_HEADS_PER_STEP = 2


def _attn_kernel(tab_ref, q_ref, k_ref, v_ref, bkt_ref, o_ref, bias_scr, *, nblk, far_bucket):
    hg = pl.program_id(0)
    b = pl.program_id(1)
    L = MOBA_BLOCK
    G = ATTN_HEADS_PER_STEP

    @pl.when(b == 0)
    def _():
        for g in range(G):
            for m in range(2):
                bk = bkt_ref[m]
                acc = jnp.full((L, L), NEG_INF, F32)
                for r in range(REL_BUCKETS):
                    acc = jnp.where(bk == r, tab_ref[hg * G + g, r] * LOG2E, acc)
                bias_scr[g, m] = acc

    dn = (((1,), (1,)), ((), ()))
    lane = lax.broadcasted_iota(jnp.int32, (L, LANES), 1)
    heads = []
    for g in range(G):
        cols = slice(g * HEAD_DIM, (g + 1) * HEAD_DIM)
        q, k, v = q_ref[:, cols], k_ref[:, cols], v_ref[:, cols]
        kmean = jnp.mean(k.astype(F32).reshape(nblk, L, HEAD_DIM), axis=1)
        kmean = jnp.concatenate([kmean, jnp.zeros((LANES - nblk, HEAD_DIM), F32)], axis=0).astype(BF16)
        gate = lax.dot_general(q, kmean, dn, preferred_element_type=F32)
        heads.append((q, k, v, gate, tab_ref[hg * G + g, far_bucket] * LOG2E))

    for a, g in [(a, g) for a in range(nblk) for g in range(G)]:
        q, k, v, gate, far_bias = heads[g]
        cols = slice(g * HEAD_DIM, (g + 1) * HEAD_DIM)
        qa = q[a * L:(a + 1) * L]
        n = (a + 1) * L
        s = lax.dot_general(qa, k[:n], dn, preferred_element_type=F32) * (ATTN_SCALE * LOG2E)
        hide = None
        if a > MOBA_TOPK:
            ga = jnp.where(lane < a, gate[a * L:(a + 1) * L], NEG_INF)
            cnt = jnp.zeros((L, LANES), F32)
            for i in range(a):
                gi = ga[:, i:i + 1]
                ge = jnp.where(gi >= ga, 1.0, 0.0)
                gt = jnp.where(gi > ga, 1.0, 0.0)
                cnt = cnt + jnp.where(lane > i, ge, gt)
            hide = jnp.where((cnt < float(MOBA_TOPK)) & (lane < a), 0.0, NEG_INF)
            hide_far = hide + far_bias
        pieces = []
        for j in range(a + 1):
            sj = s[:, j * L:(j + 1) * L]
            if j == a:
                sj = sj + bias_scr[g, 0]
            elif j == a - 1:
                sj = sj + bias_scr[g, 1]
                if hide is not None:
                    sj = sj + hide[:, j:j + 1]
            else:
                sj = sj + (far_bias if hide is None else hide_far[:, j:j + 1])
            pieces.append(sj)
        s = pieces[0] if a == 0 else jnp.concatenate(pieces, axis=1)
        m = jnp.max(s, axis=-1, keepdims=True)
        p = jnp.exp2(s - m)
        l = jnp.sum(p, axis=-1, keepdims=True)
        o = jnp.dot(p.astype(BF16), v[:n], preferred_element_type=F32) / l
        o_ref[a * L:(a + 1) * L, cols] = o.astype(o_ref.dtype)


def _attention(qkv, rel_bias, batch, seq, heads):
    T = qkv.shape[0]
    nblk = seq // MOBA_BLOCK
    G = ATTN_HEADS_PER_STEP
    assert heads % G == 0
    ng = heads // G
    bkt, far_bucket = _bucket_tables(seq)
    table = rel_bias.T.astype(F32)
    return pl.pallas_call(
        functools.partial(_attn_kernel, nblk=nblk, far_bucket=far_bucket),
        out_shape=jax.ShapeDtypeStruct((T, heads * HEAD_DIM), BF16),
        grid=(ng, batch),
        in_specs=[
            pl.BlockSpec(memory_space=pltpu.SMEM),
            pl.BlockSpec((seq, G * HEAD_DIM), lambda h, b: (b, h)),
            pl.BlockSpec((seq, G * HEAD_DIM), lambda h, b: (b, ng + h)),
            pl.BlockSpec((seq, G * HEAD_DIM), lambda h, b: (b, 2 * ng + h)),
            pl.BlockSpec((2, MOBA_BLOCK, MOBA_BLOCK), lambda h, b: (0, 0, 0)),
        ],
        out_specs=pl.BlockSpec((seq, G * HEAD_DIM), lambda h, b: (b, h)),
        scratch_shapes=[pltpu.VMEM((G, 2, MOBA_BLOCK, MOBA_BLOCK), F32)],
        compiler_params=_cparams(("arbitrary", "arbitrary")),
        name="moba_attention",
    )(table, qkv, qkv, qkv, jnp.asarray(bkt))


def _poolconv_kernel(u_ref, gb_ref, gc_ref, hc_ref, pw_ref, ps_ref, cw_ref, o_ref):
    S = u_ref.shape[0]
    G = len(POOL_WINDOWS)
    row = lax.broadcasted_iota(jnp.int32, (S, GROUP_DIM), 0)

    def shift(x, k):
        return jnp.where(row >= k, pltpu.roll(x, k, 0), 0.0)

    for gi, w in enumerate(POOL_WINDOWS):
        cs = slice(gi * GROUP_DIM, (gi + 1) * GROUP_DIM)
        ug = u_ref[:, cs]
        win = ug
        k = 1
        while k < w:
            win = win + shift(win, k)
            k *= 2
        count = jnp.minimum(row + 1, w).astype(F32)
        d = win / count - ug
        y = jnp.dot(d.astype(BF16), pw_ref[gi].astype(BF16), preferred_element_type=F32)
        o_ref[:, cs] = (y * ps_ref[:, cs]).astype(o_ref.dtype)

        uu = gc_ref[:, cs] * hc_ref[:, cs]
        conv = (cw_ref[0:1, cs] * shift(uu, 2) + cw_ref[1:2, cs] * shift(uu, 1)
                + cw_ref[2:3, cs] * uu)
        oc = slice((G + gi) * GROUP_DIM, (G + gi + 1) * GROUP_DIM)
        o_ref[:, oc] = (gb_ref[:, cs] * conv).astype(o_ref.dtype)


def _poolconv(rest, pool_w, pool_scale, conv_w, batch, seq):
    T = rest.shape[0]
    W = len(POOL_WINDOWS) * GROUP_DIM
    spec = lambda c: pl.BlockSpec((seq, W), lambda b: (b, c))
    return pl.pallas_call(
        _poolconv_kernel,
        out_shape=jax.ShapeDtypeStruct((T, 2 * W), BF16),
        grid=(batch,),
        in_specs=[
            spec(0), spec(1), spec(2), spec(3),
            pl.BlockSpec(pool_w.shape, lambda b: (0, 0, 0)),
            pl.BlockSpec((1, W), lambda b: (0, 0)),
            pl.BlockSpec((CONV_K, W), lambda b: (0, 0)),
        ],
        out_specs=pl.BlockSpec((seq, 2 * W), lambda b: (b, 0)),
        compiler_params=_cparams(("parallel",)),
        name="pool_conv_mixers",
    )(rest, rest, rest, rest, pool_w, pool_scale.reshape(1, W), conv_w)


def _outproj_kernel(att_ref, pc_ref, x_ref, wa_ref, wp_ref, gpost_ref, gtm_ref, gpre_ref, scf_ref, shf_ref,
                    rwh_ref, rwl_ref, rb_ref,
                    x1_ref, h2_ref, idx_ref, gate_ref, rank_ref, cnt_ref, carry_scr):
    i = pl.program_id(0)
    tm = x_ref.shape[0]

    @pl.when(i == 0)
    def _():
        carry_scr[...] = jnp.zeros_like(carry_scr)

    mix = (jnp.dot(att_ref[...], wa_ref[...], preferred_element_type=F32)
           + jnp.dot(pc_ref[...], wp_ref[...], preferred_element_type=F32))
    x1 = x_ref[...] + gtm_ref[0] * (_rms(mix) * gpost_ref[...])
    x1_ref[...] = x1
    h2 = (_rms(x1) * gpre_ref[...]) * (1.0 + scf_ref[0]) + shf_ref[0]
    half = h2.shape[1] // 2
    for s in range(ROW_SUBLANES):
        h2_ref[pl.ds(s, tm, stride=ROW_SUBLANES), :] = _pack_words(h2[:, s * LANES:(s + 1) * LANES],
                                      h2[:, half + s * LANES:half + (s + 1) * LANES])

    hi = h2.astype(BF16)
    lo = (h2 - hi.astype(F32)).astype(BF16)
    r2 = jnp.dot(hi, jnp.concatenate([rwh_ref[...], rwl_ref[...]], axis=1), preferred_element_type=F32)
    logits = (r2[:, :LANES] + r2[:, LANES:]
              + jnp.dot(lo, rwh_ref[...], preferred_element_type=F32)) + rb_ref[...]

    lane = lax.broadcasted_iota(jnp.int32, (tm, LANES), 1)
    work = logits
    mem = jnp.zeros((tm, LANES), F32)
    vals, hots = [], []
    idx_out = jnp.zeros((tm, LANES), jnp.int32)
    for kk in range(TOPK_EXPERTS):
        m = jnp.max(work, axis=-1, keepdims=True)
        ik = jnp.min(jnp.where(work == m, lane, LANES), axis=-1, keepdims=True)
        hot = lane == ik
        vals.append(m)
        hots.append(hot)
        idx_out = jnp.where(lane == kk, ik, idx_out)
        mem = jnp.where(hot, 1.0, mem)
        work = jnp.where(hot, -jnp.inf, work)
    idx_ref[...] = idx_out

    es = [jnp.exp(vk - vals[0]) for vk in vals]
    denom = es[0]
    for e in es[1:]:
        denom = denom + e
    gates = jnp.zeros((tm, LANES), F32)
    for kk in range(TOPK_EXPERTS):
        gates = jnp.where(lane == kk, es[kk] / denom, gates)
    gate_ref[...] = gates

    r_i = lax.broadcasted_iota(jnp.int32, (tm, tm), 0)
    c_i = lax.broadcasted_iota(jnp.int32, (tm, tm), 1)
    tri = jnp.where(c_i < r_i, 1.0, 0.0).astype(BF16)
    before = jnp.dot(tri, mem.astype(BF16), preferred_element_type=F32) + carry_scr[0:1, :]
    ranks = jnp.zeros((tm, LANES), jnp.int32)
    for kk in range(TOPK_EXPERTS):
        rk = jnp.sum(jnp.where(hots[kk], before, 0.0), axis=-1, keepdims=True)
        ranks = jnp.where(lane == kk, rk.astype(jnp.int32), ranks)
    rank_ref[...] = ranks
    carry = carry_scr[...] + jnp.sum(mem, axis=0, keepdims=True)
    carry_scr[...] = carry
    cnt_ref[...] = carry


def _outproj(att, pc, x2, wa, wp, gpost, gtm, gpre, scf, shf, rwh, rwl, rb, seq):
    T, D = x2.shape
    tm = 512
    per_b = seq // tm
    row = lambda w: pl.BlockSpec((tm, w), lambda i: (i, 0))
    const = lambda shape: pl.BlockSpec(shape, lambda i: tuple(0 for _ in shape), pipeline_mode=pl.Buffered(1))
    perb = pl.BlockSpec((1, 1, D), lambda i: (i // per_b, 0, 0))
    return pl.pallas_call(
        _outproj_kernel,
        out_shape=(jax.ShapeDtypeStruct((T, D), F32), jax.ShapeDtypeStruct((T * ROW_SUBLANES, LANES), jnp.uint32),
                   jax.ShapeDtypeStruct((T, LANES), jnp.int32), jax.ShapeDtypeStruct((T, LANES), F32),
                   jax.ShapeDtypeStruct((T, LANES), jnp.int32), jax.ShapeDtypeStruct((8, LANES), F32)),
        grid=(T // tm,),
        in_specs=[row(att.shape[1]), row(pc.shape[1]), row(D), const(wa.shape), const(wp.shape),
                  const((1, D)), perb, const((1, D)), perb, perb,
                  const(rwh.shape), const(rwl.shape), const((1, LANES))],
        out_specs=(row(D), pl.BlockSpec((tm * ROW_SUBLANES, LANES), lambda i: (i, 0)),
                   row(LANES), row(LANES), row(LANES), pl.BlockSpec((8, LANES), lambda i: (0, 0))),
        scratch_shapes=[pltpu.VMEM((8, LANES), F32)],
        compiler_params=_cparams(("arbitrary",)),
        name="out_proj_router",
    )(att, pc, x2, wa, wp, gpost, gtm, gpre, scf, shf, rwh, rwl, rb)


MOE_F_CHUNK = 256
MOE_D_CHUNK = 512
MOE_AHEAD = 2
MOE_SLOTS = MOE_AHEAD + 1


MOE_W_PHASES = 6


def _moe_kernel(te_ref, dest_ref, nu_ref, par_ref, pos_ref, nxt_ref, pn_ref, lo_ref, hi_ref,
                h_hbm, wg_hbm, bg_ref, wu_hbm, bu_ref, wd_hbm, bd_ref, o_ref,
                xbuf, act_scr, hi_scr, wg_bf, wu_bf, wd_bf, stg_a, stg_b, rt_ref, sem, wsem, *, layer):
    i = pl.program_id(0)
    tm = act_scr.shape[0]
    Fe = act_scr.shape[1]
    D = wd_bf.shape[-1]
    half = D // 2
    n_used = nu_ref[0]
    last = n_used - 1

    def piece(ph, expert):
        if ph < 4:
            src_ref, dst = (wg_hbm, stg_a) if ph < 2 else (wu_hbm, stg_a)
            rows = D // 2
        else:
            src_ref, dst, rows = wd_hbm, stg_b, Fe // 2
        r0 = (ph % 2) * rows
        return pltpu.make_async_copy(src_ref.at[layer, expert, pl.ds(r0, rows), :], dst, wsem)

    def round_piece(ph, slot):
        if ph < 4:
            rows = D // 2
            dst = wg_bf if ph < 2 else wu_bf
            dst[slot, (ph % 2) * rows:(ph % 2 + 1) * rows, :] = stg_a[...].astype(BF16)
        else:
            rows = Fe // 2
            wd_bf[slot, (ph % 2) * rows:(ph % 2 + 1) * rows, :] = stg_b[...].astype(BF16)

    def row_copy(tile_id, r, slot):
        tok = rt_ref[tile_id * tm + r]
        src = h_hbm.at[pl.ds(pl.multiple_of(tok * ROW_SUBLANES, ROW_SUBLANES), ROW_SUBLANES)]
        return pltpu.make_async_copy(src, xbuf.at[slot, pl.ds(r * ROW_SUBLANES, ROW_SUBLANES)], sem.at[slot])

    def tile_wait(slot):
        pltpu.make_async_copy(h_hbm.at[pl.ds(0, tm * ROW_SUBLANES)], xbuf.at[slot], sem.at[slot]).wait()

    @pl.when(i == 0)
    def _():
        _invert_rows(dest_ref, lo_ref, hi_ref, rt_ref)
        for t in range(MOE_AHEAD):
            def issue(r, carry, t=t):
                row_copy(jnp.minimum(t, last), r, t).start()
                return carry
            lax.fori_loop(0, tm, issue, 0)

    @pl.when(i < n_used)
    def _():
        slot = i % MOE_SLOTS
        nslot = (i + MOE_AHEAD) % MOE_SLOTS
        ntile = jnp.minimum(i + MOE_AHEAD, last)
        n_f, n_d = Fe // MOE_F_CHUNK, D // MOE_D_CHUNK
        per = tm // (n_f + n_d)

        def issue_part(p):
            for r in range(p * per, (p + 1) * per):
                row_copy(ntile, r, nslot).start()

        expert = te_ref[i]
        wslot = par_ref[i]
        k = pos_ref[i]
        nxt = nxt_ref[i]

        @pl.when(i == 0)
        def _():
            for ph in range(MOE_W_PHASES):
                piece(ph, expert).start()
                piece(ph, expert).wait()
                round_piece(ph, wslot)

        @pl.when((k == 0) & (i > 0))
        def _():
            done = jnp.minimum(pn_ref[i] - 1, MOE_W_PHASES)
            for ph in range(MOE_W_PHASES):
                @pl.when(ph >= done)
                def _(ph=ph):
                    @pl.when(ph > done)
                    def _():
                        piece(ph, expert).start()
                    piece(ph, expert).wait()
                    round_piece(ph, wslot)

        for ph in range(MOE_W_PHASES):
            @pl.when((k == ph + 1) & (nxt >= 0))
            def _(ph=ph):
                piece(ph, nxt).wait()
                round_piece(ph, 1 - wslot)

        for ph in range(MOE_W_PHASES):
            @pl.when((k == ph) & (nxt >= 0))
            def _(ph=ph):
                piece(ph, nxt).start()

        tile_wait(slot)
        words = [xbuf[slot, pl.ds(s, tm, stride=ROW_SUBLANES), :] for s in range(ROW_SUBLANES)]
        x = jnp.concatenate([_unpack_hi(w).astype(BF16) for w in words]
                            + [_unpack_lo(w).astype(BF16) for w in words], axis=1)
        for c in range(n_f):
            issue_part(c)
            cs = slice(c * MOE_F_CHUNK, (c + 1) * MOE_F_CHUNK)
            g = jnp.dot(x, wg_bf[wslot, :, cs], preferred_element_type=F32) + bg_ref[0, expert, :, cs]
            u = jnp.dot(x, wu_bf[wslot, :, cs], preferred_element_type=F32) + bu_ref[0, expert, :, cs]
            g = jnp.minimum(g, SWIGLU_LIMIT)
            u = jnp.clip(u, -SWIGLU_LIMIT, SWIGLU_LIMIT)
            act_scr[:, cs] = (g * jax.nn.sigmoid(SWIGLU_ALPHA * g) * (u + 1.0)).astype(BF16)
        act = act_scr[...]
        for c in range(n_d):
            issue_part(n_f + c)
            c0 = c * MOE_D_CHUNK
            y = (jnp.dot(act, wd_bf[wslot, :, c0:c0 + MOE_D_CHUNK], preferred_element_type=F32)
                 + bd_ref[0, expert, :, c0:c0 + MOE_D_CHUNK])
            for j in range(MOE_D_CHUNK // LANES):
                col = c0 + j * LANES
                yj = y[:, j * LANES:(j + 1) * LANES]
                if col < half:
                    hi_scr[:, col:col + LANES] = _bf16_bits(yj)
                else:
                    s = (col - half) // LANES
                    o_ref[pl.ds(s, tm, stride=ROW_SUBLANES), :] = hi_scr[:, col - half:col - half + LANES] | (_bf16_bits(yj) >> 16)

        @pl.when(i == last)
        def _():
            for t in range(1, MOE_AHEAD + 1):
                tile_wait((i + t) % MOE_SLOTS)

    @pl.when(i >= n_used)
    def _():
        o_ref[...] = jnp.zeros_like(o_ref)


def _group_tables(tile_expert, n_used):
    n = tile_expert.shape[0]
    idx = jnp.arange(n, dtype=jnp.int32)
    first = jnp.concatenate([jnp.ones((1,), bool), tile_expert[1:] != tile_expert[:-1]])
    par = (jnp.cumsum(first.astype(jnp.int32)) - 1) % 2
    pos = idx - lax.cummax(jnp.where(first, idx, 0))
    later = (idx[None, :] > idx[:, None]) & (idx[None, :] < n_used[0]) & (tile_expert[None, :] != tile_expert[:, None])
    nxt_idx = jnp.min(jnp.where(later, idx[None, :], n), axis=1)
    nxt = jnp.where(nxt_idx < n, tile_expert[jnp.minimum(nxt_idx, n - 1)], -1)
    pn = jnp.concatenate([jnp.zeros((1,), jnp.int32), pos[:-1] + 1])
    return par.astype(jnp.int32), pos.astype(jnp.int32), nxt.astype(jnp.int32), pn.astype(jnp.int32)


def _moe(layer, tile_expert, dest, pad_rows, n_used, h2p, wg, bg, wu, bu, wd, bd, tm):
    L, E, D, Fe = wg.shape
    n_tiles = tile_expert.shape[0]
    lo, hi = pad_rows
    assert tm % (Fe // MOE_F_CHUNK + D // MOE_D_CHUNK) == 0 and (D // 2) % MOE_D_CHUNK == 0
    par, pos, nxt, pn = _group_tables(tile_expert, n_used)
    bspec = lambda w: pl.BlockSpec((1, E, 1, w), lambda i, *_: (layer, 0, 0, 0), pipeline_mode=pl.Buffered(1))
    hbm = pl.BlockSpec(memory_space=pl.ANY)
    return pl.pallas_call(
        functools.partial(_moe_kernel, layer=layer),
        out_shape=jax.ShapeDtypeStruct((n_tiles * tm * ROW_SUBLANES, LANES), jnp.uint32),
        grid_spec=pltpu.PrefetchScalarGridSpec(
            num_scalar_prefetch=9,
            grid=(n_tiles,),
            in_specs=[hbm, hbm, bspec(Fe), hbm, bspec(Fe), hbm, bspec(D)],
            out_specs=pl.BlockSpec((tm * ROW_SUBLANES, LANES), lambda i, *_: (i, 0)),
            scratch_shapes=[pltpu.VMEM((MOE_SLOTS, tm * ROW_SUBLANES, LANES), jnp.uint32),
                            pltpu.VMEM((tm, Fe), BF16), pltpu.VMEM((tm, D // 2), jnp.uint32),
                            pltpu.VMEM((2, D, Fe), BF16), pltpu.VMEM((2, D, Fe), BF16),
                            pltpu.VMEM((2, Fe, D), BF16),
                            pltpu.VMEM((D // 2, Fe), F32), pltpu.VMEM((Fe // 2, D), F32),
                            pltpu.SMEM((n_tiles * tm,), jnp.int32),
                            pltpu.SemaphoreType.DMA((MOE_SLOTS,)), pltpu.SemaphoreType.DMA],
        ),
        compiler_params=_cparams(("arbitrary",)),
        name="moe_experts",
    )(tile_expert, dest, n_used, par, pos, nxt, pn, lo, hi, h2p, wg, bg.reshape(L, E, 1, Fe),
      wu, bu.reshape(L, E, 1, Fe), wd, bd.reshape(L, E, 1, D))


COMBINE_AHEAD = 2
COMBINE_SLOTS = COMBINE_AHEAD + 1
COMBINE_ROW_CHUNK = 8


def _combine_kernel(dest_ref, yp_hbm, gate_ref, x_ref, gpost_ref, gtf_ref, o_ref, buf, sem):
    i = pl.program_id(0)
    n = pl.num_programs(0)
    tc, D = x_ref.shape
    half = D // 2
    K = TOPK_EXPERTS

    def row_copies(tile_id, r, slot):
        base = tile_id * tc * K
        for kk in range(K):
            d = dest_ref[base + r * K + kk]
            src = yp_hbm.at[pl.ds(pl.multiple_of(d * ROW_SUBLANES, ROW_SUBLANES), ROW_SUBLANES)]
            dst = buf.at[slot, kk, pl.ds(r * ROW_SUBLANES, ROW_SUBLANES)]
            pltpu.make_async_copy(src, dst, sem.at[slot]).start(priority=kk % 2)

    def tile_wait(slot):
        for kk in range(K):
            pltpu.make_async_copy(yp_hbm.at[pl.ds(0, tc * ROW_SUBLANES)], buf.at[slot, kk], sem.at[slot]).wait()

    @pl.when(i == 0)
    def _():
        for t in range(COMBINE_AHEAD):
            def issue(r, carry, t=t):
                row_copies(jnp.minimum(t, n - 1), r, t)
                return carry
            lax.fori_loop(0, tc, issue, 0)

    slot = i % COMBINE_SLOTS
    nslot = (i + COMBINE_AHEAD) % COMBINE_SLOTS
    ntile = jnp.minimum(i + COMBINE_AHEAD, n - 1)
    tile_wait(slot)
    RC = COMBINE_ROW_CHUNK
    for c in range(tc // RC):
        for r in range(c * RC, (c + 1) * RC):
            row_copies(ntile, r, nslot)
        rows = slice(c * RC, (c + 1) * RC)
        gates = gate_ref[rows, :]
        gk = [jnp.broadcast_to(gates[:, kk:kk + 1], (RC, LANES)) for kk in range(K)]
        his, los = [], []
        sq = jnp.zeros((RC, LANES), F32)
        for s in range(ROW_SUBLANES):
            hi = lo = None
            for kk in range(K):
                w = buf[slot, kk, pl.ds(c * RC * ROW_SUBLANES + s, RC, stride=ROW_SUBLANES), :]
                h, l = gk[kk] * _unpack_hi(w), gk[kk] * _unpack_lo(w)
                hi, lo = (h, l) if kk == 0 else (hi + h, lo + l)
            his.append(hi)
            los.append(lo)
            sq = sq + hi * hi + lo * lo
        inv = lax.rsqrt(jnp.sum(sq, axis=-1, keepdims=True) * (1.0 / D) + RMS_EPS)
        for s in range(ROW_SUBLANES):
            for col, y in ((s * LANES, his[s]), (half + s * LANES, los[s])):
                cs = slice(col, col + LANES)
                o_ref[rows, cs] = x_ref[rows, cs] + gtf_ref[0, :, cs] * ((y * inv) * gpost_ref[:, cs])

    @pl.when(i == n - 1)
    def _():
        for t in range(1, COMBINE_AHEAD + 1):
            tile_wait((i + t) % COMBINE_SLOTS)


def _combine(dest, yp, gates, x1, gpost, gtf, seq):
    T, D = x1.shape
    tc = 256
    per_b = seq // tc
    return pl.pallas_call(
        _combine_kernel,
        out_shape=jax.ShapeDtypeStruct((T, D), F32),
        grid_spec=pltpu.PrefetchScalarGridSpec(
            num_scalar_prefetch=1,
            grid=(T // tc,),
            in_specs=[pl.BlockSpec(memory_space=pl.ANY),
                      pl.BlockSpec((tc, LANES), lambda i, d: (i, 0)),
                      pl.BlockSpec((tc, D), lambda i, d: (i, 0)),
                      pl.BlockSpec((1, D), lambda i, d: (0, 0)),
                      pl.BlockSpec((1, 1, D), lambda i, d: (i // per_b, 0, 0))],
            out_specs=pl.BlockSpec((tc, D), lambda i, d: (i, 0)),
            scratch_shapes=[pltpu.VMEM((COMBINE_SLOTS, TOPK_EXPERTS, tc * ROW_SUBLANES, LANES), jnp.uint32),
                            pltpu.SemaphoreType.DMA((COMBINE_SLOTS,))],
        ),
        compiler_params=_cparams(("arbitrary",)),
        name="moe_combine",
    )(dest, yp, gates, x1, gpost, gtf)


def _dest_kernel(idx_ref, rank_ref, cnt_ref, dest_ref, *, tm_rows):
    tm = idx_ref.shape[0]
    tiles = jnp.floor((cnt_ref[0:1, :] + float(tm_rows - 1)) * (1.0 / tm_rows))
    r_i = lax.broadcasted_iota(jnp.int32, (LANES, LANES), 0)
    c_i = lax.broadcasted_iota(jnp.int32, (LANES, LANES), 1)
    before = jnp.where(r_i < c_i, 1.0, 0.0).astype(BF16)
    start = jnp.dot(jnp.broadcast_to(tiles, (8, LANES)).astype(BF16), before,
                    preferred_element_type=F32)[0:1] * float(tm_rows)
    lane = lax.broadcasted_iota(jnp.int32, (tm, LANES), 1)
    idx = idx_ref[...]
    rank = rank_ref[...]
    out = jnp.zeros((tm, LANES), jnp.int32)
    for kk in range(TOPK_EXPERTS):
        base = jnp.sum(jnp.where(lane == idx[:, kk:kk + 1], start, 0.0), axis=-1, keepdims=True)
        out = jnp.where(lane == kk, base.astype(jnp.int32) + rank[:, kk:kk + 1], out)
    dest_ref[...] = out


def _dest(idx, rank, cnt, tm_rows):
    T = idx.shape[0]
    tm = 2048
    assert tm_rows & (tm_rows - 1) == 0
    row = pl.BlockSpec((tm, LANES), lambda i: (i, 0))
    return pl.pallas_call(
        functools.partial(_dest_kernel, tm_rows=tm_rows),
        out_shape=jax.ShapeDtypeStruct((T, LANES), jnp.int32),
        grid=(T // tm,),
        in_specs=[row, row, pl.BlockSpec((8, LANES), lambda i: (0, 0))],
        out_specs=row,
        compiler_params=_cparams(("parallel",)),
        name="route_dest",
    )(idx, rank, cnt)


INVERT_UNROLL = 8


def _invert_rows(dest_ref, lo_ref, hi_ref, out_ref):
    def fill(j, carry):
        out_ref[j] = 0
        return carry

    for e in range(lo_ref.shape[0]):
        lax.fori_loop(lo_ref[e], hi_ref[e], fill, 0)

    per = INVERT_UNROLL // TOPK_EXPERTS

    def body(b, carry):
        for u in range(INVERT_UNROLL):
            out_ref[dest_ref[b * INVERT_UNROLL + u]] = b * per + u // TOPK_EXPERTS
        return carry

    lax.fori_loop(0, dest_ref.shape[0] // INVERT_UNROLL, body, 0)


def _routing_tables(idx, rank, cnt, n_experts, tm):
    T = idx.shape[0]
    n_tiles = (T * TOPK_EXPERTS) // tm + n_experts
    counts = cnt[0, :n_experts].astype(jnp.int32)
    padded = ((counts + tm - 1) // tm) * tm
    pad_end = jnp.cumsum(padded)
    tile_start = jnp.arange(n_tiles, dtype=jnp.int32) * tm
    tile_expert = jnp.minimum(jnp.sum((pad_end[None, :] <= tile_start[:, None]).astype(jnp.int32), axis=1),
                              n_experts - 1)
    n_used = pad_end[-1:] // tm
    pad_rows = (pad_end - padded + counts, pad_end)
    dest = _dest(idx, rank, cnt, tm)[:, :TOPK_EXPERTS].reshape(-1)
    return dest, pad_rows, tile_expert, n_used


def kernel(x, c, w_ada, b_ada, g_pre_mix, g_post_mix, g_pre_ffn, g_post_ffn, w_in, w_out, pool_w, pool_scale,
           conv_w, rel_bias, router_w, router_b, w_gate, b_gate, w_up, b_up, w_down, b_down):
    B, S, D = x.shape
    L = w_ada.shape[0]
    T = B * S
    E = router_w.shape[-1]
    attn_w = D // 2
    heads = attn_w // HEAD_DIM
    moe_tm = 256
    assert D == 2 * ROW_SUBLANES * LANES

    mod = _ada(c, w_ada, b_ada)
    x2 = x.reshape(T, D)
    for l in range(L):
        sh_m, sc_m, gt_m, sh_f, sc_f, gt_f = [mod[l, :, i * D:(i + 1) * D].reshape(B, 1, D) for i in range(N_MOD)]
        row = lambda v: v.reshape(1, D)

        qkv, rest = _inproj(x2, row(g_pre_mix[l]), sc_m, sh_m, w_in[l].astype(BF16), S, attn_w)
        att = _attention(qkv, rel_bias, B, S, heads)
        pc = _poolconv(rest, pool_w[l], pool_scale[l], conv_w[l], B, S)

        wo = w_out[l].astype(BF16)
        rw = jnp.zeros((D, LANES), F32).at[:, :E].set(router_w[l])
        rwh = rw.astype(BF16)
        rwl = (rw - rwh.astype(F32)).astype(BF16)
        rb = jnp.full((1, LANES), NEG_INF, F32).at[0, :E].set(router_b[l])
        x1, h2, idx, gates, rank, cnt = _outproj(
            att, pc, x2, wo[:attn_w], wo[attn_w:], row(g_post_mix[l]), gt_m, row(g_pre_ffn[l]), sc_f, sh_f,
            rwh, rwl, rb, S)

        dest, pad_rows, tile_expert, n_used = _routing_tables(idx, rank, cnt, E, moe_tm)
        yp = _moe(l, tile_expert, dest, pad_rows, n_used, h2, w_gate, b_gate, w_up, b_up, w_down, b_down, moe_tm)
        x2 = _combine(dest, yp, gates, x1, row(g_post_ffn[l]), gt_f, S)
    return x2.reshape(B, S, D)
```

```python
import functools
import math

import numpy as np
import jax
import jax.numpy as jnp
from jax import lax
from jax.experimental import pallas as pl
from jax.experimental.pallas import tpu as pltpu

F32 = jnp.float32
BF16 = jnp.bfloat16

LANES = 128
HEAD_DIM = 128
MOBA_BLOCK = 256
MOBA_TOPK = 3
REL_BUCKETS = 32
REL_MAX_DISTANCE = 128
POOL_WINDOWS = (2, 4, 8, 16)
GROUP_DIM = 128
CONV_K = 3
TOPK_EXPERTS = 4
SWIGLU_LIMIT = 7.0
SWIGLU_ALPHA = 1.702
N_MOD = 6
RMS_EPS = 1e-6
NEG_INF = -1e30
ATTN_SCALE = HEAD_DIM ** -0.5
LOG2E = math.log2(math.e)

VMEM_LIMIT = 56 * 1024 * 1024


def _cparams(sem):
    return pltpu.CompilerParams(dimension_semantics=sem, vmem_limit_bytes=VMEM_LIMIT)


def _rms(x):
    return x * lax.rsqrt(jnp.mean(x * x, axis=-1, keepdims=True) + RMS_EPS)


ROW_SUBLANES = 8


def _bf16_bits(v):
    return lax.bitcast_convert_type(v.astype(BF16).astype(F32), jnp.uint32)


def _pack_words(hi, lo):
    return _bf16_bits(hi) | (_bf16_bits(lo) >> 16)


def _unpack_hi(w):
    return lax.bitcast_convert_type(w & jnp.uint32(0xFFFF0000), F32)


def _unpack_lo(w):
    return lax.bitcast_convert_type(w << 16, F32)


def _ada_kernel(c_ref, w_ref, b_ref, o_ref):
    c = c_ref[...]
    ca = (c * jax.nn.sigmoid(c)).astype(BF16)
    o_ref[0] = jnp.dot(ca, w_ref[0].astype(BF16), preferred_element_type=F32) + b_ref[0]


def _ada(c, w_ada, b_ada):
    L, D, N = w_ada.shape
    B = c.shape[0]
    tn = 1024
    return pl.pallas_call(
        _ada_kernel,
        out_shape=jax.ShapeDtypeStruct((L, B, N), F32),
        grid=(L, N // tn),
        in_specs=[
            pl.BlockSpec((B, D), lambda l, j: (0, 0)),
            pl.BlockSpec((1, D, tn), lambda l, j: (l, 0, j)),
            pl.BlockSpec((1, 1, tn), lambda l, j: (l, 0, j)),
        ],
        out_specs=pl.BlockSpec((1, B, tn), lambda l, j: (l, 0, j)),
        compiler_params=_cparams(("parallel", "parallel")),
        name="ada_mod",
    )(c, w_ada, b_ada.reshape(L, 1, N))


INPROJ_TM = 1024
INPROJ_TN = 1024


def _inproj_kernel(x_ref, g_ref, sc_ref, sh_ref, w_ref, qkv_ref, rest_ref, h_scr, *, n_qkv):
    j = pl.program_id(1)

    @pl.when(j == 0)
    def _():
        h = _rms(x_ref[...]) * g_ref[...]
        h = h * (1.0 + sc_ref[0]) + sh_ref[0]
        h_scr[...] = h.astype(BF16)

    acc = jnp.dot(h_scr[...], w_ref[0], preferred_element_type=F32)

    @pl.when(j < n_qkv)
    def _():
        qkv_ref[...] = acc.astype(BF16)

    @pl.when(j >= n_qkv)
    def _():
        rest_ref[...] = acc


def _inproj(layer, x2, g, sc, sh, w_bf, seq, attn_w):
    T, D = x2.shape
    N = w_bf.shape[2]
    tm, tn = INPROJ_TM, INPROJ_TN
    n_qkv = 3 * attn_w // tn
    n_rest = (N - 3 * attn_w) // tn
    per_b = seq // tm
    return pl.pallas_call(
        functools.partial(_inproj_kernel, n_qkv=n_qkv),
        out_shape=(jax.ShapeDtypeStruct((T, 3 * attn_w), BF16),
                   jax.ShapeDtypeStruct((T, N - 3 * attn_w), F32)),
        grid=(T // tm, n_qkv + n_rest),
        in_specs=[
            pl.BlockSpec((tm, D), lambda i, j: (i, 0)),
            pl.BlockSpec((1, D), lambda i, j: (0, 0)),
            pl.BlockSpec((1, 1, D), lambda i, j: (i // per_b, 0, 0)),
            pl.BlockSpec((1, 1, D), lambda i, j: (i // per_b, 0, 0)),
            pl.BlockSpec((1, D, tn), lambda i, j: (layer, 0, j)),
        ],
        out_specs=(
            pl.BlockSpec((tm, tn), lambda i, j: (i, jnp.minimum(j, n_qkv - 1))),
            pl.BlockSpec((tm, tn), lambda i, j: (i, jnp.maximum(j - n_qkv, 0))),
        ),
        scratch_shapes=[pltpu.VMEM((tm, D), BF16)],
        compiler_params=_cparams(("parallel", "arbitrary")),
        name="mixer_in_proj",
    )(x2, g, sc, sh, w_bf)


def _rel_bucket_np(n):
    n = np.maximum(n, 0)
    max_exact = REL_BUCKETS // 2
    nf = np.maximum(n, max_exact).astype(np.float32)
    large = max_exact + (np.log(nf / np.float32(max_exact)) / np.float32(math.log(REL_MAX_DISTANCE / max_exact))
                         * np.float32(REL_BUCKETS - max_exact)).astype(np.int32)
    large = np.minimum(large, REL_BUCKETS - 1)
    return np.where(n < max_exact, n, large).astype(np.int32)


def _bucket_tables(seq):
    qi = np.arange(MOBA_BLOCK)[:, None]
    ki = np.arange(MOBA_BLOCK)[None, :]
    own = np.where(ki <= qi, _rel_bucket_np(qi - ki), -1)
    prev = _rel_bucket_np(qi - ki + MOBA_BLOCK)
    far = _rel_bucket_np(np.arange(MOBA_BLOCK + 1, max(seq, MOBA_BLOCK + 2)))
    assert np.all(far == far[0])
    return np.stack([own, prev]).astype(np.int32), int(far[0])


ATTN_HEADS_PER_STEP = 2


def _attn_kernel(tab_ref, q_ref, k_ref, v_ref, bkt_ref, o_ref, bias_scr, *, nblk, far_bucket):
    hg = pl.program_id(0)
    b = pl.program_id(1)
    L = MOBA_BLOCK
    G = ATTN_HEADS_PER_STEP

    @pl.when(b == 0)
    def _():
        for g in range(G):
            for m in range(2):
                bk = bkt_ref[m]
                acc = jnp.full((L, L), NEG_INF, F32)
                for r in range(REL_BUCKETS):
                    acc = jnp.where(bk == r, tab_ref[hg * G + g, r] * LOG2E, acc)
                bias_scr[g, m] = acc

    dn = (((1,), (1,)), ((), ()))
    lane = lax.broadcasted_iota(jnp.int32, (L, LANES), 1)
    heads = []
    for g in range(G):
        cols = slice(g * HEAD_DIM, (g + 1) * HEAD_DIM)
        q, k, v = q_ref[:, cols], k_ref[:, cols], v_ref[:, cols]
        kmean = jnp.mean(k.astype(F32).reshape(nblk, L, HEAD_DIM), axis=1)
        kmean = jnp.concatenate([kmean, jnp.zeros((LANES - nblk, HEAD_DIM), F32)], axis=0).astype(BF16)
        gate = lax.dot_general(q, kmean, dn, preferred_element_type=F32)
        heads.append((q, k, v, gate, tab_ref[hg * G + g, far_bucket] * LOG2E))

    for a, g in [(a, g) for a in range(nblk) for g in range(G)]:
        q, k, v, gate, far_bias = heads[g]
        cols = slice(g * HEAD_DIM, (g + 1) * HEAD_DIM)
        qa = q[a * L:(a + 1) * L]
        n = (a + 1) * L
        s = lax.dot_general(qa, k[:n], dn, preferred_element_type=F32) * (ATTN_SCALE * LOG2E)
        hide = None
        if a > MOBA_TOPK:
            ga = jnp.where(lane < a, gate[a * L:(a + 1) * L], NEG_INF)
            cnt = jnp.zeros((L, LANES), F32)
            for i in range(a):
                gi = ga[:, i:i + 1]
                ge = jnp.where(gi >= ga, 1.0, 0.0)
                gt = jnp.where(gi > ga, 1.0, 0.0)
                cnt = cnt + jnp.where(lane > i, ge, gt)
            hide = jnp.where((cnt < float(MOBA_TOPK)) & (lane < a), 0.0, NEG_INF)
            hide_far = hide + far_bias
        pieces = []
        for j in range(a + 1):
            sj = s[:, j * L:(j + 1) * L]
            if j == a:
                sj = sj + bias_scr[g, 0]
            elif j == a - 1:
                sj = sj + bias_scr[g, 1]
                if hide is not None:
                    sj = sj + hide[:, j:j + 1]
            else:
                sj = sj + (far_bias if hide is None else hide_far[:, j:j + 1])
            pieces.append(sj)
        s = pieces[0] if a == 0 else jnp.concatenate(pieces, axis=1)
        m = jnp.max(s, axis=-1, keepdims=True)
        p = jnp.exp2(s - m)
        l = jnp.sum(p, axis=-1, keepdims=True)
        o = jnp.dot(p.astype(BF16), v[:n], preferred_element_type=F32) / l
        o_ref[a * L:(a + 1) * L, cols] = o.astype(o_ref.dtype)


def _attention(qkv, rel_bias, batch, seq, heads):
    T = qkv.shape[0]
    nblk = seq // MOBA_BLOCK
    G = ATTN_HEADS_PER_STEP
    assert heads % G == 0
    ng = heads // G
    bkt, far_bucket = _bucket_tables(seq)
    table = rel_bias.T.astype(F32)
    return pl.pallas_call(
        functools.partial(_attn_kernel, nblk=nblk, far_bucket=far_bucket),
        out_shape=jax.ShapeDtypeStruct((T, heads * HEAD_DIM), BF16),
        grid=(ng, batch),
        in_specs=[
            pl.BlockSpec(memory_space=pltpu.SMEM),
            pl.BlockSpec((seq, G * HEAD_DIM), lambda h, b: (b, h)),
            pl.BlockSpec((seq, G * HEAD_DIM), lambda h, b: (b, ng + h)),
            pl.BlockSpec((seq, G * HEAD_DIM), lambda h, b: (b, 2 * ng + h)),
            pl.BlockSpec((2, MOBA_BLOCK, MOBA_BLOCK), lambda h, b: (0, 0, 0)),
        ],
        out_specs=pl.BlockSpec((seq, G * HEAD_DIM), lambda h, b: (b, h)),
        scratch_shapes=[pltpu.VMEM((G, 2, MOBA_BLOCK, MOBA_BLOCK), F32)],
        compiler_params=_cparams(("arbitrary", "arbitrary")),
        name="moba_attention",
    )(table, qkv, qkv, qkv, jnp.asarray(bkt))


def _poolconv_kernel(u_ref, gb_ref, gc_ref, hc_ref, pw_ref, ps_ref, cw_ref, o_ref):
    S = u_ref.shape[0]
    G = len(POOL_WINDOWS)
    row = lax.broadcasted_iota(jnp.int32, (S, GROUP_DIM), 0)

    def shift(x, k):
        return jnp.where(row >= k, pltpu.roll(x, k, 0), 0.0)

    for gi, w in enumerate(POOL_WINDOWS):
        cs = slice(gi * GROUP_DIM, (gi + 1) * GROUP_DIM)
        ug = u_ref[:, cs]
        win = ug
        k = 1
        while k < w:
            win = win + shift(win, k)
            k *= 2
        count = jnp.minimum(row + 1, w).astype(F32)
        d = win / count - ug
        y = jnp.dot(d.astype(BF16), pw_ref[gi].astype(BF16), preferred_element_type=F32)
        o_ref[:, cs] = (y * ps_ref[:, cs]).astype(o_ref.dtype)

        uu = gc_ref[:, cs] * hc_ref[:, cs]
        conv = (cw_ref[0:1, cs] * shift(uu, 2) + cw_ref[1:2, cs] * shift(uu, 1)
                + cw_ref[2:3, cs] * uu)
        oc = slice((G + gi) * GROUP_DIM, (G + gi + 1) * GROUP_DIM)
        o_ref[:, oc] = (gb_ref[:, cs] * conv).astype(o_ref.dtype)


def _poolconv(rest, pool_w, pool_scale, conv_w, batch, seq):
    T = rest.shape[0]
    W = len(POOL_WINDOWS) * GROUP_DIM
    spec = lambda c: pl.BlockSpec((seq, W), lambda b: (b, c))
    return pl.pallas_call(
        _poolconv_kernel,
        out_shape=jax.ShapeDtypeStruct((T, 2 * W), BF16),
        grid=(batch,),
        in_specs=[
            spec(0), spec(1), spec(2), spec(3),
            pl.BlockSpec(pool_w.shape, lambda b: (0, 0, 0)),
            pl.BlockSpec((1, W), lambda b: (0, 0)),
            pl.BlockSpec((CONV_K, W), lambda b: (0, 0)),
        ],
        out_specs=pl.BlockSpec((seq, 2 * W), lambda b: (b, 0)),
        compiler_params=_cparams(("parallel",)),
        name="pool_conv_mixers",
    )(rest, rest, rest, rest, pool_w, pool_scale.reshape(1, W), conv_w)


def _outproj_kernel(att_ref, pc_ref, x_ref, wa_ref, wp_ref, gpost_ref, gtm_ref, gpre_ref, scf_ref, shf_ref,
                    rwh_ref, rwl_ref, rb_ref,
                    x1_ref, h2_ref, idx_ref, gate_ref, rank_ref, cnt_ref, carry_scr):
    i = pl.program_id(0)
    tm = x_ref.shape[0]

    @pl.when(i == 0)
    def _():
        carry_scr[...] = jnp.zeros_like(carry_scr)

    mix = (jnp.dot(att_ref[...], wa_ref[...], preferred_element_type=F32)
           + jnp.dot(pc_ref[...], wp_ref[...], preferred_element_type=F32))
    x1 = x_ref[...] + gtm_ref[0] * (_rms(mix) * gpost_ref[...])
    x1_ref[...] = x1
    h2 = (_rms(x1) * gpre_ref[...]) * (1.0 + scf_ref[0]) + shf_ref[0]
    half = h2.shape[1] // 2
    for s in range(ROW_SUBLANES):
        h2_ref[pl.ds(s, tm, stride=ROW_SUBLANES), :] = _pack_words(h2[:, s * LANES:(s + 1) * LANES],
                                      h2[:, half + s * LANES:half + (s + 1) * LANES])

    hi = h2.astype(BF16)
    lo = (h2 - hi.astype(F32)).astype(BF16)
    r2 = jnp.dot(hi, jnp.concatenate([rwh_ref[...], rwl_ref[...]], axis=1), preferred_element_type=F32)
    logits = (r2[:, :LANES] + r2[:, LANES:]
              + jnp.dot(lo, rwh_ref[...], preferred_element_type=F32)) + rb_ref[...]

    lane = lax.broadcasted_iota(jnp.int32, (tm, LANES), 1)
    work = logits
    mem = jnp.zeros((tm, LANES), F32)
    vals, hots = [], []
    idx_out = jnp.zeros((tm, LANES), jnp.int32)
    for kk in range(TOPK_EXPERTS):
        m = jnp.max(work, axis=-1, keepdims=True)
        ik = jnp.min(jnp.where(work == m, lane, LANES), axis=-1, keepdims=True)
        hot = lane == ik
        vals.append(m)
        hots.append(hot)
        idx_out = jnp.where(lane == kk, ik, idx_out)
        mem = jnp.where(hot, 1.0, mem)
        work = jnp.where(hot, -jnp.inf, work)
    idx_ref[...] = idx_out

    es = [jnp.exp(vk - vals[0]) for vk in vals]
    denom = es[0]
    for e in es[1:]:
        denom = denom + e
    gates = jnp.zeros((tm, LANES), F32)
    for kk in range(TOPK_EXPERTS):
        gates = jnp.where(lane == kk, es[kk] / denom, gates)
    gate_ref[...] = gates

    r_i = lax.broadcasted_iota(jnp.int32, (tm, tm), 0)
    c_i = lax.broadcasted_iota(jnp.int32, (tm, tm), 1)
    tri = jnp.where(c_i < r_i, 1.0, 0.0).astype(BF16)
    before = jnp.dot(tri, mem.astype(BF16), preferred_element_type=F32) + carry_scr[0:1, :]
    ranks = jnp.zeros((tm, LANES), jnp.int32)
    for kk in range(TOPK_EXPERTS):
        rk = jnp.sum(jnp.where(hots[kk], before, 0.0), axis=-1, keepdims=True)
        ranks = jnp.where(lane == kk, rk.astype(jnp.int32), ranks)
    rank_ref[...] = ranks
    carry = carry_scr[...] + jnp.sum(mem, axis=0, keepdims=True)
    carry_scr[...] = carry
    cnt_ref[...] = carry


def _outproj(att, pc, x2, wa, wp, gpost, gtm, gpre, scf, shf, rwh, rwl, rb, seq):
    T, D = x2.shape
    tm = 512
    per_b = seq // tm
    row = lambda w: pl.BlockSpec((tm, w), lambda i: (i, 0))
    const = lambda shape: pl.BlockSpec(shape, lambda i: tuple(0 for _ in shape), pipeline_mode=pl.Buffered(1))
    perb = pl.BlockSpec((1, 1, D), lambda i: (i // per_b, 0, 0))
    return pl.pallas_call(
        _outproj_kernel,
        out_shape=(jax.ShapeDtypeStruct((T, D), F32), jax.ShapeDtypeStruct((T * ROW_SUBLANES, LANES), jnp.uint32),
                   jax.ShapeDtypeStruct((T, LANES), jnp.int32), jax.ShapeDtypeStruct((T, LANES), F32),
                   jax.ShapeDtypeStruct((T, LANES), jnp.int32), jax.ShapeDtypeStruct((8, LANES), F32)),
        grid=(T // tm,),
        in_specs=[row(att.shape[1]), row(pc.shape[1]), row(D), const(wa.shape), const(wp.shape),
                  const((1, D)), perb, const((1, D)), perb, perb,
                  const(rwh.shape), const(rwl.shape), const((1, LANES))],
        out_specs=(row(D), pl.BlockSpec((tm * ROW_SUBLANES, LANES), lambda i: (i, 0)),
                   row(LANES), row(LANES), row(LANES), pl.BlockSpec((8, LANES), lambda i: (0, 0))),
        scratch_shapes=[pltpu.VMEM((8, LANES), F32)],
        compiler_params=_cparams(("arbitrary",)),
        name="out_proj_router",
    )(att, pc, x2, wa, wp, gpost, gtm, gpre, scf, shf, rwh, rwl, rb)


MOE_F_CHUNK = 256
MOE_D_CHUNK = 512
MOE_AHEAD = 2
MOE_SLOTS = MOE_AHEAD + 1


MOE_W_PHASES = 6


def _moe_kernel(te_ref, dest_ref, nu_ref, par_ref, pos_ref, nxt_ref, pn_ref, lo_ref, hi_ref,
                h_hbm, wg_hbm, bg_ref, wu_hbm, bu_ref, wd_hbm, bd_ref, o_ref,
                xbuf, act_scr, hi_scr, wg_bf, wu_bf, wd_bf, stg_a, stg_b, rt_ref, sem, wsem, *, layer):
    i = pl.program_id(0)
    tm = act_scr.shape[0]
    Fe = act_scr.shape[1]
    D = wd_bf.shape[-1]
    half = D // 2
    n_used = nu_ref[0]
    last = n_used - 1

    def piece(ph, expert):
        if ph < 4:
            src_ref, dst = (wg_hbm, stg_a) if ph < 2 else (wu_hbm, stg_a)
            rows = D // 2
        else:
            src_ref, dst, rows = wd_hbm, stg_b, Fe // 2
        r0 = (ph % 2) * rows
        return pltpu.make_async_copy(src_ref.at[layer, expert, pl.ds(r0, rows), :], dst, wsem)

    def round_piece(ph, slot):
        if ph < 4:
            rows = D // 2
            dst = wg_bf if ph < 2 else wu_bf
            dst[slot, (ph % 2) * rows:(ph % 2 + 1) * rows, :] = stg_a[...].astype(BF16)
        else:
            rows = Fe // 2
            wd_bf[slot, (ph % 2) * rows:(ph % 2 + 1) * rows, :] = stg_b[...].astype(BF16)

    def row_copy(tile_id, r, slot):
        tok = rt_ref[tile_id * tm + r]
        src = h_hbm.at[pl.ds(pl.multiple_of(tok * ROW_SUBLANES, ROW_SUBLANES), ROW_SUBLANES)]
        return pltpu.make_async_copy(src, xbuf.at[slot, pl.ds(r * ROW_SUBLANES, ROW_SUBLANES)], sem.at[slot])

    def tile_wait(slot):
        pltpu.make_async_copy(h_hbm.at[pl.ds(0, tm * ROW_SUBLANES)], xbuf.at[slot], sem.at[slot]).wait()

    @pl.when(i == 0)
    def _():
        _invert_rows(dest_ref, lo_ref, hi_ref, rt_ref)
        for t in range(MOE_AHEAD):
            def issue(r, carry, t=t):
                row_copy(jnp.minimum(t, last), r, t).start()
                return carry
            lax.fori_loop(0, tm, issue, 0)

    @pl.when(i < n_used)
    def _():
        slot = i % MOE_SLOTS
        nslot = (i + MOE_AHEAD) % MOE_SLOTS
        ntile = jnp.minimum(i + MOE_AHEAD, last)
        n_f, n_d = Fe // MOE_F_CHUNK, D // MOE_D_CHUNK
        per = tm // (n_f + n_d)

        def issue_part(p):
            for r in range(p * per, (p + 1) * per):
                row_copy(ntile, r, nslot).start()

        expert = te_ref[i]
        wslot = par_ref[i]
        k = pos_ref[i]
        nxt = nxt_ref[i]

        @pl.when(i == 0)
        def _():
            for ph in range(MOE_W_PHASES):
                piece(ph, expert).start()
                piece(ph, expert).wait()
                round_piece(ph, wslot)

        @pl.when((k == 0) & (i > 0))
        def _():
            done = jnp.minimum(pn_ref[i] - 1, MOE_W_PHASES)
            for ph in range(MOE_W_PHASES):
                @pl.when(ph >= done)
                def _(ph=ph):
                    @pl.when(ph > done)
                    def _():
                        piece(ph, expert).start()
                    piece(ph, expert).wait()
                    round_piece(ph, wslot)

        for ph in range(MOE_W_PHASES):
            @pl.when((k == ph + 1) & (nxt >= 0))
            def _(ph=ph):
                piece(ph, nxt).wait()
                round_piece(ph, 1 - wslot)

        for ph in range(MOE_W_PHASES):
            @pl.when((k == ph) & (nxt >= 0))
            def _(ph=ph):
                piece(ph, nxt).start()

        tile_wait(slot)
        words = [xbuf[slot, pl.ds(s, tm, stride=ROW_SUBLANES), :] for s in range(ROW_SUBLANES)]
        x = jnp.concatenate([_unpack_hi(w).astype(BF16) for w in words]
                            + [_unpack_lo(w).astype(BF16) for w in words], axis=1)
        for c in range(n_f):
            issue_part(c)
            cs = slice(c * MOE_F_CHUNK, (c + 1) * MOE_F_CHUNK)
            g = jnp.dot(x, wg_bf[wslot, :, cs], preferred_element_type=F32) + bg_ref[0, expert, :, cs]
            u = jnp.dot(x, wu_bf[wslot, :, cs], preferred_element_type=F32) + bu_ref[0, expert, :, cs]
            g = jnp.minimum(g, SWIGLU_LIMIT)
            u = jnp.clip(u, -SWIGLU_LIMIT, SWIGLU_LIMIT)
            act_scr[:, cs] = (g * jax.nn.sigmoid(SWIGLU_ALPHA * g) * (u + 1.0)).astype(BF16)
        act = act_scr[...]
        for c in range(n_d):
            issue_part(n_f + c)
            c0 = c * MOE_D_CHUNK
            y = (jnp.dot(act, wd_bf[wslot, :, c0:c0 + MOE_D_CHUNK], preferred_element_type=F32)
                 + bd_ref[0, expert, :, c0:c0 + MOE_D_CHUNK])
            for j in range(MOE_D_CHUNK // LANES):
                col = c0 + j * LANES
                yj = y[:, j * LANES:(j + 1) * LANES]
                if col < half:
                    hi_scr[:, col:col + LANES] = _bf16_bits(yj)
                else:
                    s = (col - half) // LANES
                    o_ref[pl.ds(s, tm, stride=ROW_SUBLANES), :] = hi_scr[:, col - half:col - half + LANES] | (_bf16_bits(yj) >> 16)

        @pl.when(i == last)
        def _():
            for t in range(1, MOE_AHEAD + 1):
                tile_wait((i + t) % MOE_SLOTS)

    @pl.when(i >= n_used)
    def _():
        o_ref[...] = jnp.zeros_like(o_ref)


def _group_tables(tile_expert, n_used):
    n = tile_expert.shape[0]
    idx = jnp.arange(n, dtype=jnp.int32)
    first = jnp.concatenate([jnp.ones((1,), bool), tile_expert[1:] != tile_expert[:-1]])
    par = (jnp.cumsum(first.astype(jnp.int32)) - 1) % 2
    pos = idx - lax.cummax(jnp.where(first, idx, 0))
    later = (idx[None, :] > idx[:, None]) & (idx[None, :] < n_used[0]) & (tile_expert[None, :] != tile_expert[:, None])
    nxt_idx = jnp.min(jnp.where(later, idx[None, :], n), axis=1)
    nxt = jnp.where(nxt_idx < n, tile_expert[jnp.minimum(nxt_idx, n - 1)], -1)
    pn = jnp.concatenate([jnp.zeros((1,), jnp.int32), pos[:-1] + 1])
    return par.astype(jnp.int32), pos.astype(jnp.int32), nxt.astype(jnp.int32), pn.astype(jnp.int32)


def _moe(layer, tile_expert, dest, pad_rows, n_used, h2p, wg, bg, wu, bu, wd, bd, tm):
    L, E, D, Fe = wg.shape
    n_tiles = tile_expert.shape[0]
    lo, hi = pad_rows
    assert tm % (Fe // MOE_F_CHUNK + D // MOE_D_CHUNK) == 0 and (D // 2) % MOE_D_CHUNK == 0
    par, pos, nxt, pn = _group_tables(tile_expert, n_used)
    bspec = lambda w: pl.BlockSpec((1, E, 1, w), lambda i, *_: (layer, 0, 0, 0), pipeline_mode=pl.Buffered(1))
    hbm = pl.BlockSpec(memory_space=pl.ANY)
    return pl.pallas_call(
        functools.partial(_moe_kernel, layer=layer),
        out_shape=jax.ShapeDtypeStruct((n_tiles * tm * ROW_SUBLANES, LANES), jnp.uint32),
        grid_spec=pltpu.PrefetchScalarGridSpec(
            num_scalar_prefetch=9,
            grid=(n_tiles,),
            in_specs=[hbm, hbm, bspec(Fe), hbm, bspec(Fe), hbm, bspec(D)],
            out_specs=pl.BlockSpec((tm * ROW_SUBLANES, LANES), lambda i, *_: (i, 0)),
            scratch_shapes=[pltpu.VMEM((MOE_SLOTS, tm * ROW_SUBLANES, LANES), jnp.uint32),
                            pltpu.VMEM((tm, Fe), BF16), pltpu.VMEM((tm, D // 2), jnp.uint32),
                            pltpu.VMEM((2, D, Fe), BF16), pltpu.VMEM((2, D, Fe), BF16),
                            pltpu.VMEM((2, Fe, D), BF16),
                            pltpu.VMEM((D // 2, Fe), F32), pltpu.VMEM((Fe // 2, D), F32),
                            pltpu.SMEM((n_tiles * tm,), jnp.int32),
                            pltpu.SemaphoreType.DMA((MOE_SLOTS,)), pltpu.SemaphoreType.DMA],
        ),
        compiler_params=_cparams(("arbitrary",)),
        name="moe_experts",
    )(tile_expert, dest, n_used, par, pos, nxt, pn, lo, hi, h2p, wg, bg.reshape(L, E, 1, Fe),
      wu, bu.reshape(L, E, 1, Fe), wd, bd.reshape(L, E, 1, D))


COMBINE_AHEAD = 2
COMBINE_SLOTS = COMBINE_AHEAD + 1
COMBINE_ROW_CHUNK = 8


def _combine_kernel(dest_ref, yp_hbm, gate_ref, x_ref, gpost_ref, gtf_ref, o_ref, buf, sem):
    i = pl.program_id(0)
    n = pl.num_programs(0)
    tc, D = x_ref.shape
    half = D // 2
    K = TOPK_EXPERTS

    def row_copies(tile_id, r, slot):
        base = tile_id * tc * K
        for kk in range(K):
            d = dest_ref[base + r * K + kk]
            src = yp_hbm.at[pl.ds(pl.multiple_of(d * ROW_SUBLANES, ROW_SUBLANES), ROW_SUBLANES)]
            dst = buf.at[slot, kk, pl.ds(r * ROW_SUBLANES, ROW_SUBLANES)]
            pltpu.make_async_copy(src, dst, sem.at[slot]).start(priority=kk % 2)

    def tile_wait(slot):
        for kk in range(K):
            pltpu.make_async_copy(yp_hbm.at[pl.ds(0, tc * ROW_SUBLANES)], buf.at[slot, kk], sem.at[slot]).wait()

    @pl.when(i == 0)
    def _():
        for t in range(COMBINE_AHEAD):
            def issue(r, carry, t=t):
                row_copies(jnp.minimum(t, n - 1), r, t)
                return carry
            lax.fori_loop(0, tc, issue, 0)

    slot = i % COMBINE_SLOTS
    nslot = (i + COMBINE_AHEAD) % COMBINE_SLOTS
    ntile = jnp.minimum(i + COMBINE_AHEAD, n - 1)
    tile_wait(slot)
    RC = COMBINE_ROW_CHUNK
    for c in range(tc // RC):
        for r in range(c * RC, (c + 1) * RC):
            row_copies(ntile, r, nslot)
        rows = slice(c * RC, (c + 1) * RC)
        gates = gate_ref[rows, :]
        gk = [jnp.broadcast_to(gates[:, kk:kk + 1], (RC, LANES)) for kk in range(K)]
        his, los = [], []
        sq = jnp.zeros((RC, LANES), F32)
        for s in range(ROW_SUBLANES):
            hi = lo = None
            for kk in range(K):
                w = buf[slot, kk, pl.ds(c * RC * ROW_SUBLANES + s, RC, stride=ROW_SUBLANES), :]
                h, l = gk[kk] * _unpack_hi(w), gk[kk] * _unpack_lo(w)
                hi, lo = (h, l) if kk == 0 else (hi + h, lo + l)
            his.append(hi)
            los.append(lo)
            sq = sq + hi * hi + lo * lo
        inv = lax.rsqrt(jnp.sum(sq, axis=-1, keepdims=True) * (1.0 / D) + RMS_EPS)
        for s in range(ROW_SUBLANES):
            for col, y in ((s * LANES, his[s]), (half + s * LANES, los[s])):
                cs = slice(col, col + LANES)
                o_ref[rows, cs] = x_ref[rows, cs] + gtf_ref[0, :, cs] * ((y * inv) * gpost_ref[:, cs])

    @pl.when(i == n - 1)
    def _():
        for t in range(1, COMBINE_AHEAD + 1):
            tile_wait((i + t) % COMBINE_SLOTS)


def _combine(dest, yp, gates, x1, gpost, gtf, seq):
    T, D = x1.shape
    tc = 512
    per_b = seq // tc
    return pl.pallas_call(
        _combine_kernel,
        out_shape=jax.ShapeDtypeStruct((T, D), F32),
        grid_spec=pltpu.PrefetchScalarGridSpec(
            num_scalar_prefetch=1,
            grid=(T // tc,),
            in_specs=[pl.BlockSpec(memory_space=pl.ANY),
                      pl.BlockSpec((tc, LANES), lambda i, d: (i, 0)),
                      pl.BlockSpec((tc, D), lambda i, d: (i, 0)),
                      pl.BlockSpec((1, D), lambda i, d: (0, 0)),
                      pl.BlockSpec((1, 1, D), lambda i, d: (i // per_b, 0, 0))],
            out_specs=pl.BlockSpec((tc, D), lambda i, d: (i, 0)),
            scratch_shapes=[pltpu.VMEM((COMBINE_SLOTS, TOPK_EXPERTS, tc * ROW_SUBLANES, LANES), jnp.uint32),
                            pltpu.SemaphoreType.DMA((COMBINE_SLOTS,))],
        ),
        compiler_params=_cparams(("arbitrary",)),
        name="moe_combine",
    )(dest, yp, gates, x1, gpost, gtf)


def _dest_kernel(idx_ref, rank_ref, cnt_ref, dest_ref, *, tm_rows):
    tm = idx_ref.shape[0]
    tiles = jnp.floor((cnt_ref[0:1, :] + float(tm_rows - 1)) * (1.0 / tm_rows))
    r_i = lax.broadcasted_iota(jnp.int32, (LANES, LANES), 0)
    c_i = lax.broadcasted_iota(jnp.int32, (LANES, LANES), 1)
    before = jnp.where(r_i < c_i, 1.0, 0.0).astype(BF16)
    start = jnp.dot(jnp.broadcast_to(tiles, (8, LANES)).astype(BF16), before,
                    preferred_element_type=F32)[0:1] * float(tm_rows)
    lane = lax.broadcasted_iota(jnp.int32, (tm, LANES), 1)
    idx = idx_ref[...]
    rank = rank_ref[...]
    out = jnp.zeros((tm, LANES), jnp.int32)
    for kk in range(TOPK_EXPERTS):
        base = jnp.sum(jnp.where(lane == idx[:, kk:kk + 1], start, 0.0), axis=-1, keepdims=True)
        out = jnp.where(lane == kk, base.astype(jnp.int32) + rank[:, kk:kk + 1], out)
    dest_ref[...] = out


def _dest(idx, rank, cnt, tm_rows):
    T = idx.shape[0]
    tm = 2048
    assert tm_rows & (tm_rows - 1) == 0
    row = pl.BlockSpec((tm, LANES), lambda i: (i, 0))
    return pl.pallas_call(
        functools.partial(_dest_kernel, tm_rows=tm_rows),
        out_shape=jax.ShapeDtypeStruct((T, LANES), jnp.int32),
        grid=(T // tm,),
        in_specs=[row, row, pl.BlockSpec((8, LANES), lambda i: (0, 0))],
        out_specs=row,
        compiler_params=_cparams(("parallel",)),
        name="route_dest",
    )(idx, rank, cnt)


INVERT_UNROLL = 8


def _invert_rows(dest_ref, lo_ref, hi_ref, out_ref):
    def fill(j, carry):
        out_ref[j] = 0
        return carry

    for e in range(lo_ref.shape[0]):
        lax.fori_loop(lo_ref[e], hi_ref[e], fill, 0)

    per = INVERT_UNROLL // TOPK_EXPERTS

    def body(b, carry):
        for u in range(INVERT_UNROLL):
            out_ref[dest_ref[b * INVERT_UNROLL + u]] = b * per + u // TOPK_EXPERTS
        return carry

    lax.fori_loop(0, dest_ref.shape[0] // INVERT_UNROLL, body, 0)


def _routing_tables(idx, rank, cnt, n_experts, tm):
    T = idx.shape[0]
    n_tiles = (T * TOPK_EXPERTS) // tm + n_experts
    counts = cnt[0, :n_experts].astype(jnp.int32)
    padded = ((counts + tm - 1) // tm) * tm
    pad_end = jnp.cumsum(padded)
    tile_start = jnp.arange(n_tiles, dtype=jnp.int32) * tm
    tile_expert = jnp.minimum(jnp.sum((pad_end[None, :] <= tile_start[:, None]).astype(jnp.int32), axis=1),
                              n_experts - 1)
    n_used = pad_end[-1:] // tm
    pad_rows = (pad_end - padded + counts, pad_end)
    dest = _dest(idx, rank, cnt, tm)[:, :TOPK_EXPERTS].reshape(-1)
    return dest, pad_rows, tile_expert, n_used


def kernel(x, c, w_ada, b_ada, g_pre_mix, g_post_mix, g_pre_ffn, g_post_ffn, w_in, w_out, pool_w, pool_scale,
           conv_w, rel_bias, router_w, router_b, w_gate, b_gate, w_up, b_up, w_down, b_down):
    B, S, D = x.shape
    L = w_ada.shape[0]
    T = B * S
    E = router_w.shape[-1]
    attn_w = D // 2
    heads = attn_w // HEAD_DIM
    moe_tm = 256
    assert D == 2 * ROW_SUBLANES * LANES

    mod = _ada(c, w_ada, b_ada)
    w_in_bf = w_in.astype(BF16)
    x2 = x.reshape(T, D)
    for l in range(L):
        sh_m, sc_m, gt_m, sh_f, sc_f, gt_f = [mod[l, :, i * D:(i + 1) * D].reshape(B, 1, D) for i in range(N_MOD)]
        row = lambda v: v.reshape(1, D)

        qkv, rest = _inproj(l, x2, row(g_pre_mix[l]), sc_m, sh_m, w_in_bf, S, attn_w)
        att = _attention(qkv, rel_bias, B, S, heads)
        pc = _poolconv(rest, pool_w[l], pool_scale[l], conv_w[l], B, S)

        wo = w_out[l].astype(BF16)
        rw = jnp.zeros((D, LANES), F32).at[:, :E].set(router_w[l])
        rwh = rw.astype(BF16)
        rwl = (rw - rwh.astype(F32)).astype(BF16)
        rb = jnp.full((1, LANES), NEG_INF, F32).at[0, :E].set(router_b[l])
        x1, h2, idx, gates, rank, cnt = _outproj(
            att, pc, x2, wo[:attn_w], wo[attn_w:], row(g_post_mix[l]), gt_m, row(g_pre_ffn[l]), sc_f, sh_f,
            rwh, rwl, rb, S)

        dest, pad_rows, tile_expert, n_used = _routing_tables(idx, rank, cnt, E, moe_tm)
        yp = _moe(l, tile_expert, dest, pad_rows, n_used, h2, w_gate, b_gate, w_up, b_up, w_down, b_down, moe_tm)
        x2 = _combine(dest, yp, gates, x1, row(g_post_ffn[l]), gt_f, S)
    return x2.reshape(B, S, D)
```

```python
import functools
import math

import numpy as np
import jax
import jax.numpy as jnp
from jax import lax
from jax.experimental import pallas as pl
from jax.experimental.pallas import tpu as pltpu

F32 = jnp.float32
BF16 = jnp.bfloat16

LANES = 128
HEAD_DIM = 128
MOBA_BLOCK = 256
MOBA_TOPK = 3
REL_BUCKETS = 32
REL_MAX_DISTANCE = 128
POOL_WINDOWS = (2, 4, 8, 16)
GROUP_DIM = 128
CONV_K = 3
TOPK_EXPERTS = 4
SWIGLU_LIMIT = 7.0
SWIGLU_ALPHA = 1.702
N_MOD = 6
RMS_EPS = 1e-6
NEG_INF = -1e30
ATTN_SCALE = HEAD_DIM ** -0.5
LOG2E = math.log2(math.e)

VMEM_LIMIT = 56 * 1024 * 1024


def _cparams(sem):
    return pltpu.CompilerParams(dimension_semantics=sem, vmem_limit_bytes=VMEM_LIMIT)


def _rms(x):
    return x * lax.rsqrt(jnp.mean(x * x, axis=-1, keepdims=True) + RMS_EPS)


ROW_SUBLANES = 8


def _bf16_bits(v):
    return lax.bitcast_convert_type(v.astype(BF16).astype(F32), jnp.uint32)


def _pack_words(hi, lo):
    return _bf16_bits(hi) | (_bf16_bits(lo) >> 16)


def _unpack_hi(w):
    return lax.bitcast_convert_type(w & jnp.uint32(0xFFFF0000), F32)


def _unpack_lo(w):
    return lax.bitcast_convert_type(w << 16, F32)


def _ada_kernel(c_ref, w_ref, b_ref, o_ref):
    c = c_ref[...]
    ca = (c * jax.nn.sigmoid(c)).astype(BF16)
    o_ref[0] = jnp.dot(ca, w_ref[0].astype(BF16), preferred_element_type=F32) + b_ref[0]


def _ada(c, w_ada, b_ada):
    L, D, N = w_ada.shape
    B = c.shape[0]
    tn = 1024
    return pl.pallas_call(
        _ada_kernel,
        out_shape=jax.ShapeDtypeStruct((L, B, N), F32),
        grid=(L, N // tn),
        in_specs=[
            pl.BlockSpec((B, D), lambda l, j: (0, 0)),
            pl.BlockSpec((1, D, tn), lambda l, j: (l, 0, j)),
            pl.BlockSpec((1, 1, tn), lambda l, j: (l, 0, j)),
        ],
        out_specs=pl.BlockSpec((1, B, tn), lambda l, j: (l, 0, j)),
        compiler_params=_cparams(("parallel", "parallel")),
        name="ada_mod",
    )(c, w_ada, b_ada.reshape(L, 1, N))


INPROJ_TM = 1024
INPROJ_TN = 1024


def _inproj_kernel(x_ref, g_ref, sc_ref, sh_ref, w_ref, qkv_ref, rest_ref, h_scr, *, n_qkv):
    j = pl.program_id(1)

    @pl.when(j == 0)
    def _():
        h = _rms(x_ref[...]) * g_ref[...]
        h = h * (1.0 + sc_ref[0]) + sh_ref[0]
        h_scr[...] = h.astype(BF16)

    acc = jnp.dot(h_scr[...], w_ref[0], preferred_element_type=F32)

    @pl.when(j < n_qkv)
    def _():
        qkv_ref[...] = acc.astype(BF16)

    @pl.when(j >= n_qkv)
    def _():
        rest_ref[...] = acc


def _inproj(layer, x2, g, sc, sh, w_bf, seq, attn_w):
    T, D = x2.shape
    N = w_bf.shape[2]
    tm, tn = INPROJ_TM, INPROJ_TN
    n_qkv = 3 * attn_w // tn
    n_rest = (N - 3 * attn_w) // tn
    per_b = seq // tm
    return pl.pallas_call(
        functools.partial(_inproj_kernel, n_qkv=n_qkv),
        out_shape=(jax.ShapeDtypeStruct((T, 3 * attn_w), BF16),
                   jax.ShapeDtypeStruct((T, N - 3 * attn_w), F32)),
        grid=(T // tm, n_qkv + n_rest),
        in_specs=[
            pl.BlockSpec((tm, D), lambda i, j: (i, 0)),
            pl.BlockSpec((1, D), lambda i, j: (0, 0)),
            pl.BlockSpec((1, 1, D), lambda i, j: (i // per_b, 0, 0)),
            pl.BlockSpec((1, 1, D), lambda i, j: (i // per_b, 0, 0)),
            pl.BlockSpec((1, D, tn), lambda i, j: (layer, 0, j)),
        ],
        out_specs=(
            pl.BlockSpec((tm, tn), lambda i, j: (i, jnp.minimum(j, n_qkv - 1))),
            pl.BlockSpec((tm, tn), lambda i, j: (i, jnp.maximum(j - n_qkv, 0))),
        ),
        scratch_shapes=[pltpu.VMEM((tm, D), BF16)],
        compiler_params=_cparams(("parallel", "arbitrary")),
        name="mixer_in_proj",
    )(x2, g, sc, sh, w_bf)


def _rel_bucket_np(n):
    n = np.maximum(n, 0)
    max_exact = REL_BUCKETS // 2
    nf = np.maximum(n, max_exact).astype(np.float32)
    large = max_exact + (np.log(nf / np.float32(max_exact)) / np.float32(math.log(REL_MAX_DISTANCE / max_exact))
                         * np.float32(REL_BUCKETS - max_exact)).astype(np.int32)
    large = np.minimum(large, REL_BUCKETS - 1)
    return np.where(n < max_exact, n, large).astype(np.int32)


def _bucket_tables(seq):
    qi = np.arange(MOBA_BLOCK)[:, None]
    ki = np.arange(MOBA_BLOCK)[None, :]
    own = np.where(ki <= qi, _rel_bucket_np(qi - ki), -1)
    prev = _rel_bucket_np(qi - ki + MOBA_BLOCK)
    far = _rel_bucket_np(np.arange(MOBA_BLOCK + 1, max(seq, MOBA_BLOCK + 2)))
    assert np.all(far == far[0])
    return np.stack([own, prev]).astype(np.int32), int(far[0])


ATTN_HEADS_PER_STEP = 2


def _attn_kernel(tab_ref, q_ref, k_ref, v_ref, bkt_ref, o_ref, bias_scr, *, nblk, far_bucket):
    hg = pl.program_id(0)
    b = pl.program_id(1)
    L = MOBA_BLOCK
    G = ATTN_HEADS_PER_STEP

    @pl.when(b == 0)
    def _():
        for g in range(G):
            for m in range(2):
                bk = bkt_ref[m]
                acc = jnp.full((L, L), NEG_INF, F32)
                for r in range(REL_BUCKETS):
                    acc = jnp.where(bk == r, tab_ref[hg * G + g, r] * LOG2E, acc)
                bias_scr[g, m] = acc

    dn = (((1,), (1,)), ((), ()))
    lane = lax.broadcasted_iota(jnp.int32, (L, LANES), 1)
    heads = []
    for g in range(G):
        cols = slice(g * HEAD_DIM, (g + 1) * HEAD_DIM)
        q, k, v = q_ref[:, cols], k_ref[:, cols], v_ref[:, cols]
        kmean = jnp.mean(k.astype(F32).reshape(nblk, L, HEAD_DIM), axis=1)
        kmean = jnp.concatenate([kmean, jnp.zeros((LANES - nblk, HEAD_DIM), F32)], axis=0).astype(BF16)
        gate = lax.dot_general(q, kmean, dn, preferred_element_type=F32)
        heads.append((q, k, v, gate, tab_ref[hg * G + g, far_bucket] * LOG2E))

    for a, g in [(a, g) for a in range(nblk) for g in range(G)]:
        q, k, v, gate, far_bias = heads[g]
        cols = slice(g * HEAD_DIM, (g + 1) * HEAD_DIM)
        qa = q[a * L:(a + 1) * L]
        n = (a + 1) * L
        s = lax.dot_general(qa, k[:n], dn, preferred_element_type=F32) * (ATTN_SCALE * LOG2E)
        hide = None
        if a > MOBA_TOPK:
            ga = jnp.where(lane < a, gate[a * L:(a + 1) * L], NEG_INF)
            cnt = jnp.zeros((L, LANES), F32)
            for i in range(a):
                gi = ga[:, i:i + 1]
                ge = jnp.where(gi >= ga, 1.0, 0.0)
                gt = jnp.where(gi > ga, 1.0, 0.0)
                cnt = cnt + jnp.where(lane > i, ge, gt)
            hide = jnp.where((cnt < float(MOBA_TOPK)) & (lane < a), 0.0, NEG_INF)
            hide_far = hide + far_bias
        pieces = []
        for j in range(a + 1):
            sj = s[:, j * L:(j + 1) * L]
            if j == a:
                sj = sj + bias_scr[g, 0]
            elif j == a - 1:
                sj = sj + bias_scr[g, 1]
                if hide is not None:
                    sj = sj + hide[:, j:j + 1]
            else:
                sj = sj + (far_bias if hide is None else hide_far[:, j:j + 1])
            pieces.append(sj)
        s = pieces[0] if a == 0 else jnp.concatenate(pieces, axis=1)
        m = jnp.max(s, axis=-1, keepdims=True)
        p = jnp.exp2(s - m)
        l = jnp.sum(p, axis=-1, keepdims=True)
        o = jnp.dot(p.astype(BF16), v[:n], preferred_element_type=F32) / l
        o_ref[a * L:(a + 1) * L, cols] = o.astype(o_ref.dtype)


def _attention(qkv, rel_bias, batch, seq, heads):
    T = qkv.shape[0]
    nblk = seq // MOBA_BLOCK
    G = ATTN_HEADS_PER_STEP
    assert heads % G == 0
    ng = heads // G
    bkt, far_bucket = _bucket_tables(seq)
    table = rel_bias.T.astype(F32)
    return pl.pallas_call(
        functools.partial(_attn_kernel, nblk=nblk, far_bucket=far_bucket),
        out_shape=jax.ShapeDtypeStruct((T, heads * HEAD_DIM), BF16),
        grid=(ng, batch),
        in_specs=[
            pl.BlockSpec(memory_space=pltpu.SMEM),
            pl.BlockSpec((seq, G * HEAD_DIM), lambda h, b: (b, h)),
            pl.BlockSpec((seq, G * HEAD_DIM), lambda h, b: (b, ng + h)),
            pl.BlockSpec((seq, G * HEAD_DIM), lambda h, b: (b, 2 * ng + h)),
            pl.BlockSpec((2, MOBA_BLOCK, MOBA_BLOCK), lambda h, b: (0, 0, 0)),
        ],
        out_specs=pl.BlockSpec((seq, G * HEAD_DIM), lambda h, b: (b, h)),
        scratch_shapes=[pltpu.VMEM((G, 2, MOBA_BLOCK, MOBA_BLOCK), F32)],
        compiler_params=_cparams(("arbitrary", "arbitrary")),
        name="moba_attention",
    )(table, qkv, qkv, qkv, jnp.asarray(bkt))


MIX_HALO = 16


def _pool_conv_tile(u_ref, gb_ref, gc_ref, hc_ref, uh_ref, gch_ref, hch_ref, pw_ref, ps_ref, cw_ref, pc_scr,
                    seq_start, t0):
    tm = u_ref.shape[0]
    G = len(POOL_WINDOWS)
    H = MIX_HALO
    pos = lax.broadcasted_iota(jnp.int32, (tm, GROUP_DIM), 0) + t0

    def with_halo(cur_ref, halo_ref, cs):
        return jnp.concatenate([jnp.where(seq_start, 0.0, halo_ref[:, cs]), cur_ref[:, cs]], axis=0)

    def shift(x, k):
        return pltpu.roll(x, k, 0)

    for gi, w in enumerate(POOL_WINDOWS):
        cs = slice(gi * GROUP_DIM, (gi + 1) * GROUP_DIM)
        ug = with_halo(u_ref, uh_ref, cs)
        win = ug
        k = 1
        while k < w:
            win = win + shift(win, k)
            k *= 2
        count = jnp.minimum(pos + 1, w).astype(F32)
        d = win[H:] / count - ug[H:]
        y = jnp.dot(d.astype(BF16), pw_ref[gi].astype(BF16), preferred_element_type=F32)
        pc_scr[:, cs] = (y * ps_ref[:, cs]).astype(pc_scr.dtype)

        uu = with_halo(gc_ref, gch_ref, cs) * with_halo(hc_ref, hch_ref, cs)
        conv = (cw_ref[0:1, cs] * shift(uu, 2) + cw_ref[1:2, cs] * shift(uu, 1)
                + cw_ref[2:3, cs] * uu)[H:]
        oc = slice((G + gi) * GROUP_DIM, (G + gi + 1) * GROUP_DIM)
        pc_scr[:, oc] = (gb_ref[:, cs] * conv).astype(pc_scr.dtype)


def _outproj_kernel(att_ref, u_ref, gb_ref, gc_ref, hc_ref, uh_ref, gch_ref, hch_ref, pw_ref, ps_ref, cw_ref,
                    x_ref, wa_ref, wp_ref, gpost_ref, gtm_ref, gpre_ref, scf_ref, shf_ref,
                    rwh_ref, rwl_ref, rb_ref,
                    x1_ref, h2_ref, idx_ref, gate_ref, rank_ref, cnt_ref, pc_scr, carry_scr, *, per_seq):
    i = pl.program_id(0)
    tm = x_ref.shape[0]

    @pl.when(i == 0)
    def _():
        carry_scr[...] = jnp.zeros_like(carry_scr)

    _pool_conv_tile(u_ref, gb_ref, gc_ref, hc_ref, uh_ref, gch_ref, hch_ref, pw_ref, ps_ref, cw_ref, pc_scr,
                    seq_start=i % per_seq == 0, t0=(i % per_seq) * tm)
    mix = (jnp.dot(att_ref[...], wa_ref[...], preferred_element_type=F32)
           + jnp.dot(pc_scr[...], wp_ref[...], preferred_element_type=F32))
    x1 = x_ref[...] + gtm_ref[0] * (_rms(mix) * gpost_ref[...])
    x1_ref[...] = x1
    h2 = (_rms(x1) * gpre_ref[...]) * (1.0 + scf_ref[0]) + shf_ref[0]
    half = h2.shape[1] // 2
    for s in range(ROW_SUBLANES):
        h2_ref[pl.ds(s, tm, stride=ROW_SUBLANES), :] = _pack_words(h2[:, s * LANES:(s + 1) * LANES],
                                      h2[:, half + s * LANES:half + (s + 1) * LANES])

    hi = h2.astype(BF16)
    lo = (h2 - hi.astype(F32)).astype(BF16)
    r2 = jnp.dot(hi, jnp.concatenate([rwh_ref[...], rwl_ref[...]], axis=1), preferred_element_type=F32)
    logits = (r2[:, :LANES] + r2[:, LANES:]
              + jnp.dot(lo, rwh_ref[...], preferred_element_type=F32)) + rb_ref[...]

    lane = lax.broadcasted_iota(jnp.int32, (tm, LANES), 1)
    work = logits
    mem = jnp.zeros((tm, LANES), F32)
    vals, hots = [], []
    idx_out = jnp.zeros((tm, LANES), jnp.int32)
    for kk in range(TOPK_EXPERTS):
        m = jnp.max(work, axis=-1, keepdims=True)
        ik = jnp.min(jnp.where(work == m, lane, LANES), axis=-1, keepdims=True)
        hot = lane == ik
        vals.append(m)
        hots.append(hot)
        idx_out = jnp.where(lane == kk, ik, idx_out)
        mem = jnp.where(hot, 1.0, mem)
        work = jnp.where(hot, -jnp.inf, work)
    idx_ref[...] = idx_out

    es = [jnp.exp(vk - vals[0]) for vk in vals]
    denom = es[0]
    for e in es[1:]:
        denom = denom + e
    gates = jnp.zeros((tm, LANES), F32)
    for kk in range(TOPK_EXPERTS):
        gates = jnp.where(lane == kk, es[kk] / denom, gates)
    gate_ref[...] = gates

    r_i = lax.broadcasted_iota(jnp.int32, (tm, tm), 0)
    c_i = lax.broadcasted_iota(jnp.int32, (tm, tm), 1)
    tri = jnp.where(c_i < r_i, 1.0, 0.0).astype(BF16)
    before = jnp.dot(tri, mem.astype(BF16), preferred_element_type=F32) + carry_scr[0:1, :]
    ranks = jnp.zeros((tm, LANES), jnp.int32)
    for kk in range(TOPK_EXPERTS):
        rk = jnp.sum(jnp.where(hots[kk], before, 0.0), axis=-1, keepdims=True)
        ranks = jnp.where(lane == kk, rk.astype(jnp.int32), ranks)
    rank_ref[...] = ranks
    carry = carry_scr[...] + jnp.sum(mem, axis=0, keepdims=True)
    carry_scr[...] = carry
    cnt_ref[...] = carry


def _outproj(att, rest, pool_w, pool_scale, conv_w, x2, wa, wp, gpost, gtm, gpre, scf, shf, rwh, rwl, rb, seq):
    T, D = x2.shape
    tm = 512
    per_b = seq // tm
    W = len(POOL_WINDOWS) * GROUP_DIM
    row = lambda w: pl.BlockSpec((tm, w), lambda i: (i, 0))
    cur = lambda c: pl.BlockSpec((tm, W), lambda i: (i, c))
    halo = lambda c: pl.BlockSpec((MIX_HALO, W), lambda i: (jnp.maximum(i * (tm // MIX_HALO) - 1, 0), c))
    const = lambda shape: pl.BlockSpec(shape, lambda i: tuple(0 for _ in shape), pipeline_mode=pl.Buffered(1))
    perb = pl.BlockSpec((1, 1, D), lambda i: (i // per_b, 0, 0))
    return pl.pallas_call(
        functools.partial(_outproj_kernel, per_seq=per_b),
        out_shape=(jax.ShapeDtypeStruct((T, D), F32), jax.ShapeDtypeStruct((T * ROW_SUBLANES, LANES), jnp.uint32),
                   jax.ShapeDtypeStruct((T, LANES), jnp.int32), jax.ShapeDtypeStruct((T, LANES), F32),
                   jax.ShapeDtypeStruct((T, LANES), jnp.int32), jax.ShapeDtypeStruct((8, LANES), F32)),
        grid=(T // tm,),
        in_specs=[row(att.shape[1]), cur(0), cur(1), cur(2), cur(3), halo(0), halo(2), halo(3),
                  const(pool_w.shape), const((1, W)), const((CONV_K, W)),
                  row(D), const(wa.shape), const(wp.shape),
                  const((1, D)), perb, const((1, D)), perb, perb,
                  const(rwh.shape), const(rwl.shape), const((1, LANES))],
        out_specs=(row(D), pl.BlockSpec((tm * ROW_SUBLANES, LANES), lambda i: (i, 0)),
                   row(LANES), row(LANES), row(LANES), pl.BlockSpec((8, LANES), lambda i: (0, 0))),
        scratch_shapes=[pltpu.VMEM((tm, 2 * W), BF16), pltpu.VMEM((8, LANES), F32)],
        compiler_params=_cparams(("arbitrary",)),
        name="out_proj_router",
    )(att, rest, rest, rest, rest, rest, rest, rest, pool_w, pool_scale.reshape(1, W), conv_w,
      x2, wa, wp, gpost, gtm, gpre, scf, shf, rwh, rwl, rb)


MOE_F_CHUNK = 256
MOE_D_CHUNK = 512
MOE_AHEAD = 2
MOE_SLOTS = MOE_AHEAD + 1


MOE_W_PHASES = 6


def _moe_kernel(te_ref, dest_ref, nu_ref, par_ref, pos_ref, nxt_ref, pn_ref, lo_ref, hi_ref,
                h_hbm, wg_hbm, bg_ref, wu_hbm, bu_ref, wd_hbm, bd_ref, o_ref,
                xbuf, act_scr, hi_scr, wg_bf, wu_bf, wd_bf, stg_a, stg_b, rt_ref, sem, wsem, *, layer):
    i = pl.program_id(0)
    tm = act_scr.shape[0]
    Fe = act_scr.shape[1]
    D = wd_bf.shape[-1]
    half = D // 2
    n_used = nu_ref[0]
    last = n_used - 1

    def piece(ph, expert):
        if ph < 4:
            src_ref, dst = (wg_hbm, stg_a) if ph < 2 else (wu_hbm, stg_a)
            rows = D // 2
        else:
            src_ref, dst, rows = wd_hbm, stg_b, Fe // 2
        r0 = (ph % 2) * rows
        return pltpu.make_async_copy(src_ref.at[layer, expert, pl.ds(r0, rows), :], dst, wsem)

    def round_piece(ph, slot):
        if ph < 4:
            rows = D // 2
            dst = wg_bf if ph < 2 else wu_bf
            dst[slot, (ph % 2) * rows:(ph % 2 + 1) * rows, :] = stg_a[...].astype(BF16)
        else:
            rows = Fe // 2
            wd_bf[slot, (ph % 2) * rows:(ph % 2 + 1) * rows, :] = stg_b[...].astype(BF16)

    def row_copy(tile_id, r, slot):
        tok = rt_ref[tile_id * tm + r]
        src = h_hbm.at[pl.ds(pl.multiple_of(tok * ROW_SUBLANES, ROW_SUBLANES), ROW_SUBLANES)]
        return pltpu.make_async_copy(src, xbuf.at[slot, pl.ds(r * ROW_SUBLANES, ROW_SUBLANES)], sem.at[slot])

    def tile_wait(slot):
        pltpu.make_async_copy(h_hbm.at[pl.ds(0, tm * ROW_SUBLANES)], xbuf.at[slot], sem.at[slot]).wait()

    @pl.when(i == 0)
    def _():
        _invert_rows(dest_ref, lo_ref, hi_ref, rt_ref)
        for t in range(MOE_AHEAD):
            def issue(r, carry, t=t):
                row_copy(jnp.minimum(t, last), r, t).start()
                return carry
            lax.fori_loop(0, tm, issue, 0)

    @pl.when(i < n_used)
    def _():
        slot = i % MOE_SLOTS
        nslot = (i + MOE_AHEAD) % MOE_SLOTS
        ntile = jnp.minimum(i + MOE_AHEAD, last)
        n_f, n_d = Fe // MOE_F_CHUNK, D // MOE_D_CHUNK
        per = tm // (n_f + n_d)

        def issue_part(p):
            for r in range(p * per, (p + 1) * per):
                row_copy(ntile, r, nslot).start()

        expert = te_ref[i]
        wslot = par_ref[i]
        k = pos_ref[i]
        nxt = nxt_ref[i]

        @pl.when(i == 0)
        def _():
            for ph in range(MOE_W_PHASES):
                piece(ph, expert).start()
                piece(ph, expert).wait()
                round_piece(ph, wslot)

        @pl.when((k == 0) & (i > 0))
        def _():
            done = jnp.minimum(pn_ref[i] - 1, MOE_W_PHASES)
            for ph in range(MOE_W_PHASES):
                @pl.when(ph >= done)
                def _(ph=ph):
                    @pl.when(ph > done)
                    def _():
                        piece(ph, expert).start()
                    piece(ph, expert).wait()
                    round_piece(ph, wslot)

        for ph in range(MOE_W_PHASES):
            @pl.when((k == ph + 1) & (nxt >= 0))
            def _(ph=ph):
                piece(ph, nxt).wait()
                round_piece(ph, 1 - wslot)

        for ph in range(MOE_W_PHASES):
            @pl.when((k == ph) & (nxt >= 0))
            def _(ph=ph):
                piece(ph, nxt).start()

        tile_wait(slot)
        words = [xbuf[slot, pl.ds(s, tm, stride=ROW_SUBLANES), :] for s in range(ROW_SUBLANES)]
        x = jnp.concatenate([_unpack_hi(w).astype(BF16) for w in words]
                            + [_unpack_lo(w).astype(BF16) for w in words], axis=1)
        for c in range(n_f):
            issue_part(c)
            cs = slice(c * MOE_F_CHUNK, (c + 1) * MOE_F_CHUNK)
            g = jnp.dot(x, wg_bf[wslot, :, cs], preferred_element_type=F32) + bg_ref[0, expert, :, cs]
            u = jnp.dot(x, wu_bf[wslot, :, cs], preferred_element_type=F32) + bu_ref[0, expert, :, cs]
            g = jnp.minimum(g, SWIGLU_LIMIT)
            u = jnp.clip(u, -SWIGLU_LIMIT, SWIGLU_LIMIT)
            act_scr[:, cs] = (g * jax.nn.sigmoid(SWIGLU_ALPHA * g) * (u + 1.0)).astype(BF16)
        act = act_scr[...]
        for c in range(n_d):
            issue_part(n_f + c)
            c0 = c * MOE_D_CHUNK
            y = (jnp.dot(act, wd_bf[wslot, :, c0:c0 + MOE_D_CHUNK], preferred_element_type=F32)
                 + bd_ref[0, expert, :, c0:c0 + MOE_D_CHUNK])
            for j in range(MOE_D_CHUNK // LANES):
                col = c0 + j * LANES
                yj = y[:, j * LANES:(j + 1) * LANES]
                if col < half:
                    hi_scr[:, col:col + LANES] = _bf16_bits(yj)
                else:
                    s = (col - half) // LANES
                    o_ref[pl.ds(s, tm, stride=ROW_SUBLANES), :] = hi_scr[:, col - half:col - half + LANES] | (_bf16_bits(yj) >> 16)

        @pl.when(i == last)
        def _():
            for t in range(1, MOE_AHEAD + 1):
                tile_wait((i + t) % MOE_SLOTS)

    @pl.when(i >= n_used)
    def _():
        o_ref[...] = jnp.zeros_like(o_ref)


def _group_tables(tile_expert, n_used):
    n = tile_expert.shape[0]
    idx = jnp.arange(n, dtype=jnp.int32)
    first = jnp.concatenate([jnp.ones((1,), bool), tile_expert[1:] != tile_expert[:-1]])
    par = (jnp.cumsum(first.astype(jnp.int32)) - 1) % 2
    pos = idx - lax.cummax(jnp.where(first, idx, 0))
    later = (idx[None, :] > idx[:, None]) & (idx[None, :] < n_used[0]) & (tile_expert[None, :] != tile_expert[:, None])
    nxt_idx = jnp.min(jnp.where(later, idx[None, :], n), axis=1)
    nxt = jnp.where(nxt_idx < n, tile_expert[jnp.minimum(nxt_idx, n - 1)], -1)
    pn = jnp.concatenate([jnp.zeros((1,), jnp.int32), pos[:-1] + 1])
    return par.astype(jnp.int32), pos.astype(jnp.int32), nxt.astype(jnp.int32), pn.astype(jnp.int32)


def _moe(layer, tile_expert, dest, pad_rows, n_used, h2p, wg, bg, wu, bu, wd, bd, tm):
    L, E, D, Fe = wg.shape
    n_tiles = tile_expert.shape[0]
    lo, hi = pad_rows
    assert tm % (Fe // MOE_F_CHUNK + D // MOE_D_CHUNK) == 0 and (D // 2) % MOE_D_CHUNK == 0
    par, pos, nxt, pn = _group_tables(tile_expert, n_used)
    bspec = lambda w: pl.BlockSpec((1, E, 1, w), lambda i, *_: (layer, 0, 0, 0), pipeline_mode=pl.Buffered(1))
    hbm = pl.BlockSpec(memory_space=pl.ANY)
    return pl.pallas_call(
        functools.partial(_moe_kernel, layer=layer),
        out_shape=jax.ShapeDtypeStruct((n_tiles * tm * ROW_SUBLANES, LANES), jnp.uint32),
        grid_spec=pltpu.PrefetchScalarGridSpec(
            num_scalar_prefetch=9,
            grid=(n_tiles,),
            in_specs=[hbm, hbm, bspec(Fe), hbm, bspec(Fe), hbm, bspec(D)],
            out_specs=pl.BlockSpec((tm * ROW_SUBLANES, LANES), lambda i, *_: (i, 0)),
            scratch_shapes=[pltpu.VMEM((MOE_SLOTS, tm * ROW_SUBLANES, LANES), jnp.uint32),
                            pltpu.VMEM((tm, Fe), BF16), pltpu.VMEM((tm, D // 2), jnp.uint32),
                            pltpu.VMEM((2, D, Fe), BF16), pltpu.VMEM((2, D, Fe), BF16),
                            pltpu.VMEM((2, Fe, D), BF16),
                            pltpu.VMEM((D // 2, Fe), F32), pltpu.VMEM((Fe // 2, D), F32),
                            pltpu.SMEM((n_tiles * tm,), jnp.int32),
                            pltpu.SemaphoreType.DMA((MOE_SLOTS,)), pltpu.SemaphoreType.DMA],
        ),
        compiler_params=_cparams(("arbitrary",)),
        name="moe_experts",
    )(tile_expert, dest, n_used, par, pos, nxt, pn, lo, hi, h2p, wg, bg.reshape(L, E, 1, Fe),
      wu, bu.reshape(L, E, 1, Fe), wd, bd.reshape(L, E, 1, D))


COMBINE_AHEAD = 2
COMBINE_SLOTS = COMBINE_AHEAD + 1
COMBINE_ROW_CHUNK = 8


def _combine_kernel(dest_ref, yp_hbm, gate_ref, x_ref, gpost_ref, gtf_ref, o_ref, buf, sem):
    i = pl.program_id(0)
    n = pl.num_programs(0)
    tc, D = x_ref.shape
    half = D // 2
    K = TOPK_EXPERTS

    def row_copies(tile_id, r, slot):
        base = tile_id * tc * K
        for kk in range(K):
            d = dest_ref[base + r * K + kk]
            src = yp_hbm.at[pl.ds(pl.multiple_of(d * ROW_SUBLANES, ROW_SUBLANES), ROW_SUBLANES)]
            dst = buf.at[slot, kk, pl.ds(r * ROW_SUBLANES, ROW_SUBLANES)]
            pltpu.make_async_copy(src, dst, sem.at[slot]).start(priority=kk % 2)

    def tile_wait(slot):
        for kk in range(K):
            pltpu.make_async_copy(yp_hbm.at[pl.ds(0, tc * ROW_SUBLANES)], buf.at[slot, kk], sem.at[slot]).wait()

    @pl.when(i == 0)
    def _():
        for t in range(COMBINE_AHEAD):
            def issue(r, carry, t=t):
                row_copies(jnp.minimum(t, n - 1), r, t)
                return carry
            lax.fori_loop(0, tc, issue, 0)

    slot = i % COMBINE_SLOTS
    nslot = (i + COMBINE_AHEAD) % COMBINE_SLOTS
    ntile = jnp.minimum(i + COMBINE_AHEAD, n - 1)
    tile_wait(slot)
    RC = COMBINE_ROW_CHUNK
    for c in range(tc // RC):
        for r in range(c * RC, (c + 1) * RC):
            row_copies(ntile, r, nslot)
        rows = slice(c * RC, (c + 1) * RC)
        gates = gate_ref[rows, :]
        gk = [jnp.broadcast_to(gates[:, kk:kk + 1], (RC, LANES)) for kk in range(K)]
        his, los = [], []
        sq = jnp.zeros((RC, LANES), F32)
        for s in range(ROW_SUBLANES):
            hi = lo = None
            for kk in range(K):
                w = buf[slot, kk, pl.ds(c * RC * ROW_SUBLANES + s, RC, stride=ROW_SUBLANES), :]
                h, l = gk[kk] * _unpack_hi(w), gk[kk] * _unpack_lo(w)
                hi, lo = (h, l) if kk == 0 else (hi + h, lo + l)
            his.append(hi)
            los.append(lo)
            sq = sq + hi * hi + lo * lo
        inv = lax.rsqrt(jnp.sum(sq, axis=-1, keepdims=True) * (1.0 / D) + RMS_EPS)
        for s in range(ROW_SUBLANES):
            for col, y in ((s * LANES, his[s]), (half + s * LANES, los[s])):
                cs = slice(col, col + LANES)
                o_ref[rows, cs] = x_ref[rows, cs] + gtf_ref[0, :, cs] * ((y * inv) * gpost_ref[:, cs])

    @pl.when(i == n - 1)
    def _():
        for t in range(1, COMBINE_AHEAD + 1):
            tile_wait((i + t) % COMBINE_SLOTS)


def _combine(dest, yp, gates, x1, gpost, gtf, seq):
    T, D = x1.shape
    tc = 256
    per_b = seq // tc
    return pl.pallas_call(
        _combine_kernel,
        out_shape=jax.ShapeDtypeStruct((T, D), F32),
        grid_spec=pltpu.PrefetchScalarGridSpec(
            num_scalar_prefetch=1,
            grid=(T // tc,),
            in_specs=[pl.BlockSpec(memory_space=pl.ANY),
                      pl.BlockSpec((tc, LANES), lambda i, d: (i, 0)),
                      pl.BlockSpec((tc, D), lambda i, d: (i, 0)),
                      pl.BlockSpec((1, D), lambda i, d: (0, 0)),
                      pl.BlockSpec((1, 1, D), lambda i, d: (i // per_b, 0, 0))],
            out_specs=pl.BlockSpec((tc, D), lambda i, d: (i, 0)),
            scratch_shapes=[pltpu.VMEM((COMBINE_SLOTS, TOPK_EXPERTS, tc * ROW_SUBLANES, LANES), jnp.uint32),
                            pltpu.SemaphoreType.DMA((COMBINE_SLOTS,))],
        ),
        compiler_params=_cparams(("arbitrary",)),
        name="moe_combine",
    )(dest, yp, gates, x1, gpost, gtf)


def _dest_kernel(idx_ref, rank_ref, cnt_ref, dest_ref, *, tm_rows):
    tm = idx_ref.shape[0]
    tiles = jnp.floor((cnt_ref[0:1, :] + float(tm_rows - 1)) * (1.0 / tm_rows))
    r_i = lax.broadcasted_iota(jnp.int32, (LANES, LANES), 0)
    c_i = lax.broadcasted_iota(jnp.int32, (LANES, LANES), 1)
    before = jnp.where(r_i < c_i, 1.0, 0.0).astype(BF16)
    start = jnp.dot(jnp.broadcast_to(tiles, (8, LANES)).astype(BF16), before,
                    preferred_element_type=F32)[0:1] * float(tm_rows)
    lane = lax.broadcasted_iota(jnp.int32, (tm, LANES), 1)
    idx = idx_ref[...]
    rank = rank_ref[...]
    out = jnp.zeros((tm, LANES), jnp.int32)
    for kk in range(TOPK_EXPERTS):
        base = jnp.sum(jnp.where(lane == idx[:, kk:kk + 1], start, 0.0), axis=-1, keepdims=True)
        out = jnp.where(lane == kk, base.astype(jnp.int32) + rank[:, kk:kk + 1], out)
    dest_ref[...] = out


def _dest(idx, rank, cnt, tm_rows):
    T = idx.shape[0]
    tm = 2048
    assert tm_rows & (tm_rows - 1) == 0
    row = pl.BlockSpec((tm, LANES), lambda i: (i, 0))
    return pl.pallas_call(
        functools.partial(_dest_kernel, tm_rows=tm_rows),
        out_shape=jax.ShapeDtypeStruct((T, LANES), jnp.int32),
        grid=(T // tm,),
        in_specs=[row, row, pl.BlockSpec((8, LANES), lambda i: (0, 0))],
        out_specs=row,
        compiler_params=_cparams(("parallel",)),
        name="route_dest",
    )(idx, rank, cnt)


INVERT_UNROLL = 8


def _invert_rows(dest_ref, lo_ref, hi_ref, out_ref):
    def fill(j, carry):
        out_ref[j] = 0
        return carry

    for e in range(lo_ref.shape[0]):
        lax.fori_loop(lo_ref[e], hi_ref[e], fill, 0)

    per = INVERT_UNROLL // TOPK_EXPERTS

    def body(b, carry):
        for u in range(INVERT_UNROLL):
            out_ref[dest_ref[b * INVERT_UNROLL + u]] = b * per + u // TOPK_EXPERTS
        return carry

    lax.fori_loop(0, dest_ref.shape[0] // INVERT_UNROLL, body, 0)


def _routing_tables(idx, rank, cnt, n_experts, tm):
    T = idx.shape[0]
    n_tiles = (T * TOPK_EXPERTS) // tm + n_experts
    counts = cnt[0, :n_experts].astype(jnp.int32)
    padded = ((counts + tm - 1) // tm) * tm
    pad_end = jnp.cumsum(padded)
    tile_start = jnp.arange(n_tiles, dtype=jnp.int32) * tm
    tile_expert = jnp.minimum(jnp.sum((pad_end[None, :] <= tile_start[:, None]).astype(jnp.int32), axis=1),
                              n_experts - 1)
    n_used = pad_end[-1:] // tm
    pad_rows = (pad_end - padded + counts, pad_end)
    dest = _dest(idx, rank, cnt, tm)[:, :TOPK_EXPERTS].reshape(-1)
    return dest, pad_rows, tile_expert, n_used


def kernel(x, c, w_ada, b_ada, g_pre_mix, g_post_mix, g_pre_ffn, g_post_ffn, w_in, w_out, pool_w, pool_scale,
           conv_w, rel_bias, router_w, router_b, w_gate, b_gate, w_up, b_up, w_down, b_down):
    B, S, D = x.shape
    L = w_ada.shape[0]
    T = B * S
    E = router_w.shape[-1]
    attn_w = D // 2
    heads = attn_w // HEAD_DIM
    moe_tm = 256
    assert D == 2 * ROW_SUBLANES * LANES

    mod = _ada(c, w_ada, b_ada)
    w_in_bf = w_in.astype(BF16)
    x2 = x.reshape(T, D)
    for l in range(L):
        sh_m, sc_m, gt_m, sh_f, sc_f, gt_f = [mod[l, :, i * D:(i + 1) * D].reshape(B, 1, D) for i in range(N_MOD)]
        row = lambda v: v.reshape(1, D)

        qkv, rest = _inproj(l, x2, row(g_pre_mix[l]), sc_m, sh_m, w_in_bf, S, attn_w)
        att = _attention(qkv, rel_bias, B, S, heads)

        wo = w_out[l].astype(BF16)
        rw = jnp.zeros((D, LANES), F32).at[:, :E].set(router_w[l])
        rwh = rw.astype(BF16)
        rwl = (rw - rwh.astype(F32)).astype(BF16)
        rb = jnp.full((1, LANES), NEG_INF, F32).at[0, :E].set(router_b[l])
        x1, h2, idx, gates, rank, cnt = _outproj(
            att, rest, pool_w[l], pool_scale[l], conv_w[l], x2, wo[:attn_w], wo[attn_w:],
            row(g_post_mix[l]), gt_m, row(g_pre_ffn[l]), sc_f, sh_f,
            rwh, rwl, rb, S)

        dest, pad_rows, tile_expert, n_used = _routing_tables(idx, rank, cnt, E, moe_tm)
        yp = _moe(l, tile_expert, dest, pad_rows, n_used, h2, w_gate, b_gate, w_up, b_up, w_down, b_down, moe_tm)
        x2 = _combine(dest, yp, gates, x1, row(g_post_ffn[l]), gt_f, S)
    return x2.reshape(B, S, D)
```

```python
import functools
import math

import numpy as np
import jax
import jax.numpy as jnp
from jax import lax
from jax.experimental import pallas as pl
from jax.experimental.pallas import tpu as pltpu

F32 = jnp.float32
BF16 = jnp.bfloat16

LANES = 128
HEAD_DIM = 128
MOBA_BLOCK = 256
MOBA_TOPK = 3
REL_BUCKETS = 32
REL_MAX_DISTANCE = 128
POOL_WINDOWS = (2, 4, 8, 16)
GROUP_DIM = 128
CONV_K = 3
TOPK_EXPERTS = 4
SWIGLU_LIMIT = 7.0
SWIGLU_ALPHA = 1.702
N_MOD = 6
RMS_EPS = 1e-6
NEG_INF = -1e30
ATTN_SCALE = HEAD_DIM ** -0.5
LOG2E = math.log2(math.e)

VMEM_LIMIT = 56 * 1024 * 1024

ADA_TN = 1024
OUTPROJ_TM = 512
MOE_TM = 256
COMBINE_TC = 256
DEST_TM = 2048


def _cparams(sem):
    return pltpu.CompilerParams(dimension_semantics=sem, vmem_limit_bytes=VMEM_LIMIT)


def _rms(x):
    return x * lax.rsqrt(jnp.mean(x * x, axis=-1, keepdims=True) + RMS_EPS)


ROW_SUBLANES = 8


def _bf16_bits(v):
    return lax.bitcast_convert_type(v.astype(BF16).astype(F32), jnp.uint32)


def _pack_words(hi, lo):
    return _bf16_bits(hi) | (_bf16_bits(lo) >> 16)


def _unpack_hi(w):
    return lax.bitcast_convert_type(w & jnp.uint32(0xFFFF0000), F32)


def _unpack_lo(w):
    return lax.bitcast_convert_type(w << 16, F32)


def _ada_kernel(c_ref, w_ref, b_ref, o_ref):
    c = c_ref[...]
    ca = (c * jax.nn.sigmoid(c)).astype(BF16)
    o_ref[0] = jnp.dot(ca, w_ref[0].astype(BF16), preferred_element_type=F32) + b_ref[0]


def _ada(c, w_ada, b_ada):
    L, D, N = w_ada.shape
    B = c.shape[0]
    tn = ADA_TN
    return pl.pallas_call(
        _ada_kernel,
        out_shape=jax.ShapeDtypeStruct((L, B, N), F32),
        grid=(L, N // tn),
        in_specs=[
            pl.BlockSpec((B, D), lambda l, j: (0, 0)),
            pl.BlockSpec((1, D, tn), lambda l, j: (l, 0, j)),
            pl.BlockSpec((1, 1, tn), lambda l, j: (l, 0, j)),
        ],
        out_specs=pl.BlockSpec((1, B, tn), lambda l, j: (l, 0, j)),
        compiler_params=_cparams(("parallel", "parallel")),
        name="ada_mod",
    )(c, w_ada, b_ada.reshape(L, 1, N))


INPROJ_TM = 1024
INPROJ_TN = 1024


def _inproj_kernel(x_ref, g_ref, sc_ref, sh_ref, w_ref, qkv_ref, rest_ref, h_scr, *, n_qkv):
    j = pl.program_id(1)

    @pl.when(j == 0)
    def _():
        h = _rms(x_ref[...]) * g_ref[...]
        h = h * (1.0 + sc_ref[0]) + sh_ref[0]
        h_scr[...] = h.astype(BF16)

    acc = jnp.dot(h_scr[...], w_ref[0], preferred_element_type=F32)

    @pl.when(j < n_qkv)
    def _():
        qkv_ref[...] = acc.astype(BF16)

    @pl.when(j >= n_qkv)
    def _():
        rest_ref[...] = acc


def _inproj(layer, x2, g, sc, sh, w_bf, seq, attn_w):
    T, D = x2.shape
    N = w_bf.shape[2]
    tm, tn = INPROJ_TM, INPROJ_TN
    n_qkv = 3 * attn_w // tn
    n_rest = (N - 3 * attn_w) // tn
    per_b = seq // tm
    return pl.pallas_call(
        functools.partial(_inproj_kernel, n_qkv=n_qkv),
        out_shape=(jax.ShapeDtypeStruct((T, 3 * attn_w), BF16),
                   jax.ShapeDtypeStruct((T, N - 3 * attn_w), F32)),
        grid=(T // tm, n_qkv + n_rest),
        in_specs=[
            pl.BlockSpec((tm, D), lambda i, j: (i, 0)),
            pl.BlockSpec((1, D), lambda i, j: (0, 0)),
            pl.BlockSpec((1, 1, D), lambda i, j: (i // per_b, 0, 0)),
            pl.BlockSpec((1, 1, D), lambda i, j: (i // per_b, 0, 0)),
            pl.BlockSpec((1, D, tn), lambda i, j: (layer, 0, j)),
        ],
        out_specs=(
            pl.BlockSpec((tm, tn), lambda i, j: (i, jnp.minimum(j, n_qkv - 1))),
            pl.BlockSpec((tm, tn), lambda i, j: (i, jnp.maximum(j - n_qkv, 0))),
        ),
        scratch_shapes=[pltpu.VMEM((tm, D), BF16)],
        compiler_params=_cparams(("parallel", "arbitrary")),
        name="mixer_in_proj",
    )(x2, g, sc, sh, w_bf)


def _rel_bucket_np(n):
    n = np.maximum(n, 0)
    max_exact = REL_BUCKETS // 2
    nf = np.maximum(n, max_exact).astype(np.float32)
    large = max_exact + (np.log(nf / np.float32(max_exact)) / np.float32(math.log(REL_MAX_DISTANCE / max_exact))
                         * np.float32(REL_BUCKETS - max_exact)).astype(np.int32)
    large = np.minimum(large, REL_BUCKETS - 1)
    return np.where(n < max_exact, n, large).astype(np.int32)


def _bucket_tables(seq):
    qi = np.arange(MOBA_BLOCK)[:, None]
    ki = np.arange(MOBA_BLOCK)[None, :]
    own = np.where(ki <= qi, _rel_bucket_np(qi - ki), -1)
    prev = _rel_bucket_np(qi - ki + MOBA_BLOCK)
    far = _rel_bucket_np(np.arange(MOBA_BLOCK + 1, max(seq, MOBA_BLOCK + 2)))
    assert np.all(far == far[0])
    return np.stack([own, prev]).astype(np.int32), int(far[0])


ATTN_HEADS_PER_STEP = 2


def _attn_kernel(tab_ref, q_ref, k_ref, v_ref, bkt_ref, o_ref, bias_scr, *, nblk, far_bucket):
    hg = pl.program_id(0)
    b = pl.program_id(1)
    L = MOBA_BLOCK
    G = ATTN_HEADS_PER_STEP

    @pl.when(b == 0)
    def _():
        for g in range(G):
            for m in range(2):
                bk = bkt_ref[m]
                acc = jnp.full((L, L), NEG_INF, F32)
                for r in range(REL_BUCKETS):
                    acc = jnp.where(bk == r, tab_ref[hg * G + g, r] * LOG2E, acc)
                bias_scr[g, m] = acc

    dn = (((1,), (1,)), ((), ()))
    lane = lax.broadcasted_iota(jnp.int32, (L, LANES), 1)
    heads = []
    for g in range(G):
        cols = slice(g * HEAD_DIM, (g + 1) * HEAD_DIM)
        q, k, v = q_ref[:, cols], k_ref[:, cols], v_ref[:, cols]
        kmean = jnp.mean(k.astype(F32).reshape(nblk, L, HEAD_DIM), axis=1)
        kmean = jnp.concatenate([kmean, jnp.zeros((LANES - nblk, HEAD_DIM), F32)], axis=0).astype(BF16)
        gate = lax.dot_general(q, kmean, dn, preferred_element_type=F32)
        heads.append((q, k, v, gate, tab_ref[hg * G + g, far_bucket] * LOG2E))

    for a, g in [(a, g) for a in range(nblk) for g in range(G)]:
        q, k, v, gate, far_bias = heads[g]
        cols = slice(g * HEAD_DIM, (g + 1) * HEAD_DIM)
        qa = q[a * L:(a + 1) * L]
        n = (a + 1) * L
        s = lax.dot_general(qa, k[:n], dn, preferred_element_type=F32) * (ATTN_SCALE * LOG2E)
        hide = None
        if a > MOBA_TOPK:
            ga = jnp.where(lane < a, gate[a * L:(a + 1) * L], NEG_INF)
            cnt = jnp.zeros((L, LANES), F32)
            for i in range(a):
                gi = ga[:, i:i + 1]
                ge = jnp.where(gi >= ga, 1.0, 0.0)
                gt = jnp.where(gi > ga, 1.0, 0.0)
                cnt = cnt + jnp.where(lane > i, ge, gt)
            hide = jnp.where((cnt < float(MOBA_TOPK)) & (lane < a), 0.0, NEG_INF)
            hide_far = hide + far_bias
        pieces = []
        for j in range(a + 1):
            sj = s[:, j * L:(j + 1) * L]
            if j == a:
                sj = sj + bias_scr[g, 0]
            elif j == a - 1:
                sj = sj + bias_scr[g, 1]
                if hide is not None:
                    sj = sj + hide[:, j:j + 1]
            else:
                sj = sj + (far_bias if hide is None else hide_far[:, j:j + 1])
            pieces.append(sj)
        s = pieces[0] if a == 0 else jnp.concatenate(pieces, axis=1)
        m = jnp.max(s, axis=-1, keepdims=True)
        p = jnp.exp2(s - m)
        l = jnp.sum(p, axis=-1, keepdims=True)
        o = jnp.dot(p.astype(BF16), v[:n], preferred_element_type=F32) / l
        o_ref[a * L:(a + 1) * L, cols] = o.astype(o_ref.dtype)


def _attention(qkv, rel_bias, batch, seq, heads):
    T = qkv.shape[0]
    nblk = seq // MOBA_BLOCK
    G = ATTN_HEADS_PER_STEP
    assert heads % G == 0
    ng = heads // G
    bkt, far_bucket = _bucket_tables(seq)
    table = rel_bias.T.astype(F32)
    return pl.pallas_call(
        functools.partial(_attn_kernel, nblk=nblk, far_bucket=far_bucket),
        out_shape=jax.ShapeDtypeStruct((T, heads * HEAD_DIM), BF16),
        grid=(ng, batch),
        in_specs=[
            pl.BlockSpec(memory_space=pltpu.SMEM),
            pl.BlockSpec((seq, G * HEAD_DIM), lambda h, b: (b, h)),
            pl.BlockSpec((seq, G * HEAD_DIM), lambda h, b: (b, ng + h)),
            pl.BlockSpec((seq, G * HEAD_DIM), lambda h, b: (b, 2 * ng + h)),
            pl.BlockSpec((2, MOBA_BLOCK, MOBA_BLOCK), lambda h, b: (0, 0, 0)),
        ],
        out_specs=pl.BlockSpec((seq, G * HEAD_DIM), lambda h, b: (b, h)),
        scratch_shapes=[pltpu.VMEM((G, 2, MOBA_BLOCK, MOBA_BLOCK), F32)],
        compiler_params=_cparams(("arbitrary", "arbitrary")),
        name="moba_attention",
    )(table, qkv, qkv, qkv, jnp.asarray(bkt))


MIX_HALO = 16


def _pool_conv_tile(u_ref, gb_ref, gc_ref, hc_ref, uh_ref, gch_ref, hch_ref, pw_ref, ps_ref, cw_ref, pc_scr,
                    seq_start, t0):
    tm = u_ref.shape[0]
    G = len(POOL_WINDOWS)
    H = MIX_HALO
    pos = lax.broadcasted_iota(jnp.int32, (tm, GROUP_DIM), 0) + t0

    def with_halo(cur_ref, halo_ref, cs):
        return jnp.concatenate([jnp.where(seq_start, 0.0, halo_ref[:, cs]), cur_ref[:, cs]], axis=0)

    def shift(x, k):
        return pltpu.roll(x, k, 0)

    for gi, w in enumerate(POOL_WINDOWS):
        cs = slice(gi * GROUP_DIM, (gi + 1) * GROUP_DIM)
        ug = with_halo(u_ref, uh_ref, cs)
        win = ug
        k = 1
        while k < w:
            win = win + shift(win, k)
            k *= 2
        count = jnp.minimum(pos + 1, w).astype(F32)
        d = win[H:] / count - ug[H:]
        y = jnp.dot(d.astype(BF16), pw_ref[gi].astype(BF16), preferred_element_type=F32)
        pc_scr[:, cs] = (y * ps_ref[:, cs]).astype(pc_scr.dtype)

        uu = with_halo(gc_ref, gch_ref, cs) * with_halo(hc_ref, hch_ref, cs)
        conv = (cw_ref[0:1, cs] * shift(uu, 2) + cw_ref[1:2, cs] * shift(uu, 1)
                + cw_ref[2:3, cs] * uu)[H:]
        oc = slice((G + gi) * GROUP_DIM, (G + gi + 1) * GROUP_DIM)
        pc_scr[:, oc] = (gb_ref[:, cs] * conv).astype(pc_scr.dtype)


def _outproj_kernel(att_ref, u_ref, gb_ref, gc_ref, hc_ref, uh_ref, gch_ref, hch_ref, pw_ref, ps_ref, cw_ref,
                    x_ref, wa_ref, wp_ref, gpost_ref, gtm_ref, gpre_ref, scf_ref, shf_ref,
                    rwh_ref, rwl_ref, rb_ref,
                    x1_ref, h2_ref, idx_ref, gate_ref, rank_ref, cnt_ref, pc_scr, carry_scr, *, per_seq):
    i = pl.program_id(0)
    tm = x_ref.shape[0]

    @pl.when(i == 0)
    def _():
        carry_scr[...] = jnp.zeros_like(carry_scr)

    _pool_conv_tile(u_ref, gb_ref, gc_ref, hc_ref, uh_ref, gch_ref, hch_ref, pw_ref, ps_ref, cw_ref, pc_scr,
                    seq_start=i % per_seq == 0, t0=(i % per_seq) * tm)
    mix = (jnp.dot(att_ref[...], wa_ref[...], preferred_element_type=F32)
           + jnp.dot(pc_scr[...], wp_ref[...], preferred_element_type=F32))
    x1 = x_ref[...] + gtm_ref[0] * (_rms(mix) * gpost_ref[...])
    x1_ref[...] = x1
    h2 = (_rms(x1) * gpre_ref[...]) * (1.0 + scf_ref[0]) + shf_ref[0]
    half = h2.shape[1] // 2
    for s in range(ROW_SUBLANES):
        h2_ref[pl.ds(s, tm, stride=ROW_SUBLANES), :] = _pack_words(h2[:, s * LANES:(s + 1) * LANES],
                                      h2[:, half + s * LANES:half + (s + 1) * LANES])

    hi = h2.astype(BF16)
    lo = (h2 - hi.astype(F32)).astype(BF16)
    r2 = jnp.dot(hi, jnp.concatenate([rwh_ref[...], rwl_ref[...]], axis=1), preferred_element_type=F32)
    logits = (r2[:, :LANES] + r2[:, LANES:]
              + jnp.dot(lo, rwh_ref[...], preferred_element_type=F32)) + rb_ref[...]

    lane = lax.broadcasted_iota(jnp.int32, (tm, LANES), 1)
    work = logits
    mem = jnp.zeros((tm, LANES), F32)
    vals, hots = [], []
    idx_out = jnp.zeros((tm, LANES), jnp.int32)
    for kk in range(TOPK_EXPERTS):
        m = jnp.max(work, axis=-1, keepdims=True)
        ik = jnp.min(jnp.where(work == m, lane, LANES), axis=-1, keepdims=True)
        hot = lane == ik
        vals.append(m)
        hots.append(hot)
        idx_out = jnp.where(lane == kk, ik, idx_out)
        mem = jnp.where(hot, 1.0, mem)
        work = jnp.where(hot, -jnp.inf, work)
    idx_ref[...] = idx_out

    es = [jnp.exp(vk - vals[0]) for vk in vals]
    denom = es[0]
    for e in es[1:]:
        denom = denom + e
    gates = jnp.zeros((tm, LANES), F32)
    for kk in range(TOPK_EXPERTS):
        gates = jnp.where(lane == kk, es[kk] / denom, gates)
    gate_ref[...] = gates

    r_i = lax.broadcasted_iota(jnp.int32, (tm, tm), 0)
    c_i = lax.broadcasted_iota(jnp.int32, (tm, tm), 1)
    tri = jnp.where(c_i < r_i, 1.0, 0.0).astype(BF16)
    before = jnp.dot(tri, mem.astype(BF16), preferred_element_type=F32) + carry_scr[0:1, :]
    ranks = jnp.zeros((tm, LANES), jnp.int32)
    for kk in range(TOPK_EXPERTS):
        rk = jnp.sum(jnp.where(hots[kk], before, 0.0), axis=-1, keepdims=True)
        ranks = jnp.where(lane == kk, rk.astype(jnp.int32), ranks)
    rank_ref[...] = ranks
    carry = carry_scr[...] + jnp.sum(mem, axis=0, keepdims=True)
    carry_scr[...] = carry
    cnt_ref[...] = carry


def _outproj(att, rest, pool_w, pool_scale, conv_w, x2, wa, wp, gpost, gtm, gpre, scf, shf, rwh, rwl, rb, seq):
    T, D = x2.shape
    tm = OUTPROJ_TM
    per_b = seq // tm
    W = len(POOL_WINDOWS) * GROUP_DIM
    row = lambda w: pl.BlockSpec((tm, w), lambda i: (i, 0))
    cur = lambda c: pl.BlockSpec((tm, W), lambda i: (i, c))
    halo = lambda c: pl.BlockSpec((MIX_HALO, W), lambda i: (jnp.maximum(i * (tm // MIX_HALO) - 1, 0), c))
    const = lambda shape: pl.BlockSpec(shape, lambda i: tuple(0 for _ in shape), pipeline_mode=pl.Buffered(1))
    perb = pl.BlockSpec((1, 1, D), lambda i: (i // per_b, 0, 0))
    return pl.pallas_call(
        functools.partial(_outproj_kernel, per_seq=per_b),
        out_shape=(jax.ShapeDtypeStruct((T, D), F32), jax.ShapeDtypeStruct((T * ROW_SUBLANES, LANES), jnp.uint32),
                   jax.ShapeDtypeStruct((T, LANES), jnp.int32), jax.ShapeDtypeStruct((T, LANES), F32),
                   jax.ShapeDtypeStruct((T, LANES), jnp.int32), jax.ShapeDtypeStruct((8, LANES), F32)),
        grid=(T // tm,),
        in_specs=[row(att.shape[1]), cur(0), cur(1), cur(2), cur(3), halo(0), halo(2), halo(3),
                  const(pool_w.shape), const((1, W)), const((CONV_K, W)),
                  row(D), const(wa.shape), const(wp.shape),
                  const((1, D)), perb, const((1, D)), perb, perb,
                  const(rwh.shape), const(rwl.shape), const((1, LANES))],
        out_specs=(row(D), pl.BlockSpec((tm * ROW_SUBLANES, LANES), lambda i: (i, 0)),
                   row(LANES), row(LANES), row(LANES), pl.BlockSpec((8, LANES), lambda i: (0, 0))),
        scratch_shapes=[pltpu.VMEM((tm, 2 * W), BF16), pltpu.VMEM((8, LANES), F32)],
        compiler_params=_cparams(("arbitrary",)),
        name="out_proj_router",
    )(att, rest, rest, rest, rest, rest, rest, rest, pool_w, pool_scale.reshape(1, W), conv_w,
      x2, wa, wp, gpost, gtm, gpre, scf, shf, rwh, rwl, rb)


MOE_F_CHUNK = 256
MOE_D_CHUNK = 512
MOE_AHEAD = 2
MOE_SLOTS = MOE_AHEAD + 1


MOE_W_PHASES = 6


def _moe_kernel(te_ref, dest_ref, nu_ref, par_ref, pos_ref, nxt_ref, pn_ref, lo_ref, hi_ref,
                h_hbm, wg_hbm, bg_ref, wu_hbm, bu_ref, wd_hbm, bd_ref, o_ref,
                xbuf, act_scr, hi_scr, wg_bf, wu_bf, wd_bf, stg_a, stg_b, rt_ref, sem, wsem, *, layer):
    i = pl.program_id(0)
    tm = act_scr.shape[0]
    Fe = act_scr.shape[1]
    D = wd_bf.shape[-1]
    half = D // 2
    n_used = nu_ref[0]
    last = n_used - 1

    def piece(ph, expert):
        if ph < 4:
            src_ref, dst = (wg_hbm, stg_a) if ph < 2 else (wu_hbm, stg_a)
            rows = D // 2
        else:
            src_ref, dst, rows = wd_hbm, stg_b, Fe // 2
        r0 = (ph % 2) * rows
        return pltpu.make_async_copy(src_ref.at[layer, expert, pl.ds(r0, rows), :], dst, wsem)

    def round_piece(ph, slot):
        if ph < 4:
            rows = D // 2
            dst = wg_bf if ph < 2 else wu_bf
            dst[slot, (ph % 2) * rows:(ph % 2 + 1) * rows, :] = stg_a[...].astype(BF16)
        else:
            rows = Fe // 2
            wd_bf[slot, (ph % 2) * rows:(ph % 2 + 1) * rows, :] = stg_b[...].astype(BF16)

    def row_copy(tile_id, r, slot):
        tok8 = rt_ref[tile_id * tm + r]
        src = h_hbm.at[pl.ds(pl.multiple_of(tok8, ROW_SUBLANES), ROW_SUBLANES)]
        return pltpu.make_async_copy(src, xbuf.at[slot, pl.ds(r * ROW_SUBLANES, ROW_SUBLANES)], sem.at[slot])

    def tile_wait(slot):
        pltpu.make_async_copy(h_hbm.at[pl.ds(0, tm * ROW_SUBLANES)], xbuf.at[slot], sem.at[slot]).wait()

    @pl.when(i == 0)
    def _():
        _invert_rows(dest_ref, lo_ref, hi_ref, rt_ref)
        for t in range(MOE_AHEAD):
            def issue(r, carry, t=t):
                row_copy(jnp.minimum(t, last), r, t).start()
                return carry
            lax.fori_loop(0, tm, issue, 0)

    @pl.when(i < n_used)
    def _():
        slot = i % MOE_SLOTS
        nslot = (i + MOE_AHEAD) % MOE_SLOTS
        ntile = jnp.minimum(i + MOE_AHEAD, last)
        n_f, n_d = Fe // MOE_F_CHUNK, D // MOE_D_CHUNK
        per = tm // (n_f + n_d)

        def issue_part(p):
            for r in range(p * per, (p + 1) * per):
                row_copy(ntile, r, nslot).start()

        expert = te_ref[i]
        wslot = par_ref[i]
        k = pos_ref[i]
        nxt = nxt_ref[i]

        @pl.when(i == 0)
        def _():
            for ph in range(MOE_W_PHASES):
                piece(ph, expert).start()
                piece(ph, expert).wait()
                round_piece(ph, wslot)

        @pl.when((k == 0) & (i > 0))
        def _():
            done = jnp.minimum(pn_ref[i] - 1, MOE_W_PHASES)
            for ph in range(MOE_W_PHASES):
                @pl.when(ph >= done)
                def _(ph=ph):
                    @pl.when(ph > done)
                    def _():
                        piece(ph, expert).start()
                    piece(ph, expert).wait()
                    round_piece(ph, wslot)

        for ph in range(MOE_W_PHASES):
            @pl.when((k == ph + 1) & (nxt >= 0))
            def _(ph=ph):
                piece(ph, nxt).wait()
                round_piece(ph, 1 - wslot)

        for ph in range(MOE_W_PHASES):
            @pl.when((k == ph) & (nxt >= 0))
            def _(ph=ph):
                piece(ph, nxt).start()

        tile_wait(slot)
        words = [xbuf[slot, pl.ds(s, tm, stride=ROW_SUBLANES), :] for s in range(ROW_SUBLANES)]
        x = jnp.concatenate([_unpack_hi(w).astype(BF16) for w in words]
                            + [_unpack_lo(w).astype(BF16) for w in words], axis=1)
        for c in range(n_f):
            issue_part(c)
            cs = slice(c * MOE_F_CHUNK, (c + 1) * MOE_F_CHUNK)
            g = jnp.dot(x, wg_bf[wslot, :, cs], preferred_element_type=F32) + bg_ref[0, expert, :, cs]
            u = jnp.dot(x, wu_bf[wslot, :, cs], preferred_element_type=F32) + bu_ref[0, expert, :, cs]
            g = jnp.minimum(g, SWIGLU_LIMIT)
            u = jnp.clip(u, -SWIGLU_LIMIT, SWIGLU_LIMIT)
            act_scr[:, cs] = (g * jax.nn.sigmoid(SWIGLU_ALPHA * g) * (u + 1.0)).astype(BF16)
        act = act_scr[...]
        for c in range(n_d):
            issue_part(n_f + c)
            c0 = c * MOE_D_CHUNK
            y = (jnp.dot(act, wd_bf[wslot, :, c0:c0 + MOE_D_CHUNK], preferred_element_type=F32)
                 + bd_ref[0, expert, :, c0:c0 + MOE_D_CHUNK])
            for j in range(MOE_D_CHUNK // LANES):
                col = c0 + j * LANES
                yj = y[:, j * LANES:(j + 1) * LANES]
                if col < half:
                    hi_scr[:, col:col + LANES] = _bf16_bits(yj)
                else:
                    s = (col - half) // LANES
                    o_ref[pl.ds(s, tm, stride=ROW_SUBLANES), :] = hi_scr[:, col - half:col - half + LANES] | (_bf16_bits(yj) >> 16)

        @pl.when(i == last)
        def _():
            for t in range(1, MOE_AHEAD + 1):
                tile_wait((i + t) % MOE_SLOTS)

    @pl.when(i >= n_used)
    def _():
        o_ref[...] = jnp.zeros_like(o_ref)


def _group_tables(tile_expert, n_used):
    n = tile_expert.shape[0]
    idx = jnp.arange(n, dtype=jnp.int32)
    first = jnp.concatenate([jnp.ones((1,), bool), tile_expert[1:] != tile_expert[:-1]])
    par = (jnp.cumsum(first.astype(jnp.int32)) - 1) % 2
    pos = idx - lax.cummax(jnp.where(first, idx, 0))
    later = (idx[None, :] > idx[:, None]) & (idx[None, :] < n_used[0]) & (tile_expert[None, :] != tile_expert[:, None])
    nxt_idx = jnp.min(jnp.where(later, idx[None, :], n), axis=1)
    nxt = jnp.where(nxt_idx < n, tile_expert[jnp.minimum(nxt_idx, n - 1)], -1)
    pn = jnp.concatenate([jnp.zeros((1,), jnp.int32), pos[:-1] + 1])
    return par.astype(jnp.int32), pos.astype(jnp.int32), nxt.astype(jnp.int32), pn.astype(jnp.int32)


def _moe(layer, tile_expert, dest, pad_rows, n_used, h2p, wg, bg, wu, bu, wd, bd, tm):
    L, E, D, Fe = wg.shape
    n_tiles = tile_expert.shape[0]
    lo, hi = pad_rows
    assert tm % (Fe // MOE_F_CHUNK + D // MOE_D_CHUNK) == 0 and (D // 2) % MOE_D_CHUNK == 0
    par, pos, nxt, pn = _group_tables(tile_expert, n_used)
    bspec = lambda w: pl.BlockSpec((1, E, 1, w), lambda i, *_: (layer, 0, 0, 0), pipeline_mode=pl.Buffered(1))
    hbm = pl.BlockSpec(memory_space=pl.ANY)
    return pl.pallas_call(
        functools.partial(_moe_kernel, layer=layer),
        out_shape=jax.ShapeDtypeStruct((n_tiles * tm * ROW_SUBLANES, LANES), jnp.uint32),
        grid_spec=pltpu.PrefetchScalarGridSpec(
            num_scalar_prefetch=9,
            grid=(n_tiles,),
            in_specs=[hbm, hbm, bspec(Fe), hbm, bspec(Fe), hbm, bspec(D)],
            out_specs=pl.BlockSpec((tm * ROW_SUBLANES, LANES), lambda i, *_: (i, 0)),
            scratch_shapes=[pltpu.VMEM((MOE_SLOTS, tm * ROW_SUBLANES, LANES), jnp.uint32),
                            pltpu.VMEM((tm, Fe), BF16), pltpu.VMEM((tm, D // 2), jnp.uint32),
                            pltpu.VMEM((2, D, Fe), BF16), pltpu.VMEM((2, D, Fe), BF16),
                            pltpu.VMEM((2, Fe, D), BF16),
                            pltpu.VMEM((D // 2, Fe), F32), pltpu.VMEM((Fe // 2, D), F32),
                            pltpu.SMEM((n_tiles * tm,), jnp.int32),
                            pltpu.SemaphoreType.DMA((MOE_SLOTS,)), pltpu.SemaphoreType.DMA],
        ),
        compiler_params=_cparams(("arbitrary",)),
        name="moe_experts",
    )(tile_expert, dest, n_used, par, pos, nxt, pn, lo, hi, h2p, wg, bg.reshape(L, E, 1, Fe),
      wu, bu.reshape(L, E, 1, Fe), wd, bd.reshape(L, E, 1, D))


COMBINE_AHEAD = 2
COMBINE_SLOTS = COMBINE_AHEAD + 1
COMBINE_ROW_CHUNK = 8


def _combine_kernel(dest_ref, yp_hbm, gate_ref, x_ref, gpost_ref, gtf_ref, o_ref, buf, sem):
    i = pl.program_id(0)
    n = pl.num_programs(0)
    tc, D = x_ref.shape
    half = D // 2
    K = TOPK_EXPERTS

    def row_copies(tile_id, r, slot):
        base = tile_id * tc * K
        for kk in range(K):
            d = dest_ref[base + r * K + kk]
            src = yp_hbm.at[pl.ds(pl.multiple_of(d, ROW_SUBLANES), ROW_SUBLANES)]
            dst = buf.at[slot, kk, pl.ds(r * ROW_SUBLANES, ROW_SUBLANES)]
            pltpu.make_async_copy(src, dst, sem.at[slot]).start(priority=kk % 2)

    def tile_wait(slot):
        for kk in range(K):
            pltpu.make_async_copy(yp_hbm.at[pl.ds(0, tc * ROW_SUBLANES)], buf.at[slot, kk], sem.at[slot]).wait()

    @pl.when(i == 0)
    def _():
        for t in range(COMBINE_AHEAD):
            def issue(r, carry, t=t):
                row_copies(jnp.minimum(t, n - 1), r, t)
                return carry
            lax.fori_loop(0, tc, issue, 0)

    slot = i % COMBINE_SLOTS
    nslot = (i + COMBINE_AHEAD) % COMBINE_SLOTS
    ntile = jnp.minimum(i + COMBINE_AHEAD, n - 1)
    tile_wait(slot)
    RC = COMBINE_ROW_CHUNK
    for c in range(tc // RC):
        for r in range(c * RC, (c + 1) * RC):
            row_copies(ntile, r, nslot)
        rows = slice(c * RC, (c + 1) * RC)
        gates = gate_ref[rows, :]
        gk = [jnp.broadcast_to(gates[:, kk:kk + 1], (RC, LANES)) for kk in range(K)]
        his, los = [], []
        sq = jnp.zeros((RC, LANES), F32)
        for s in range(ROW_SUBLANES):
            hi = lo = None
            for kk in range(K):
                w = buf[slot, kk, pl.ds(c * RC * ROW_SUBLANES + s, RC, stride=ROW_SUBLANES), :]
                h, l = gk[kk] * _unpack_hi(w), gk[kk] * _unpack_lo(w)
                hi, lo = (h, l) if kk == 0 else (hi + h, lo + l)
            his.append(hi)
            los.append(lo)
            sq = sq + hi * hi + lo * lo
        inv = lax.rsqrt(jnp.sum(sq, axis=-1, keepdims=True) * (1.0 / D) + RMS_EPS)
        for s in range(ROW_SUBLANES):
            for col, y in ((s * LANES, his[s]), (half + s * LANES, los[s])):
                cs = slice(col, col + LANES)
                o_ref[rows, cs] = x_ref[rows, cs] + gtf_ref[0, :, cs] * ((y * inv) * gpost_ref[:, cs])

    @pl.when(i == n - 1)
    def _():
        for t in range(1, COMBINE_AHEAD + 1):
            tile_wait((i + t) % COMBINE_SLOTS)


def _combine(dest, yp, gates, x1, gpost, gtf, seq):
    T, D = x1.shape
    tc = COMBINE_TC
    per_b = seq // tc
    return pl.pallas_call(
        _combine_kernel,
        out_shape=jax.ShapeDtypeStruct((T, D), F32),
        grid_spec=pltpu.PrefetchScalarGridSpec(
            num_scalar_prefetch=1,
            grid=(T // tc,),
            in_specs=[pl.BlockSpec(memory_space=pl.ANY),
                      pl.BlockSpec((tc, LANES), lambda i, d: (i, 0)),
                      pl.BlockSpec((tc, D), lambda i, d: (i, 0)),
                      pl.BlockSpec((1, D), lambda i, d: (0, 0)),
                      pl.BlockSpec((1, 1, D), lambda i, d: (i // per_b, 0, 0))],
            out_specs=pl.BlockSpec((tc, D), lambda i, d: (i, 0)),
            scratch_shapes=[pltpu.VMEM((COMBINE_SLOTS, TOPK_EXPERTS, tc * ROW_SUBLANES, LANES), jnp.uint32),
                            pltpu.SemaphoreType.DMA((COMBINE_SLOTS,))],
        ),
        compiler_params=_cparams(("arbitrary",)),
        name="moe_combine",
    )(dest, yp, gates, x1, gpost, gtf)


def _dest_kernel(idx_ref, rank_ref, cnt_ref, dest_ref, *, tm_rows):
    tm = idx_ref.shape[0]
    tiles = jnp.floor((cnt_ref[0:1, :] + float(tm_rows - 1)) * (1.0 / tm_rows))
    r_i = lax.broadcasted_iota(jnp.int32, (LANES, LANES), 0)
    c_i = lax.broadcasted_iota(jnp.int32, (LANES, LANES), 1)
    before = jnp.where(r_i < c_i, 1.0, 0.0).astype(BF16)
    start = jnp.dot(jnp.broadcast_to(tiles, (8, LANES)).astype(BF16), before,
                    preferred_element_type=F32)[0:1] * float(tm_rows)
    lane = lax.broadcasted_iota(jnp.int32, (tm, LANES), 1)
    idx = idx_ref[...]
    rank = rank_ref[...]
    out = jnp.zeros((tm, LANES), jnp.int32)
    for kk in range(TOPK_EXPERTS):
        base = jnp.sum(jnp.where(lane == idx[:, kk:kk + 1], start, 0.0), axis=-1, keepdims=True)
        out = jnp.where(lane == kk, base.astype(jnp.int32) + rank[:, kk:kk + 1], out)
    dest_ref[...] = out


def _dest(idx, rank, cnt, tm_rows):
    T = idx.shape[0]
    tm = DEST_TM
    assert tm_rows & (tm_rows - 1) == 0
    row = pl.BlockSpec((tm, LANES), lambda i: (i, 0))
    return pl.pallas_call(
        functools.partial(_dest_kernel, tm_rows=tm_rows),
        out_shape=jax.ShapeDtypeStruct((T, LANES), jnp.int32),
        grid=(T // tm,),
        in_specs=[row, row, pl.BlockSpec((8, LANES), lambda i: (0, 0))],
        out_specs=row,
        compiler_params=_cparams(("parallel",)),
        name="route_dest",
    )(idx, rank, cnt)


INVERT_UNROLL = 8


def _invert_rows(dest_ref, lo_ref, hi_ref, out_ref):
    def fill(j, carry):
        out_ref[j] = 0
        return carry

    for e in range(lo_ref.shape[0]):
        lax.fori_loop(lo_ref[e], hi_ref[e], fill, 0)

    per = INVERT_UNROLL // TOPK_EXPERTS

    def body(b, carry):
        for u in range(INVERT_UNROLL):
            out_ref[dest_ref[b * INVERT_UNROLL + u]] = (b * per + u // TOPK_EXPERTS) * ROW_SUBLANES
        return carry

    lax.fori_loop(0, dest_ref.shape[0] // INVERT_UNROLL, body, 0)


def _routing_tables(idx, rank, cnt, n_experts, tm):
    T = idx.shape[0]
    n_tiles = (T * TOPK_EXPERTS) // tm + n_experts
    counts = cnt[0, :n_experts].astype(jnp.int32)
    padded = ((counts + tm - 1) // tm) * tm
    pad_end = jnp.cumsum(padded)
    tile_start = jnp.arange(n_tiles, dtype=jnp.int32) * tm
    tile_expert = jnp.minimum(jnp.sum((pad_end[None, :] <= tile_start[:, None]).astype(jnp.int32), axis=1),
                              n_experts - 1)
    n_used = pad_end[-1:] // tm
    pad_rows = (pad_end - padded + counts, pad_end)
    dest = _dest(idx, rank, cnt, tm)[:, :TOPK_EXPERTS].reshape(-1)
    return dest, pad_rows, tile_expert, n_used


def kernel(x, c, w_ada, b_ada, g_pre_mix, g_post_mix, g_pre_ffn, g_post_ffn, w_in, w_out, pool_w, pool_scale,
           conv_w, rel_bias, router_w, router_b, w_gate, b_gate, w_up, b_up, w_down, b_down):
    B, S, D = x.shape
    L = w_ada.shape[0]
    T = B * S
    E = router_w.shape[-1]
    attn_w = D // 2
    heads = attn_w // HEAD_DIM
    moe_tm = MOE_TM
    assert D == 2 * ROW_SUBLANES * LANES

    mod = _ada(c, w_ada, b_ada)
    w_in_bf = w_in.astype(BF16)
    x2 = x.reshape(T, D)
    for l in range(L):
        sh_m, sc_m, gt_m, sh_f, sc_f, gt_f = [mod[l, :, i * D:(i + 1) * D].reshape(B, 1, D) for i in range(N_MOD)]
        row = lambda v: v.reshape(1, D)

        qkv, rest = _inproj(l, x2, row(g_pre_mix[l]), sc_m, sh_m, w_in_bf, S, attn_w)
        att = _attention(qkv, rel_bias, B, S, heads)

        wo = w_out[l].astype(BF16)
        rw = jnp.zeros((D, LANES), F32).at[:, :E].set(router_w[l])
        rwh = rw.astype(BF16)
        rwl = (rw - rwh.astype(F32)).astype(BF16)
        rb = jnp.full((1, LANES), NEG_INF, F32).at[0, :E].set(router_b[l])
        x1, h2, idx, gates, rank, cnt = _outproj(
            att, rest, pool_w[l], pool_scale[l], conv_w[l], x2, wo[:attn_w], wo[attn_w:],
            row(g_post_mix[l]), gt_m, row(g_pre_ffn[l]), sc_f, sh_f,
            rwh, rwl, rb, S)

        dest, pad_rows, tile_expert, n_used = _routing_tables(idx, rank, cnt, E, moe_tm)
        yp = _moe(l, tile_expert, dest, pad_rows, n_used, h2, w_gate, b_gate, w_up, b_up, w_down, b_down, moe_tm)
        x2 = _combine(dest * ROW_SUBLANES, yp, gates, x1, row(g_post_ffn[l]), gt_f, S)
    return x2.reshape(B, S, D)
```

```python
import functools
import math

import numpy as np
import jax
import jax.numpy as jnp
from jax import lax
from jax.experimental import pallas as pl
from jax.experimental.pallas import tpu as pltpu

F32 = jnp.float32
BF16 = jnp.bfloat16

LANES = 128
HEAD_DIM = 128
MOBA_BLOCK = 256
MOBA_TOPK = 3
REL_BUCKETS = 32
REL_MAX_DISTANCE = 128
POOL_WINDOWS = (2, 4, 8, 16)
GROUP_DIM = 128
CONV_K = 3
TOPK_EXPERTS = 4
SWIGLU_LIMIT = 7.0
SWIGLU_ALPHA = 1.702
N_MOD = 6
RMS_EPS = 1e-6
NEG_INF = -1e30
ATTN_SCALE = HEAD_DIM ** -0.5
LOG2E = math.log2(math.e)

VMEM_LIMIT = 56 * 1024 * 1024

ADA_TN = 1024
OUTPROJ_TM = 512
MOE_TM = 256
COMBINE_TC = 256
DEST_TM = 2048


def _cparams(sem):
    return pltpu.CompilerParams(dimension_semantics=sem, vmem_limit_bytes=VMEM_LIMIT)


def _rms(x):
    return x * lax.rsqrt(jnp.mean(x * x, axis=-1, keepdims=True) + RMS_EPS)


ROW_SUBLANES = 8


def _bf16_bits(v):
    return lax.bitcast_convert_type(v.astype(BF16).astype(F32), jnp.uint32)


def _pack_words(hi, lo):
    return _bf16_bits(hi) | (_bf16_bits(lo) >> 16)


def _unpack_hi(w):
    return lax.bitcast_convert_type(w & jnp.uint32(0xFFFF0000), F32)


def _unpack_lo(w):
    return lax.bitcast_convert_type(w << 16, F32)


def _ada_kernel(c_ref, w_ref, b_ref, o_ref):
    c = c_ref[...]
    ca = (c * jax.nn.sigmoid(c)).astype(BF16)
    o_ref[0] = jnp.dot(ca, w_ref[0].astype(BF16), preferred_element_type=F32) + b_ref[0]


def _ada(c, w_ada, b_ada):
    L, D, N = w_ada.shape
    B = c.shape[0]
    tn = ADA_TN
    return pl.pallas_call(
        _ada_kernel,
        out_shape=jax.ShapeDtypeStruct((L, B, N), F32),
        grid=(L, N // tn),
        in_specs=[
            pl.BlockSpec((B, D), lambda l, j: (0, 0)),
            pl.BlockSpec((1, D, tn), lambda l, j: (l, 0, j)),
            pl.BlockSpec((1, 1, tn), lambda l, j: (l, 0, j)),
        ],
        out_specs=pl.BlockSpec((1, B, tn), lambda l, j: (l, 0, j)),
        compiler_params=_cparams(("parallel", "parallel")),
        name="ada_mod",
    )(c, w_ada, b_ada.reshape(L, 1, N))


INPROJ_TM = 1024
INPROJ_TN = 1024


def _inproj_kernel(x_ref, g_ref, sc_ref, sh_ref, w_ref, qkv_ref, rest_ref, h_scr, *, n_qkv):
    j = pl.program_id(1)

    @pl.when(j == 0)
    def _():
        h = _rms(x_ref[...]) * g_ref[...]
        h = h * (1.0 + sc_ref[0]) + sh_ref[0]
        h_scr[...] = h.astype(BF16)

    acc = jnp.dot(h_scr[...], w_ref[0], preferred_element_type=F32)

    @pl.when(j < n_qkv)
    def _():
        qkv_ref[...] = acc.astype(BF16)

    @pl.when(j >= n_qkv)
    def _():
        rest_ref[...] = acc


def _inproj(layer, x2, g, sc, sh, w_bf, seq, attn_w):
    T, D = x2.shape
    N = w_bf.shape[2]
    tm, tn = INPROJ_TM, INPROJ_TN
    n_qkv = 3 * attn_w // tn
    n_rest = (N - 3 * attn_w) // tn
    per_b = seq // tm
    return pl.pallas_call(
        functools.partial(_inproj_kernel, n_qkv=n_qkv),
        out_shape=(jax.ShapeDtypeStruct((T, 3 * attn_w), BF16),
                   jax.ShapeDtypeStruct((T, N - 3 * attn_w), F32)),
        grid=(T // tm, n_qkv + n_rest),
        in_specs=[
            pl.BlockSpec((tm, D), lambda i, j: (i, 0)),
            pl.BlockSpec((1, D), lambda i, j: (0, 0)),
            pl.BlockSpec((1, 1, D), lambda i, j: (i // per_b, 0, 0)),
            pl.BlockSpec((1, 1, D), lambda i, j: (i // per_b, 0, 0)),
            pl.BlockSpec((1, D, tn), lambda i, j: (layer, 0, j)),
        ],
        out_specs=(
            pl.BlockSpec((tm, tn), lambda i, j: (i, jnp.minimum(j, n_qkv - 1))),
            pl.BlockSpec((tm, tn), lambda i, j: (i, jnp.maximum(j - n_qkv, 0))),
        ),
        scratch_shapes=[pltpu.VMEM((tm, D), BF16)],
        compiler_params=_cparams(("parallel", "arbitrary")),
        name="mixer_in_proj",
    )(x2, g, sc, sh, w_bf)


def _rel_bucket_np(n):
    n = np.maximum(n, 0)
    max_exact = REL_BUCKETS // 2
    nf = np.maximum(n, max_exact).astype(np.float32)
    large = max_exact + (np.log(nf / np.float32(max_exact)) / np.float32(math.log(REL_MAX_DISTANCE / max_exact))
                         * np.float32(REL_BUCKETS - max_exact)).astype(np.int32)
    large = np.minimum(large, REL_BUCKETS - 1)
    return np.where(n < max_exact, n, large).astype(np.int32)


def _bucket_tables(seq):
    qi = np.arange(MOBA_BLOCK)[:, None]
    ki = np.arange(MOBA_BLOCK)[None, :]
    own = np.where(ki <= qi, _rel_bucket_np(qi - ki), -1)
    prev = _rel_bucket_np(qi - ki + MOBA_BLOCK)
    far = _rel_bucket_np(np.arange(MOBA_BLOCK + 1, max(seq, MOBA_BLOCK + 2)))
    assert np.all(far == far[0])
    return np.stack([own, prev]).astype(np.int32), int(far[0])


ATTN_HEADS_PER_STEP = 2


def _attn_kernel(tab_ref, q_ref, k_ref, v_ref, bkt_ref, o_ref, bias_scr, *, nblk, far_bucket):
    hg = pl.program_id(0)
    b = pl.program_id(1)
    L = MOBA_BLOCK
    G = ATTN_HEADS_PER_STEP

    @pl.when(b == 0)
    def _():
        for g in range(G):
            for m in range(2):
                bk = bkt_ref[m]
                acc = jnp.full((L, L), NEG_INF, F32)
                for r in range(REL_BUCKETS):
                    acc = jnp.where(bk == r, tab_ref[hg * G + g, r] * LOG2E, acc)
                bias_scr[g, m] = acc

    dn = (((1,), (1,)), ((), ()))
    lane = lax.broadcasted_iota(jnp.int32, (L, LANES), 1)
    heads = []
    for g in range(G):
        cols = slice(g * HEAD_DIM, (g + 1) * HEAD_DIM)
        q, k, v = q_ref[:, cols], k_ref[:, cols], v_ref[:, cols]
        kmean = jnp.mean(k.astype(F32).reshape(nblk, L, HEAD_DIM), axis=1)
        kmean = jnp.concatenate([kmean, jnp.zeros((LANES - nblk, HEAD_DIM), F32)], axis=0).astype(BF16)
        gate = lax.dot_general(q, kmean, dn, preferred_element_type=F32)
        heads.append((q, k, v, gate, tab_ref[hg * G + g, far_bucket] * LOG2E))

    for a, g in [(a, g) for a in range(nblk) for g in range(G)]:
        q, k, v, gate, far_bias = heads[g]
        cols = slice(g * HEAD_DIM, (g + 1) * HEAD_DIM)
        qa = q[a * L:(a + 1) * L]
        n = (a + 1) * L
        s = lax.dot_general(qa, k[:n], dn, preferred_element_type=F32) * (ATTN_SCALE * LOG2E)
        hide = None
        if a > MOBA_TOPK:
            ga = jnp.where(lane < a, gate[a * L:(a + 1) * L], NEG_INF)
            cnt = jnp.zeros((L, LANES), F32)
            for i in range(a):
                gi = ga[:, i:i + 1]
                ge = jnp.where(gi >= ga, 1.0, 0.0)
                gt = jnp.where(gi > ga, 1.0, 0.0)
                cnt = cnt + jnp.where(lane > i, ge, gt)
            hide = jnp.where((cnt < float(MOBA_TOPK)) & (lane < a), 0.0, NEG_INF)
            hide_far = hide + far_bias
        pieces = []
        for j in range(a + 1):
            sj = s[:, j * L:(j + 1) * L]
            if j == a:
                sj = sj + bias_scr[g, 0]
            elif j == a - 1:
                sj = sj + bias_scr[g, 1]
                if hide is not None:
                    sj = sj + hide[:, j:j + 1]
            else:
                sj = sj + (far_bias if hide is None else hide_far[:, j:j + 1])
            pieces.append(sj)
        s = pieces[0] if a == 0 else jnp.concatenate(pieces, axis=1)
        m = jnp.max(s, axis=-1, keepdims=True)
        p = jnp.exp2(s - m)
        l = jnp.sum(p, axis=-1, keepdims=True)
        o = jnp.dot(p.astype(BF16), v[:n], preferred_element_type=F32) / l
        o_ref[a * L:(a + 1) * L, cols] = o.astype(o_ref.dtype)


def _attention(qkv, rel_bias, batch, seq, heads):
    T = qkv.shape[0]
    nblk = seq // MOBA_BLOCK
    G = ATTN_HEADS_PER_STEP
    assert heads % G == 0
    ng = heads // G
    bkt, far_bucket = _bucket_tables(seq)
    table = rel_bias.T.astype(F32)
    return pl.pallas_call(
        functools.partial(_attn_kernel, nblk=nblk, far_bucket=far_bucket),
        out_shape=jax.ShapeDtypeStruct((T, heads * HEAD_DIM), BF16),
        grid=(ng, batch),
        in_specs=[
            pl.BlockSpec(memory_space=pltpu.SMEM),
            pl.BlockSpec((seq, G * HEAD_DIM), lambda h, b: (b, h)),
            pl.BlockSpec((seq, G * HEAD_DIM), lambda h, b: (b, ng + h)),
            pl.BlockSpec((seq, G * HEAD_DIM), lambda h, b: (b, 2 * ng + h)),
            pl.BlockSpec((2, MOBA_BLOCK, MOBA_BLOCK), lambda h, b: (0, 0, 0)),
        ],
        out_specs=pl.BlockSpec((seq, G * HEAD_DIM), lambda h, b: (b, h)),
        scratch_shapes=[pltpu.VMEM((G, 2, MOBA_BLOCK, MOBA_BLOCK), F32)],
        compiler_params=_cparams(("arbitrary", "arbitrary")),
        name="moba_attention",
    )(table, qkv, qkv, qkv, jnp.asarray(bkt))


MIX_HALO = 16


def _pool_conv_tile(u_ref, gb_ref, gc_ref, hc_ref, uh_ref, gch_ref, hch_ref, pw_ref, ps_ref, cw_ref, pc_scr,
                    seq_start, t0):
    tm = u_ref.shape[0]
    G = len(POOL_WINDOWS)
    H = MIX_HALO
    pos = lax.broadcasted_iota(jnp.int32, (tm, GROUP_DIM), 0) + t0

    def with_halo(cur_ref, halo_ref, cs):
        return jnp.concatenate([jnp.where(seq_start, 0.0, halo_ref[:, cs]), cur_ref[:, cs]], axis=0)

    def shift(x, k):
        return pltpu.roll(x, k, 0)

    for gi, w in enumerate(POOL_WINDOWS):
        cs = slice(gi * GROUP_DIM, (gi + 1) * GROUP_DIM)
        ug = with_halo(u_ref, uh_ref, cs)
        win = ug
        k = 1
        while k < w:
            win = win + shift(win, k)
            k *= 2
        count = jnp.minimum(pos + 1, w).astype(F32)
        d = win[H:] / count - ug[H:]
        y = jnp.dot(d.astype(BF16), pw_ref[gi].astype(BF16), preferred_element_type=F32)
        pc_scr[:, cs] = (y * ps_ref[:, cs]).astype(pc_scr.dtype)

        uu = with_halo(gc_ref, gch_ref, cs) * with_halo(hc_ref, hch_ref, cs)
        conv = (cw_ref[0:1, cs] * shift(uu, 2) + cw_ref[1:2, cs] * shift(uu, 1)
                + cw_ref[2:3, cs] * uu)[H:]
        oc = slice((G + gi) * GROUP_DIM, (G + gi + 1) * GROUP_DIM)
        pc_scr[:, oc] = (gb_ref[:, cs] * conv).astype(pc_scr.dtype)


def _outproj_kernel(att_ref, u_ref, gb_ref, gc_ref, hc_ref, uh_ref, gch_ref, hch_ref, pw_ref, ps_ref, cw_ref,
                    x_ref, wa_ref, wp_ref, gpost_ref, gtm_ref, gpre_ref, scf_ref, shf_ref,
                    rwh_ref, rwl_ref, rb_ref,
                    x1_ref, h2_ref, idx_ref, gate_ref, rank_ref, cnt_ref, pc_scr, carry_scr, *, per_seq):
    i = pl.program_id(0)
    tm = x_ref.shape[0]

    @pl.when(i == 0)
    def _():
        carry_scr[...] = jnp.zeros_like(carry_scr)

    _pool_conv_tile(u_ref, gb_ref, gc_ref, hc_ref, uh_ref, gch_ref, hch_ref, pw_ref, ps_ref, cw_ref, pc_scr,
                    seq_start=i % per_seq == 0, t0=(i % per_seq) * tm)
    mix = (jnp.dot(att_ref[...], wa_ref[...], preferred_element_type=F32)
           + jnp.dot(pc_scr[...], wp_ref[...], preferred_element_type=F32))
    x1 = x_ref[...] + gtm_ref[0] * (_rms(mix) * gpost_ref[...])
    x1_ref[...] = x1
    h2 = (_rms(x1) * gpre_ref[...]) * (1.0 + scf_ref[0]) + shf_ref[0]
    half = h2.shape[1] // 2
    for s in range(ROW_SUBLANES):
        h2_ref[pl.ds(s, tm, stride=ROW_SUBLANES), :] = _pack_words(h2[:, s * LANES:(s + 1) * LANES],
                                      h2[:, half + s * LANES:half + (s + 1) * LANES])

    hi = h2.astype(BF16)
    lo = (h2 - hi.astype(F32)).astype(BF16)
    r2 = jnp.dot(hi, jnp.concatenate([rwh_ref[...], rwl_ref[...]], axis=1), preferred_element_type=F32)
    logits = (r2[:, :LANES] + r2[:, LANES:]
              + jnp.dot(lo, rwh_ref[...], preferred_element_type=F32)) + rb_ref[...]

    lane = lax.broadcasted_iota(jnp.int32, (tm, LANES), 1)
    work = logits
    mem = jnp.zeros((tm, LANES), F32)
    vals, hots = [], []
    idx_out = jnp.zeros((tm, LANES), jnp.int32)
    for kk in range(TOPK_EXPERTS):
        m = jnp.max(work, axis=-1, keepdims=True)
        ik = jnp.min(jnp.where(work == m, lane, LANES), axis=-1, keepdims=True)
        hot = lane == ik
        vals.append(m)
        hots.append(hot)
        idx_out = jnp.where(lane == kk, ik, idx_out)
        mem = jnp.where(hot, 1.0, mem)
        work = jnp.where(hot, -jnp.inf, work)
    idx_ref[...] = idx_out

    es = [jnp.exp(vk - vals[0]) for vk in vals]
    denom = es[0]
    for e in es[1:]:
        denom = denom + e
    gates = jnp.zeros((tm, LANES), F32)
    for kk in range(TOPK_EXPERTS):
        gates = jnp.where(lane == kk, es[kk] / denom, gates)
    gate_ref[...] = gates

    r_i = lax.broadcasted_iota(jnp.int32, (tm, tm), 0)
    c_i = lax.broadcasted_iota(jnp.int32, (tm, tm), 1)
    tri = jnp.where(c_i < r_i, 1.0, 0.0).astype(BF16)
    before = jnp.dot(tri, mem.astype(BF16), preferred_element_type=F32) + carry_scr[0:1, :]
    ranks = jnp.zeros((tm, LANES), jnp.int32)
    for kk in range(TOPK_EXPERTS):
        rk = jnp.sum(jnp.where(hots[kk], before, 0.0), axis=-1, keepdims=True)
        ranks = jnp.where(lane == kk, rk.astype(jnp.int32), ranks)
    rank_ref[...] = ranks
    carry = carry_scr[...] + jnp.sum(mem, axis=0, keepdims=True)
    carry_scr[...] = carry
    cnt_ref[...] = carry


def _outproj(att, rest, pool_w, pool_scale, conv_w, x2, wa, wp, gpost, gtm, gpre, scf, shf, rwh, rwl, rb, seq):
    T, D = x2.shape
    tm = OUTPROJ_TM
    per_b = seq // tm
    W = len(POOL_WINDOWS) * GROUP_DIM
    row = lambda w: pl.BlockSpec((tm, w), lambda i: (i, 0))
    cur = lambda c: pl.BlockSpec((tm, W), lambda i: (i, c))
    halo = lambda c: pl.BlockSpec((MIX_HALO, W), lambda i: (jnp.maximum(i * (tm // MIX_HALO) - 1, 0), c))
    const = lambda shape: pl.BlockSpec(shape, lambda i: tuple(0 for _ in shape), pipeline_mode=pl.Buffered(1))
    perb = pl.BlockSpec((1, 1, D), lambda i: (i // per_b, 0, 0))
    return pl.pallas_call(
        functools.partial(_outproj_kernel, per_seq=per_b),
        out_shape=(jax.ShapeDtypeStruct((T, D), F32), jax.ShapeDtypeStruct((T * ROW_SUBLANES, LANES), jnp.uint32),
                   jax.ShapeDtypeStruct((T, LANES), jnp.int32), jax.ShapeDtypeStruct((T, LANES), F32),
                   jax.ShapeDtypeStruct((T, LANES), jnp.int32), jax.ShapeDtypeStruct((8, LANES), F32)),
        grid=(T // tm,),
        in_specs=[row(att.shape[1]), cur(0), cur(1), cur(2), cur(3), halo(0), halo(2), halo(3),
                  const(pool_w.shape), const((1, W)), const((CONV_K, W)),
                  row(D), const(wa.shape), const(wp.shape),
                  const((1, D)), perb, const((1, D)), perb, perb,
                  const(rwh.shape), const(rwl.shape), const((1, LANES))],
        out_specs=(row(D), pl.BlockSpec((tm * ROW_SUBLANES, LANES), lambda i: (i, 0)),
                   row(LANES), row(LANES), row(LANES), pl.BlockSpec((8, LANES), lambda i: (0, 0))),
        scratch_shapes=[pltpu.VMEM((tm, 2 * W), BF16), pltpu.VMEM((8, LANES), F32)],
        compiler_params=_cparams(("arbitrary",)),
        name="out_proj_router",
    )(att, rest, rest, rest, rest, rest, rest, rest, pool_w, pool_scale.reshape(1, W), conv_w,
      x2, wa, wp, gpost, gtm, gpre, scf, shf, rwh, rwl, rb)


MOE_F_CHUNK = 256
MOE_D_CHUNK = 512
MOE_AHEAD = 2
MOE_SLOTS = MOE_AHEAD + 1


MOE_W_PHASES = 6


def _moe_kernel(te_ref, dest_ref, nu_ref, par_ref, pos_ref, nxt_ref, pn_ref, lo_ref, hi_ref,
                h_hbm, wg_hbm, bg_ref, wu_hbm, bu_ref, wd_hbm, bd_ref, o_ref,
                xbuf, act_scr, hi_scr, wgu_bf, wd_bf, stg_a, stg_b, rt_ref, sem, wsem, *, layer):
    i = pl.program_id(0)
    tm = act_scr.shape[0]
    Fe = act_scr.shape[1]
    D = wd_bf.shape[-1]
    half = D // 2
    n_used = nu_ref[0]
    last = n_used - 1

    def piece(ph, expert):
        if ph < 4:
            src_ref, dst = (wg_hbm, stg_a) if ph < 2 else (wu_hbm, stg_a)
            rows = D // 2
        else:
            src_ref, dst, rows = wd_hbm, stg_b, Fe // 2
        r0 = (ph % 2) * rows
        return pltpu.make_async_copy(src_ref.at[layer, expert, pl.ds(r0, rows), :], dst, wsem)

    def round_piece(ph, slot):
        if ph < 4:
            rows = D // 2
            FC = MOE_F_CHUNK
            for c in range(Fe // FC):
                c0 = (2 * c + (1 if ph >= 2 else 0)) * FC
                wgu_bf[slot, (ph % 2) * rows:(ph % 2 + 1) * rows, c0:c0 + FC] = (
                    stg_a[:, c * FC:(c + 1) * FC].astype(BF16))
        else:
            rows = Fe // 2
            wd_bf[slot, (ph % 2) * rows:(ph % 2 + 1) * rows, :] = stg_b[...].astype(BF16)

    def row_copy(tile_id, r, slot):
        tok8 = rt_ref[tile_id * tm + r]
        src = h_hbm.at[pl.ds(pl.multiple_of(tok8, ROW_SUBLANES), ROW_SUBLANES)]
        return pltpu.make_async_copy(src, xbuf.at[slot, pl.ds(r * ROW_SUBLANES, ROW_SUBLANES)], sem.at[slot])

    def tile_wait(slot):
        pltpu.make_async_copy(h_hbm.at[pl.ds(0, tm * ROW_SUBLANES)], xbuf.at[slot], sem.at[slot]).wait()

    @pl.when(i == 0)
    def _():
        _invert_rows(dest_ref, lo_ref, hi_ref, rt_ref)
        for t in range(MOE_AHEAD):
            def issue(r, carry, t=t):
                row_copy(jnp.minimum(t, last), r, t).start()
                return carry
            lax.fori_loop(0, tm, issue, 0)

    @pl.when(i < n_used)
    def _():
        slot = i % MOE_SLOTS
        nslot = (i + MOE_AHEAD) % MOE_SLOTS
        ntile = jnp.minimum(i + MOE_AHEAD, last)
        n_f, n_d = Fe // MOE_F_CHUNK, D // MOE_D_CHUNK
        per = tm // (n_f + n_d)

        def issue_part(p):
            for r in range(p * per, (p + 1) * per):
                row_copy(ntile, r, nslot).start()

        expert = te_ref[i]
        wslot = par_ref[i]
        k = pos_ref[i]
        nxt = nxt_ref[i]

        @pl.when(i == 0)
        def _():
            for ph in range(MOE_W_PHASES):
                piece(ph, expert).start()
                piece(ph, expert).wait()
                round_piece(ph, wslot)

        @pl.when((k == 0) & (i > 0))
        def _():
            done = jnp.minimum(pn_ref[i] - 1, MOE_W_PHASES)
            for ph in range(MOE_W_PHASES):
                @pl.when(ph >= done)
                def _(ph=ph):
                    @pl.when(ph > done)
                    def _():
                        piece(ph, expert).start()
                    piece(ph, expert).wait()
                    round_piece(ph, wslot)

        for ph in range(MOE_W_PHASES):
            @pl.when((k == ph + 1) & (nxt >= 0))
            def _(ph=ph):
                piece(ph, nxt).wait()
                round_piece(ph, 1 - wslot)

        for ph in range(MOE_W_PHASES):
            @pl.when((k == ph) & (nxt >= 0))
            def _(ph=ph):
                piece(ph, nxt).start()

        tile_wait(slot)
        words = [xbuf[slot, pl.ds(s, tm, stride=ROW_SUBLANES), :] for s in range(ROW_SUBLANES)]
        x = jnp.concatenate([_unpack_hi(w).astype(BF16) for w in words]
                            + [_unpack_lo(w).astype(BF16) for w in words], axis=1)
        for c in range(n_f):
            issue_part(c)
            cs = slice(c * MOE_F_CHUNK, (c + 1) * MOE_F_CHUNK)
            gu = jnp.dot(x, wgu_bf[wslot, :, 2 * c * MOE_F_CHUNK:2 * (c + 1) * MOE_F_CHUNK],
                         preferred_element_type=F32)
            g = gu[:, :MOE_F_CHUNK] + bg_ref[0, expert, :, cs]
            u = gu[:, MOE_F_CHUNK:] + bu_ref[0, expert, :, cs]
            g = jnp.minimum(g, SWIGLU_LIMIT)
            u = jnp.clip(u, -SWIGLU_LIMIT, SWIGLU_LIMIT)
            act_scr[:, cs] = (g * jax.nn.sigmoid(SWIGLU_ALPHA * g) * (u + 1.0)).astype(BF16)
        act = act_scr[...]
        for c in range(n_d):
            issue_part(n_f + c)
            c0 = c * MOE_D_CHUNK
            y = (jnp.dot(act, wd_bf[wslot, :, c0:c0 + MOE_D_CHUNK], preferred_element_type=F32)
                 + bd_ref[0, expert, :, c0:c0 + MOE_D_CHUNK])
            for j in range(MOE_D_CHUNK // LANES):
                col = c0 + j * LANES
                yj = y[:, j * LANES:(j + 1) * LANES]
                if col < half:
                    hi_scr[:, col:col + LANES] = _bf16_bits(yj)
                else:
                    s = (col - half) // LANES
                    o_ref[pl.ds(s, tm, stride=ROW_SUBLANES), :] = hi_scr[:, col - half:col - half + LANES] | (_bf16_bits(yj) >> 16)

        @pl.when(i == last)
        def _():
            for t in range(1, MOE_AHEAD + 1):
                tile_wait((i + t) % MOE_SLOTS)

    @pl.when(i >= n_used)
    def _():
        o_ref[...] = jnp.zeros_like(o_ref)


def _group_tables(tile_expert, n_used):
    n = tile_expert.shape[0]
    idx = jnp.arange(n, dtype=jnp.int32)
    first = jnp.concatenate([jnp.ones((1,), bool), tile_expert[1:] != tile_expert[:-1]])
    par = (jnp.cumsum(first.astype(jnp.int32)) - 1) % 2
    pos = idx - lax.cummax(jnp.where(first, idx, 0))
    later = (idx[None, :] > idx[:, None]) & (idx[None, :] < n_used[0]) & (tile_expert[None, :] != tile_expert[:, None])
    nxt_idx = jnp.min(jnp.where(later, idx[None, :], n), axis=1)
    nxt = jnp.where(nxt_idx < n, tile_expert[jnp.minimum(nxt_idx, n - 1)], -1)
    pn = jnp.concatenate([jnp.zeros((1,), jnp.int32), pos[:-1] + 1])
    return par.astype(jnp.int32), pos.astype(jnp.int32), nxt.astype(jnp.int32), pn.astype(jnp.int32)


def _moe(layer, tile_expert, dest, pad_rows, n_used, h2p, wg, bg, wu, bu, wd, bd, tm):
    L, E, D, Fe = wg.shape
    n_tiles = tile_expert.shape[0]
    lo, hi = pad_rows
    assert tm % (Fe // MOE_F_CHUNK + D // MOE_D_CHUNK) == 0 and (D // 2) % MOE_D_CHUNK == 0
    par, pos, nxt, pn = _group_tables(tile_expert, n_used)
    bspec = lambda w: pl.BlockSpec((1, E, 1, w), lambda i, *_: (layer, 0, 0, 0), pipeline_mode=pl.Buffered(1))
    hbm = pl.BlockSpec(memory_space=pl.ANY)
    return pl.pallas_call(
        functools.partial(_moe_kernel, layer=layer),
        out_shape=jax.ShapeDtypeStruct((n_tiles * tm * ROW_SUBLANES, LANES), jnp.uint32),
        grid_spec=pltpu.PrefetchScalarGridSpec(
            num_scalar_prefetch=9,
            grid=(n_tiles,),
            in_specs=[hbm, hbm, bspec(Fe), hbm, bspec(Fe), hbm, bspec(D)],
            out_specs=pl.BlockSpec((tm * ROW_SUBLANES, LANES), lambda i, *_: (i, 0)),
            scratch_shapes=[pltpu.VMEM((MOE_SLOTS, tm * ROW_SUBLANES, LANES), jnp.uint32),
                            pltpu.VMEM((tm, Fe), BF16), pltpu.VMEM((tm, D // 2), jnp.uint32),
                            pltpu.VMEM((2, D, 2 * Fe), BF16),
                            pltpu.VMEM((2, Fe, D), BF16),
                            pltpu.VMEM((D // 2, Fe), F32), pltpu.VMEM((Fe // 2, D), F32),
                            pltpu.SMEM((n_tiles * tm,), jnp.int32),
                            pltpu.SemaphoreType.DMA((MOE_SLOTS,)), pltpu.SemaphoreType.DMA],
        ),
        compiler_params=_cparams(("arbitrary",)),
        name="moe_experts",
    )(tile_expert, dest, n_used, par, pos, nxt, pn, lo, hi, h2p, wg, bg.reshape(L, E, 1, Fe),
      wu, bu.reshape(L, E, 1, Fe), wd, bd.reshape(L, E, 1, D))


COMBINE_AHEAD = 2
COMBINE_SLOTS = COMBINE_AHEAD + 1
COMBINE_ROW_CHUNK = 8


def _combine_kernel(dest_ref, yp_hbm, gate_ref, x_ref, gpost_ref, gtf_ref, o_ref, buf, sem):
    i = pl.program_id(0)
    n = pl.num_programs(0)
    tc, D = x_ref.shape
    half = D // 2
    K = TOPK_EXPERTS

    def row_copies(tile_id, r, slot):
        base = tile_id * tc * K
        for kk in range(K):
            d = dest_ref[base + r * K + kk]
            src = yp_hbm.at[pl.ds(pl.multiple_of(d, ROW_SUBLANES), ROW_SUBLANES)]
            dst = buf.at[slot, kk, pl.ds(r * ROW_SUBLANES, ROW_SUBLANES)]
            pltpu.make_async_copy(src, dst, sem.at[slot]).start(priority=kk % 2)

    def tile_wait(slot):
        for kk in range(K):
            pltpu.make_async_copy(yp_hbm.at[pl.ds(0, tc * ROW_SUBLANES)], buf.at[slot, kk], sem.at[slot]).wait()

    @pl.when(i == 0)
    def _():
        for t in range(COMBINE_AHEAD):
            def issue(r, carry, t=t):
                row_copies(jnp.minimum(t, n - 1), r, t)
                return carry
            lax.fori_loop(0, tc, issue, 0)

    slot = i % COMBINE_SLOTS
    nslot = (i + COMBINE_AHEAD) % COMBINE_SLOTS
    ntile = jnp.minimum(i + COMBINE_AHEAD, n - 1)
    tile_wait(slot)
    RC = COMBINE_ROW_CHUNK
    for c in range(tc // RC):
        for r in range(c * RC, (c + 1) * RC):
            row_copies(ntile, r, nslot)
        rows = slice(c * RC, (c + 1) * RC)
        gates = gate_ref[rows, :]
        gk = [jnp.broadcast_to(gates[:, kk:kk + 1], (RC, LANES)) for kk in range(K)]
        his, los = [], []
        sq = jnp.zeros((RC, LANES), F32)
        for s in range(ROW_SUBLANES):
            hi = lo = None
            for kk in range(K):
                w = buf[slot, kk, pl.ds(c * RC * ROW_SUBLANES + s, RC, stride=ROW_SUBLANES), :]
                h, l = gk[kk] * _unpack_hi(w), gk[kk] * _unpack_lo(w)
                hi, lo = (h, l) if kk == 0 else (hi + h, lo + l)
            his.append(hi)
            los.append(lo)
            sq = sq + hi * hi + lo * lo
        inv = lax.rsqrt(jnp.sum(sq, axis=-1, keepdims=True) * (1.0 / D) + RMS_EPS)
        for s in range(ROW_SUBLANES):
            for col, y in ((s * LANES, his[s]), (half + s * LANES, los[s])):
                cs = slice(col, col + LANES)
                o_ref[rows, cs] = x_ref[rows, cs] + gtf_ref[0, :, cs] * ((y * inv) * gpost_ref[:, cs])

    @pl.when(i == n - 1)
    def _():
        for t in range(1, COMBINE_AHEAD + 1):
            tile_wait((i + t) % COMBINE_SLOTS)


def _combine(dest, yp, gates, x1, gpost, gtf, seq):
    T, D = x1.shape
    tc = COMBINE_TC
    per_b = seq // tc
    return pl.pallas_call(
        _combine_kernel,
        out_shape=jax.ShapeDtypeStruct((T, D), F32),
        grid_spec=pltpu.PrefetchScalarGridSpec(
            num_scalar_prefetch=1,
            grid=(T // tc,),
            in_specs=[pl.BlockSpec(memory_space=pl.ANY),
                      pl.BlockSpec((tc, LANES), lambda i, d: (i, 0)),
                      pl.BlockSpec((tc, D), lambda i, d: (i, 0)),
                      pl.BlockSpec((1, D), lambda i, d: (0, 0)),
                      pl.BlockSpec((1, 1, D), lambda i, d: (i // per_b, 0, 0))],
            out_specs=pl.BlockSpec((tc, D), lambda i, d: (i, 0)),
            scratch_shapes=[pltpu.VMEM((COMBINE_SLOTS, TOPK_EXPERTS, tc * ROW_SUBLANES, LANES), jnp.uint32),
                            pltpu.SemaphoreType.DMA((COMBINE_SLOTS,))],
        ),
        compiler_params=_cparams(("arbitrary",)),
        name="moe_combine",
    )(dest, yp, gates, x1, gpost, gtf)


def _dest_kernel(idx_ref, rank_ref, cnt_ref, dest_ref, *, tm_rows):
    tm = idx_ref.shape[0]
    tiles = jnp.floor((cnt_ref[0:1, :] + float(tm_rows - 1)) * (1.0 / tm_rows))
    r_i = lax.broadcasted_iota(jnp.int32, (LANES, LANES), 0)
    c_i = lax.broadcasted_iota(jnp.int32, (LANES, LANES), 1)
    before = jnp.where(r_i < c_i, 1.0, 0.0).astype(BF16)
    start = jnp.dot(jnp.broadcast_to(tiles, (8, LANES)).astype(BF16), before,
                    preferred_element_type=F32)[0:1] * float(tm_rows)
    lane = lax.broadcasted_iota(jnp.int32, (tm, LANES), 1)
    idx = idx_ref[...]
    rank = rank_ref[...]
    out = jnp.zeros((tm, LANES), jnp.int32)
    for kk in range(TOPK_EXPERTS):
        base = jnp.sum(jnp.where(lane == idx[:, kk:kk + 1], start, 0.0), axis=-1, keepdims=True)
        out = jnp.where(lane == kk, base.astype(jnp.int32) + rank[:, kk:kk + 1], out)
    dest_ref[...] = out


def _dest(idx, rank, cnt, tm_rows):
    T = idx.shape[0]
    tm = DEST_TM
    assert tm_rows & (tm_rows - 1) == 0
    row = pl.BlockSpec((tm, LANES), lambda i: (i, 0))
    return pl.pallas_call(
        functools.partial(_dest_kernel, tm_rows=tm_rows),
        out_shape=jax.ShapeDtypeStruct((T, LANES), jnp.int32),
        grid=(T // tm,),
        in_specs=[row, row, pl.BlockSpec((8, LANES), lambda i: (0, 0))],
        out_specs=row,
        compiler_params=_cparams(("parallel",)),
        name="route_dest",
    )(idx, rank, cnt)


INVERT_UNROLL = 8


def _invert_rows(dest_ref, lo_ref, hi_ref, out_ref):
    def fill(j, carry):
        out_ref[j] = 0
        return carry

    for e in range(lo_ref.shape[0]):
        lax.fori_loop(lo_ref[e], hi_ref[e], fill, 0)

    per = INVERT_UNROLL // TOPK_EXPERTS

    def body(b, carry):
        for u in range(INVERT_UNROLL):
            out_ref[dest_ref[b * INVERT_UNROLL + u]] = (b * per + u // TOPK_EXPERTS) * ROW_SUBLANES
        return carry

    lax.fori_loop(0, dest_ref.shape[0] // INVERT_UNROLL, body, 0)


def _routing_tables(idx, rank, cnt, n_experts, tm):
    T = idx.shape[0]
    n_tiles = (T * TOPK_EXPERTS) // tm + n_experts
    counts = cnt[0, :n_experts].astype(jnp.int32)
    padded = ((counts + tm - 1) // tm) * tm
    pad_end = jnp.cumsum(padded)
    tile_start = jnp.arange(n_tiles, dtype=jnp.int32) * tm
    tile_expert = jnp.minimum(jnp.sum((pad_end[None, :] <= tile_start[:, None]).astype(jnp.int32), axis=1),
                              n_experts - 1)
    n_used = pad_end[-1:] // tm
    pad_rows = (pad_end - padded + counts, pad_end)
    dest = _dest(idx, rank, cnt, tm)[:, :TOPK_EXPERTS].reshape(-1)
    return dest, pad_rows, tile_expert, n_used


def kernel(x, c, w_ada, b_ada, g_pre_mix, g_post_mix, g_pre_ffn, g_post_ffn, w_in, w_out, pool_w, pool_scale,
           conv_w, rel_bias, router_w, router_b, w_gate, b_gate, w_up, b_up, w_down, b_down):
    B, S, D = x.shape
    L = w_ada.shape[0]
    T = B * S
    E = router_w.shape[-1]
    attn_w = D // 2
    heads = attn_w // HEAD_DIM
    moe_tm = MOE_TM
    assert D == 2 * ROW_SUBLANES * LANES

    mod = _ada(c, w_ada, b_ada)
    w_in_bf = w_in.astype(BF16)
    x2 = x.reshape(T, D)
    for l in range(L):
        sh_m, sc_m, gt_m, sh_f, sc_f, gt_f = [mod[l, :, i * D:(i + 1) * D].reshape(B, 1, D) for i in range(N_MOD)]
        row = lambda v: v.reshape(1, D)

        qkv, rest = _inproj(l, x2, row(g_pre_mix[l]), sc_m, sh_m, w_in_bf, S, attn_w)
        att = _attention(qkv, rel_bias, B, S, heads)

        wo = w_out[l].astype(BF16)
        rw = jnp.zeros((D, LANES), F32).at[:, :E].set(router_w[l])
        rwh = rw.astype(BF16)
        rwl = (rw - rwh.astype(F32)).astype(BF16)
        rb = jnp.full((1, LANES), NEG_INF, F32).at[0, :E].set(router_b[l])
        x1, h2, idx, gates, rank, cnt = _outproj(
            att, rest, pool_w[l], pool_scale[l], conv_w[l], x2, wo[:attn_w], wo[attn_w:],
            row(g_post_mix[l]), gt_m, row(g_pre_ffn[l]), sc_f, sh_f,
            rwh, rwl, rb, S)

        dest, pad_rows, tile_expert, n_used = _routing_tables(idx, rank, cnt, E, moe_tm)
        yp = _moe(l, tile_expert, dest, pad_rows, n_used, h2, w_gate, b_gate, w_up, b_up, w_down, b_down, moe_tm)
        x2 = _combine(dest * ROW_SUBLANES, yp, gates, x1, row(g_post_ffn[l]), gt_f, S)
    return x2.reshape(B, S, D)
```

```python
import functools
import math

import numpy as np
import jax
import jax.numpy as jnp
from jax import lax
from jax.experimental import pallas as pl
from jax.experimental.pallas import tpu as pltpu

F32 = jnp.float32
BF16 = jnp.bfloat16

LANES = 128
HEAD_DIM = 128
MOBA_BLOCK = 256
MOBA_TOPK = 3
REL_BUCKETS = 32
REL_MAX_DISTANCE = 128
POOL_WINDOWS = (2, 4, 8, 16)
GROUP_DIM = 128
CONV_K = 3
TOPK_EXPERTS = 4
SWIGLU_LIMIT = 7.0
SWIGLU_ALPHA = 1.702
N_MOD = 6
RMS_EPS = 1e-6
NEG_INF = -1e30
ATTN_SCALE = HEAD_DIM ** -0.5
LOG2E = math.log2(math.e)

VMEM_LIMIT = 56 * 1024 * 1024

ADA_TN = 1024
OUTPROJ_TM = 512
MOE_TM = 256
COMBINE_TC = 256
DEST_TM = 2048


def _cparams(sem):
    return pltpu.CompilerParams(dimension_semantics=sem, vmem_limit_bytes=VMEM_LIMIT)


def _rms(x):
    return x * lax.rsqrt(jnp.mean(x * x, axis=-1, keepdims=True) + RMS_EPS)


ROW_SUBLANES = 8


def _bf16_bits(v):
    return lax.bitcast_convert_type(v.astype(BF16).astype(F32), jnp.uint32)


def _pack_words(hi, lo):
    return _bf16_bits(hi) | (_bf16_bits(lo) >> 16)


def _unpack_hi(w):
    return lax.bitcast_convert_type(w & jnp.uint32(0xFFFF0000), F32)


def _unpack_lo(w):
    return lax.bitcast_convert_type(w << 16, F32)


def _ada_kernel(c_ref, w_ref, b_ref, o_ref):
    c = c_ref[...]
    ca = (c * jax.nn.sigmoid(c)).astype(BF16)
    o_ref[0] = jnp.dot(ca, w_ref[0].astype(BF16), preferred_element_type=F32) + b_ref[0]


def _ada(c, w_ada, b_ada):
    L, D, N = w_ada.shape
    B = c.shape[0]
    tn = ADA_TN
    return pl.pallas_call(
        _ada_kernel,
        out_shape=jax.ShapeDtypeStruct((L, B, N), F32),
        grid=(L, N // tn),
        in_specs=[
            pl.BlockSpec((B, D), lambda l, j: (0, 0)),
            pl.BlockSpec((1, D, tn), lambda l, j: (l, 0, j)),
            pl.BlockSpec((1, 1, tn), lambda l, j: (l, 0, j)),
        ],
        out_specs=pl.BlockSpec((1, B, tn), lambda l, j: (l, 0, j)),
        compiler_params=_cparams(("parallel", "parallel")),
        name="ada_mod",
    )(c, w_ada, b_ada.reshape(L, 1, N))


INPROJ_TM = 1024
INPROJ_TN = 1024


def _inproj_kernel(x_ref, g_ref, sc_ref, sh_ref, w_ref, qkv_ref, rest_ref, h_scr, *, n_qkv):
    j = pl.program_id(1)

    @pl.when(j == 0)
    def _():
        h = _rms(x_ref[...]) * g_ref[...]
        h = h * (1.0 + sc_ref[0]) + sh_ref[0]
        h_scr[...] = h.astype(BF16)

    acc = jnp.dot(h_scr[...], w_ref[0], preferred_element_type=F32)

    @pl.when(j < n_qkv)
    def _():
        qkv_ref[...] = acc.astype(BF16)

    @pl.when(j >= n_qkv)
    def _():
        rest_ref[...] = acc


def _inproj(layer, x2, g, sc, sh, w_bf, seq, attn_w):
    T, D = x2.shape
    N = w_bf.shape[2]
    tm, tn = INPROJ_TM, INPROJ_TN
    n_qkv = 3 * attn_w // tn
    n_rest = (N - 3 * attn_w) // tn
    per_b = seq // tm
    return pl.pallas_call(
        functools.partial(_inproj_kernel, n_qkv=n_qkv),
        out_shape=(jax.ShapeDtypeStruct((T, 3 * attn_w), BF16),
                   jax.ShapeDtypeStruct((T, N - 3 * attn_w), F32)),
        grid=(T // tm, n_qkv + n_rest),
        in_specs=[
            pl.BlockSpec((tm, D), lambda i, j: (i, 0)),
            pl.BlockSpec((1, D), lambda i, j: (0, 0)),
            pl.BlockSpec((1, 1, D), lambda i, j: (i // per_b, 0, 0)),
            pl.BlockSpec((1, 1, D), lambda i, j: (i // per_b, 0, 0)),
            pl.BlockSpec((1, D, tn), lambda i, j: (layer, 0, j)),
        ],
        out_specs=(
            pl.BlockSpec((tm, tn), lambda i, j: (i, jnp.minimum(j, n_qkv - 1))),
            pl.BlockSpec((tm, tn), lambda i, j: (i, jnp.maximum(j - n_qkv, 0))),
        ),
        scratch_shapes=[pltpu.VMEM((tm, D), BF16)],
        compiler_params=_cparams(("parallel", "arbitrary")),
        name="mixer_in_proj",
    )(x2, g, sc, sh, w_bf)


def _rel_bucket_np(n):
    n = np.maximum(n, 0)
    max_exact = REL_BUCKETS // 2
    nf = np.maximum(n, max_exact).astype(np.float32)
    large = max_exact + (np.log(nf / np.float32(max_exact)) / np.float32(math.log(REL_MAX_DISTANCE / max_exact))
                         * np.float32(REL_BUCKETS - max_exact)).astype(np.int32)
    large = np.minimum(large, REL_BUCKETS - 1)
    return np.where(n < max_exact, n, large).astype(np.int32)


def _bucket_tables(seq):
    qi = np.arange(MOBA_BLOCK)[:, None]
    ki = np.arange(MOBA_BLOCK)[None, :]
    own = np.where(ki <= qi, _rel_bucket_np(qi - ki), -1)
    prev = _rel_bucket_np(qi - ki + MOBA_BLOCK)
    far = _rel_bucket_np(np.arange(MOBA_BLOCK + 1, max(seq, MOBA_BLOCK + 2)))
    assert np.all(far == far[0])
    return np.stack([own, prev]).astype(np.int32), int(far[0])


ATTN_HEADS_PER_STEP = 4


def _attn_kernel(tab_ref, q_ref, k_ref, v_ref, bkt_ref, o_ref, bias_scr, *, nblk, far_bucket):
    hg = pl.program_id(0)
    b = pl.program_id(1)
    L = MOBA_BLOCK
    G = ATTN_HEADS_PER_STEP

    @pl.when(b == 0)
    def _():
        for g in range(G):
            for m in range(2):
                bk = bkt_ref[m]
                acc = jnp.full((L, L), NEG_INF, F32)
                for r in range(REL_BUCKETS):
                    acc = jnp.where(bk == r, tab_ref[hg * G + g, r] * LOG2E, acc)
                bias_scr[g, m] = acc

    dn = (((1,), (1,)), ((), ()))
    lane = lax.broadcasted_iota(jnp.int32, (L, LANES), 1)
    heads = []
    for g in range(G):
        cols = slice(g * HEAD_DIM, (g + 1) * HEAD_DIM)
        q, k, v = q_ref[:, cols], k_ref[:, cols], v_ref[:, cols]
        kmean = jnp.mean(k.astype(F32).reshape(nblk, L, HEAD_DIM), axis=1)
        kmean = jnp.concatenate([kmean, jnp.zeros((LANES - nblk, HEAD_DIM), F32)], axis=0).astype(BF16)
        gate = lax.dot_general(q, kmean, dn, preferred_element_type=F32)
        heads.append((q, k, v, gate, tab_ref[hg * G + g, far_bucket] * LOG2E))

    for a, g in [(a, g) for a in range(nblk) for g in range(G)]:
        q, k, v, gate, far_bias = heads[g]
        cols = slice(g * HEAD_DIM, (g + 1) * HEAD_DIM)
        qa = q[a * L:(a + 1) * L]
        n = (a + 1) * L
        s = lax.dot_general(qa, k[:n], dn, preferred_element_type=F32) * (ATTN_SCALE * LOG2E)
        hide = None
        if a > MOBA_TOPK:
            ga = jnp.where(lane < a, gate[a * L:(a + 1) * L], NEG_INF)
            cnt = jnp.zeros((L, LANES), F32)
            for i in range(a):
                gi = ga[:, i:i + 1]
                ge = jnp.where(gi >= ga, 1.0, 0.0)
                gt = jnp.where(gi > ga, 1.0, 0.0)
                cnt = cnt + jnp.where(lane > i, ge, gt)
            hide = jnp.where((cnt < float(MOBA_TOPK)) & (lane < a), 0.0, NEG_INF)
            hide_far = hide + far_bias
        pieces = []
        for j in range(a + 1):
            sj = s[:, j * L:(j + 1) * L]
            if j == a:
                sj = sj + bias_scr[g, 0]
            elif j == a - 1:
                sj = sj + bias_scr[g, 1]
                if hide is not None:
                    sj = sj + hide[:, j:j + 1]
            else:
                sj = sj + (far_bias if hide is None else hide_far[:, j:j + 1])
            pieces.append(sj)
        s = pieces[0] if a == 0 else jnp.concatenate(pieces, axis=1)
        m = jnp.max(s, axis=-1, keepdims=True)
        p = jnp.exp2(s - m)
        l = jnp.sum(p, axis=-1, keepdims=True)
        o = jnp.dot(p.astype(BF16), v[:n], preferred_element_type=F32) / l
        o_ref[a * L:(a + 1) * L, cols] = o.astype(o_ref.dtype)


def _attention(qkv, rel_bias, batch, seq, heads):
    T = qkv.shape[0]
    nblk = seq // MOBA_BLOCK
    G = ATTN_HEADS_PER_STEP
    assert heads % G == 0
    ng = heads // G
    bkt, far_bucket = _bucket_tables(seq)
    table = rel_bias.T.astype(F32)
    return pl.pallas_call(
        functools.partial(_attn_kernel, nblk=nblk, far_bucket=far_bucket),
        out_shape=jax.ShapeDtypeStruct((T, heads * HEAD_DIM), BF16),
        grid=(ng, batch),
        in_specs=[
            pl.BlockSpec(memory_space=pltpu.SMEM),
            pl.BlockSpec((seq, G * HEAD_DIM), lambda h, b: (b, h)),
            pl.BlockSpec((seq, G * HEAD_DIM), lambda h, b: (b, ng + h)),
            pl.BlockSpec((seq, G * HEAD_DIM), lambda h, b: (b, 2 * ng + h)),
            pl.BlockSpec((2, MOBA_BLOCK, MOBA_BLOCK), lambda h, b: (0, 0, 0)),
        ],
        out_specs=pl.BlockSpec((seq, G * HEAD_DIM), lambda h, b: (b, h)),
        scratch_shapes=[pltpu.VMEM((G, 2, MOBA_BLOCK, MOBA_BLOCK), F32)],
        compiler_params=_cparams(("arbitrary", "arbitrary")),
        name="moba_attention",
    )(table, qkv, qkv, qkv, jnp.asarray(bkt))


MIX_HALO = 16


def _pool_conv_tile(u_ref, gb_ref, gc_ref, hc_ref, uh_ref, gch_ref, hch_ref, pw_ref, ps_ref, cw_ref, pc_scr,
                    seq_start, t0):
    tm = u_ref.shape[0]
    G = len(POOL_WINDOWS)
    H = MIX_HALO
    pos = lax.broadcasted_iota(jnp.int32, (tm, GROUP_DIM), 0) + t0

    def with_halo(cur_ref, halo_ref, cs):
        return jnp.concatenate([jnp.where(seq_start, 0.0, halo_ref[:, cs]), cur_ref[:, cs]], axis=0)

    def shift(x, k):
        return pltpu.roll(x, k, 0)

    for gi, w in enumerate(POOL_WINDOWS):
        cs = slice(gi * GROUP_DIM, (gi + 1) * GROUP_DIM)
        ug = with_halo(u_ref, uh_ref, cs)
        win = ug
        k = 1
        while k < w:
            win = win + shift(win, k)
            k *= 2
        count = jnp.minimum(pos + 1, w).astype(F32)
        d = win[H:] / count - ug[H:]
        y = jnp.dot(d.astype(BF16), pw_ref[gi].astype(BF16), preferred_element_type=F32)
        pc_scr[:, cs] = (y * ps_ref[:, cs]).astype(pc_scr.dtype)

        uu = with_halo(gc_ref, gch_ref, cs) * with_halo(hc_ref, hch_ref, cs)
        conv = (cw_ref[0:1, cs] * shift(uu, 2) + cw_ref[1:2, cs] * shift(uu, 1)
                + cw_ref[2:3, cs] * uu)[H:]
        oc = slice((G + gi) * GROUP_DIM, (G + gi + 1) * GROUP_DIM)
        pc_scr[:, oc] = (gb_ref[:, cs] * conv).astype(pc_scr.dtype)


def _outproj_kernel(att_ref, u_ref, gb_ref, gc_ref, hc_ref, uh_ref, gch_ref, hch_ref, pw_ref, ps_ref, cw_ref,
                    x_ref, wa_ref, wp_ref, gpost_ref, gtm_ref, gpre_ref, scf_ref, shf_ref,
                    rwh_ref, rwl_ref, rb_ref,
                    x1_ref, h2_ref, idx_ref, gate_ref, rank_ref, cnt_ref, pc_scr, carry_scr, *, per_seq):
    i = pl.program_id(0)
    tm = x_ref.shape[0]

    @pl.when(i == 0)
    def _():
        carry_scr[...] = jnp.zeros_like(carry_scr)

    _pool_conv_tile(u_ref, gb_ref, gc_ref, hc_ref, uh_ref, gch_ref, hch_ref, pw_ref, ps_ref, cw_ref, pc_scr,
                    seq_start=i % per_seq == 0, t0=(i % per_seq) * tm)
    mix = (jnp.dot(att_ref[...], wa_ref[...], preferred_element_type=F32)
           + jnp.dot(pc_scr[...], wp_ref[...], preferred_element_type=F32))
    x1 = x_ref[...] + gtm_ref[0] * (_rms(mix) * gpost_ref[...])
    x1_ref[...] = x1
    h2 = (_rms(x1) * gpre_ref[...]) * (1.0 + scf_ref[0]) + shf_ref[0]
    half = h2.shape[1] // 2
    for s in range(ROW_SUBLANES):
        h2_ref[pl.ds(s, tm, stride=ROW_SUBLANES), :] = _pack_words(h2[:, s * LANES:(s + 1) * LANES],
                                      h2[:, half + s * LANES:half + (s + 1) * LANES])

    hi = h2.astype(BF16)
    lo = (h2 - hi.astype(F32)).astype(BF16)
    r2 = jnp.dot(hi, jnp.concatenate([rwh_ref[...], rwl_ref[...]], axis=1), preferred_element_type=F32)
    logits = (r2[:, :LANES] + r2[:, LANES:]
              + jnp.dot(lo, rwh_ref[...], preferred_element_type=F32)) + rb_ref[...]

    lane = lax.broadcasted_iota(jnp.int32, (tm, LANES), 1)
    work = logits
    mem = jnp.zeros((tm, LANES), F32)
    vals, hots = [], []
    idx_out = jnp.zeros((tm, LANES), jnp.int32)
    for kk in range(TOPK_EXPERTS):
        m = jnp.max(work, axis=-1, keepdims=True)
        ik = jnp.min(jnp.where(work == m, lane, LANES), axis=-1, keepdims=True)
        hot = lane == ik
        vals.append(m)
        hots.append(hot)
        idx_out = jnp.where(lane == kk, ik, idx_out)
        mem = jnp.where(hot, 1.0, mem)
        work = jnp.where(hot, -jnp.inf, work)
    idx_ref[...] = idx_out

    es = [jnp.exp(vk - vals[0]) for vk in vals]
    denom = es[0]
    for e in es[1:]:
        denom = denom + e
    gates = jnp.zeros((tm, LANES), F32)
    for kk in range(TOPK_EXPERTS):
        gates = jnp.where(lane == kk, es[kk] / denom, gates)
    gate_ref[...] = gates

    r_i = lax.broadcasted_iota(jnp.int32, (tm, tm), 0)
    c_i = lax.broadcasted_iota(jnp.int32, (tm, tm), 1)
    tri = jnp.where(c_i < r_i, 1.0, 0.0).astype(BF16)
    before = jnp.dot(tri, mem.astype(BF16), preferred_element_type=F32) + carry_scr[0:1, :]
    ranks = jnp.zeros((tm, LANES), jnp.int32)
    for kk in range(TOPK_EXPERTS):
        rk = jnp.sum(jnp.where(hots[kk], before, 0.0), axis=-1, keepdims=True)
        ranks = jnp.where(lane == kk, rk.astype(jnp.int32), ranks)
    rank_ref[...] = ranks
    carry = carry_scr[...] + jnp.sum(mem, axis=0, keepdims=True)
    carry_scr[...] = carry
    cnt_ref[...] = carry


def _outproj(att, rest, pool_w, pool_scale, conv_w, x2, wa, wp, gpost, gtm, gpre, scf, shf, rwh, rwl, rb, seq):
    T, D = x2.shape
    tm = OUTPROJ_TM
    per_b = seq // tm
    W = len(POOL_WINDOWS) * GROUP_DIM
    row = lambda w: pl.BlockSpec((tm, w), lambda i: (i, 0))
    cur = lambda c: pl.BlockSpec((tm, W), lambda i: (i, c))
    halo = lambda c: pl.BlockSpec((MIX_HALO, W), lambda i: (jnp.maximum(i * (tm // MIX_HALO) - 1, 0), c))
    const = lambda shape: pl.BlockSpec(shape, lambda i: tuple(0 for _ in shape), pipeline_mode=pl.Buffered(1))
    perb = pl.BlockSpec((1, 1, D), lambda i: (i // per_b, 0, 0))
    return pl.pallas_call(
        functools.partial(_outproj_kernel, per_seq=per_b),
        out_shape=(jax.ShapeDtypeStruct((T, D), F32), jax.ShapeDtypeStruct((T * ROW_SUBLANES, LANES), jnp.uint32),
                   jax.ShapeDtypeStruct((T, LANES), jnp.int32), jax.ShapeDtypeStruct((T, LANES), F32),
                   jax.ShapeDtypeStruct((T, LANES), jnp.int32), jax.ShapeDtypeStruct((8, LANES), F32)),
        grid=(T // tm,),
        in_specs=[row(att.shape[1]), cur(0), cur(1), cur(2), cur(3), halo(0), halo(2), halo(3),
                  const(pool_w.shape), const((1, W)), const((CONV_K, W)),
                  row(D), const(wa.shape), const(wp.shape),
                  const((1, D)), perb, const((1, D)), perb, perb,
                  const(rwh.shape), const(rwl.shape), const((1, LANES))],
        out_specs=(row(D), pl.BlockSpec((tm * ROW_SUBLANES, LANES), lambda i: (i, 0)),
                   row(LANES), row(LANES), row(LANES), pl.BlockSpec((8, LANES), lambda i: (0, 0))),
        scratch_shapes=[pltpu.VMEM((tm, 2 * W), BF16), pltpu.VMEM((8, LANES), F32)],
        compiler_params=_cparams(("arbitrary",)),
        name="out_proj_router",
    )(att, rest, rest, rest, rest, rest, rest, rest, pool_w, pool_scale.reshape(1, W), conv_w,
      x2, wa, wp, gpost, gtm, gpre, scf, shf, rwh, rwl, rb)


MOE_F_CHUNK = 256
MOE_D_CHUNK = 512
MOE_AHEAD = 2
MOE_SLOTS = MOE_AHEAD + 1


MOE_W_PHASES = 6


def _moe_kernel(te_ref, dest_ref, nu_ref, par_ref, pos_ref, nxt_ref, pn_ref, lo_ref, hi_ref,
                h_hbm, wg_hbm, bg_ref, wu_hbm, bu_ref, wd_hbm, bd_ref, o_ref,
                xbuf, act_scr, hi_scr, wg_bf, wu_bf, wd_bf, stg_a, stg_b, rt_ref, sem, wsem, *, layer):
    i = pl.program_id(0)
    tm = act_scr.shape[0]
    Fe = act_scr.shape[1]
    D = wd_bf.shape[-1]
    half = D // 2
    n_used = nu_ref[0]
    last = n_used - 1

    def piece(ph, expert):
        if ph < 4:
            src_ref, dst = (wg_hbm, stg_a) if ph < 2 else (wu_hbm, stg_a)
            rows = D // 2
        else:
            src_ref, dst, rows = wd_hbm, stg_b, Fe // 2
        r0 = (ph % 2) * rows
        return pltpu.make_async_copy(src_ref.at[layer, expert, pl.ds(r0, rows), :], dst, wsem)

    def round_piece(ph, slot):
        if ph < 4:
            rows = D // 2
            dst = wg_bf if ph < 2 else wu_bf
            dst[slot, (ph % 2) * rows:(ph % 2 + 1) * rows, :] = stg_a[...].astype(BF16)
        else:
            rows = Fe // 2
            wd_bf[slot, (ph % 2) * rows:(ph % 2 + 1) * rows, :] = stg_b[...].astype(BF16)

    def row_copy(tile_id, r, slot):
        tok8 = rt_ref[tile_id * tm + r]
        src = h_hbm.at[pl.ds(pl.multiple_of(tok8, ROW_SUBLANES), ROW_SUBLANES)]
        return pltpu.make_async_copy(src, xbuf.at[slot, pl.ds(r * ROW_SUBLANES, ROW_SUBLANES)], sem.at[slot])

    def tile_wait(slot):
        pltpu.make_async_copy(h_hbm.at[pl.ds(0, tm * ROW_SUBLANES)], xbuf.at[slot], sem.at[slot]).wait()

    @pl.when(i == 0)
    def _():
        _invert_rows(dest_ref, lo_ref, hi_ref, rt_ref)
        for t in range(MOE_AHEAD):
            def issue(r, carry, t=t):
                row_copy(jnp.minimum(t, last), r, t).start()
                return carry
            lax.fori_loop(0, tm, issue, 0)

    @pl.when(i < n_used)
    def _():
        slot = i % MOE_SLOTS
        nslot = (i + MOE_AHEAD) % MOE_SLOTS
        ntile = jnp.minimum(i + MOE_AHEAD, last)
        n_f, n_d = Fe // MOE_F_CHUNK, D // MOE_D_CHUNK
        per = tm // (n_f + n_d)

        def issue_part(p):
            for r in range(p * per, (p + 1) * per):
                row_copy(ntile, r, nslot).start()

        expert = te_ref[i]
        wslot = par_ref[i]
        k = pos_ref[i]
        nxt = nxt_ref[i]

        @pl.when(i == 0)
        def _():
            for ph in range(MOE_W_PHASES):
                piece(ph, expert).start()
                piece(ph, expert).wait()
                round_piece(ph, wslot)

        @pl.when((k == 0) & (i > 0))
        def _():
            done = jnp.minimum(pn_ref[i] - 1, MOE_W_PHASES)
            for ph in range(MOE_W_PHASES):
                @pl.when(ph >= done)
                def _(ph=ph):
                    @pl.when(ph > done)
                    def _():
                        piece(ph, expert).start()
                    piece(ph, expert).wait()
                    round_piece(ph, wslot)

        for ph in range(MOE_W_PHASES):
            @pl.when((k == ph + 1) & (nxt >= 0))
            def _(ph=ph):
                piece(ph, nxt).wait()
                round_piece(ph, 1 - wslot)

        for ph in range(MOE_W_PHASES):
            @pl.when((k == ph) & (nxt >= 0))
            def _(ph=ph):
                piece(ph, nxt).start()

        tile_wait(slot)
        words = [xbuf[slot, pl.ds(s, tm, stride=ROW_SUBLANES), :] for s in range(ROW_SUBLANES)]
        x = jnp.concatenate([_unpack_hi(w).astype(BF16) for w in words]
                            + [_unpack_lo(w).astype(BF16) for w in words], axis=1)
        for c in range(n_f):
            issue_part(c)
            cs = slice(c * MOE_F_CHUNK, (c + 1) * MOE_F_CHUNK)
            g = jnp.dot(x, wg_bf[wslot, :, cs], preferred_element_type=F32) + bg_ref[0, expert, :, cs]
            u = jnp.dot(x, wu_bf[wslot, :, cs], preferred_element_type=F32) + bu_ref[0, expert, :, cs]
            g = jnp.minimum(g, SWIGLU_LIMIT)
            u = jnp.clip(u, -SWIGLU_LIMIT, SWIGLU_LIMIT)
            act_scr[:, cs] = (g * jax.nn.sigmoid(SWIGLU_ALPHA * g) * (u + 1.0)).astype(BF16)
        act = act_scr[...]
        for c in range(n_d):
            issue_part(n_f + c)
            c0 = c * MOE_D_CHUNK
            y = (jnp.dot(act, wd_bf[wslot, :, c0:c0 + MOE_D_CHUNK], preferred_element_type=F32)
                 + bd_ref[0, expert, :, c0:c0 + MOE_D_CHUNK])
            for j in range(MOE_D_CHUNK // LANES):
                col = c0 + j * LANES
                yj = y[:, j * LANES:(j + 1) * LANES]
                if col < half:
                    hi_scr[:, col:col + LANES] = _bf16_bits(yj)
                else:
                    s = (col - half) // LANES
                    o_ref[pl.ds(s, tm, stride=ROW_SUBLANES), :] = hi_scr[:, col - half:col - half + LANES] | (_bf16_bits(yj) >> 16)

        @pl.when(i == last)
        def _():
            for t in range(1, MOE_AHEAD + 1):
                tile_wait((i + t) % MOE_SLOTS)

    @pl.when(i >= n_used)
    def _():
        o_ref[...] = jnp.zeros_like(o_ref)


def _group_tables(tile_expert, n_used):
    n = tile_expert.shape[0]
    idx = jnp.arange(n, dtype=jnp.int32)
    first = jnp.concatenate([jnp.ones((1,), bool), tile_expert[1:] != tile_expert[:-1]])
    par = (jnp.cumsum(first.astype(jnp.int32)) - 1) % 2
    pos = idx - lax.cummax(jnp.where(first, idx, 0))
    later = (idx[None, :] > idx[:, None]) & (idx[None, :] < n_used[0]) & (tile_expert[None, :] != tile_expert[:, None])
    nxt_idx = jnp.min(jnp.where(later, idx[None, :], n), axis=1)
    nxt = jnp.where(nxt_idx < n, tile_expert[jnp.minimum(nxt_idx, n - 1)], -1)
    pn = jnp.concatenate([jnp.zeros((1,), jnp.int32), pos[:-1] + 1])
    return par.astype(jnp.int32), pos.astype(jnp.int32), nxt.astype(jnp.int32), pn.astype(jnp.int32)


def _moe(layer, tile_expert, dest, pad_rows, n_used, h2p, wg, bg, wu, bu, wd, bd, tm):
    L, E, D, Fe = wg.shape
    n_tiles = tile_expert.shape[0]
    lo, hi = pad_rows
    assert tm % (Fe // MOE_F_CHUNK + D // MOE_D_CHUNK) == 0 and (D // 2) % MOE_D_CHUNK == 0
    par, pos, nxt, pn = _group_tables(tile_expert, n_used)
    bspec = lambda w: pl.BlockSpec((1, E, 1, w), lambda i, *_: (layer, 0, 0, 0), pipeline_mode=pl.Buffered(1))
    hbm = pl.BlockSpec(memory_space=pl.ANY)
    return pl.pallas_call(
        functools.partial(_moe_kernel, layer=layer),
        out_shape=jax.ShapeDtypeStruct((n_tiles * tm * ROW_SUBLANES, LANES), jnp.uint32),
        grid_spec=pltpu.PrefetchScalarGridSpec(
            num_scalar_prefetch=9,
            grid=(n_tiles,),
            in_specs=[hbm, hbm, bspec(Fe), hbm, bspec(Fe), hbm, bspec(D)],
            out_specs=pl.BlockSpec((tm * ROW_SUBLANES, LANES), lambda i, *_: (i, 0)),
            scratch_shapes=[pltpu.VMEM((MOE_SLOTS, tm * ROW_SUBLANES, LANES), jnp.uint32),
                            pltpu.VMEM((tm, Fe), BF16), pltpu.VMEM((tm, D // 2), jnp.uint32),
                            pltpu.VMEM((2, D, Fe), BF16), pltpu.VMEM((2, D, Fe), BF16),
                            pltpu.VMEM((2, Fe, D), BF16),
                            pltpu.VMEM((D // 2, Fe), F32), pltpu.VMEM((Fe // 2, D), F32),
                            pltpu.SMEM((n_tiles * tm,), jnp.int32),
                            pltpu.SemaphoreType.DMA((MOE_SLOTS,)), pltpu.SemaphoreType.DMA],
        ),
        compiler_params=_cparams(("arbitrary",)),
        name="moe_experts",
    )(tile_expert, dest, n_used, par, pos, nxt, pn, lo, hi, h2p, wg, bg.reshape(L, E, 1, Fe),
      wu, bu.reshape(L, E, 1, Fe), wd, bd.reshape(L, E, 1, D))


COMBINE_AHEAD = 2
COMBINE_SLOTS = COMBINE_AHEAD + 1
COMBINE_ROW_CHUNK = 8


def _combine_kernel(dest_ref, yp_hbm, gate_ref, x_ref, gpost_ref, gtf_ref, o_ref, buf, sem):
    i = pl.program_id(0)
    n = pl.num_programs(0)
    tc, D = x_ref.shape
    half = D // 2
    K = TOPK_EXPERTS

    def row_copies(tile_id, r, slot):
        base = tile_id * tc * K
        for kk in range(K):
            d = dest_ref[base + r * K + kk]
            src = yp_hbm.at[pl.ds(pl.multiple_of(d, ROW_SUBLANES), ROW_SUBLANES)]
            dst = buf.at[slot, kk, pl.ds(r * ROW_SUBLANES, ROW_SUBLANES)]
            pltpu.make_async_copy(src, dst, sem.at[slot]).start(priority=kk % 2)

    def tile_wait(slot):
        for kk in range(K):
            pltpu.make_async_copy(yp_hbm.at[pl.ds(0, tc * ROW_SUBLANES)], buf.at[slot, kk], sem.at[slot]).wait()

    @pl.when(i == 0)
    def _():
        for t in range(COMBINE_AHEAD):
            def issue(r, carry, t=t):
                row_copies(jnp.minimum(t, n - 1), r, t)
                return carry
            lax.fori_loop(0, tc, issue, 0)

    slot = i % COMBINE_SLOTS
    nslot = (i + COMBINE_AHEAD) % COMBINE_SLOTS
    ntile = jnp.minimum(i + COMBINE_AHEAD, n - 1)
    tile_wait(slot)
    RC = COMBINE_ROW_CHUNK
    for c in range(tc // RC):
        for r in range(c * RC, (c + 1) * RC):
            row_copies(ntile, r, nslot)
        rows = slice(c * RC, (c + 1) * RC)
        gates = gate_ref[rows, :]
        gk = [jnp.broadcast_to(gates[:, kk:kk + 1], (RC, LANES)) for kk in range(K)]
        his, los = [], []
        sq = jnp.zeros((RC, LANES), F32)
        for s in range(ROW_SUBLANES):
            hi = lo = None
            for kk in range(K):
                w = buf[slot, kk, pl.ds(c * RC * ROW_SUBLANES + s, RC, stride=ROW_SUBLANES), :]
                h, l = gk[kk] * _unpack_hi(w), gk[kk] * _unpack_lo(w)
                hi, lo = (h, l) if kk == 0 else (hi + h, lo + l)
            his.append(hi)
            los.append(lo)
            sq = sq + hi * hi + lo * lo
        inv = lax.rsqrt(jnp.sum(sq, axis=-1, keepdims=True) * (1.0 / D) + RMS_EPS)
        for s in range(ROW_SUBLANES):
            for col, y in ((s * LANES, his[s]), (half + s * LANES, los[s])):
                cs = slice(col, col + LANES)
                o_ref[rows, cs] = x_ref[rows, cs] + gtf_ref[0, :, cs] * ((y * inv) * gpost_ref[:, cs])

    @pl.when(i == n - 1)
    def _():
        for t in range(1, COMBINE_AHEAD + 1):
            tile_wait((i + t) % COMBINE_SLOTS)


def _combine(dest, yp, gates, x1, gpost, gtf, seq):
    T, D = x1.shape
    tc = COMBINE_TC
    per_b = seq // tc
    return pl.pallas_call(
        _combine_kernel,
        out_shape=jax.ShapeDtypeStruct((T, D), F32),
        grid_spec=pltpu.PrefetchScalarGridSpec(
            num_scalar_prefetch=1,
            grid=(T // tc,),
            in_specs=[pl.BlockSpec(memory_space=pl.ANY),
                      pl.BlockSpec((tc, LANES), lambda i, d: (i, 0)),
                      pl.BlockSpec((tc, D), lambda i, d: (i, 0)),
                      pl.BlockSpec((1, D), lambda i, d: (0, 0)),
                      pl.BlockSpec((1, 1, D), lambda i, d: (i // per_b, 0, 0))],
            out_specs=pl.BlockSpec((tc, D), lambda i, d: (i, 0)),
            scratch_shapes=[pltpu.VMEM((COMBINE_SLOTS, TOPK_EXPERTS, tc * ROW_SUBLANES, LANES), jnp.uint32),
                            pltpu.SemaphoreType.DMA((COMBINE_SLOTS,))],
        ),
        compiler_params=_cparams(("arbitrary",)),
        name="moe_combine",
    )(dest, yp, gates, x1, gpost, gtf)


def _dest_kernel(idx_ref, rank_ref, cnt_ref, dest_ref, *, tm_rows):
    tm = idx_ref.shape[0]
    tiles = jnp.floor((cnt_ref[0:1, :] + float(tm_rows - 1)) * (1.0 / tm_rows))
    r_i = lax.broadcasted_iota(jnp.int32, (LANES, LANES), 0)
    c_i = lax.broadcasted_iota(jnp.int32, (LANES, LANES), 1)
    before = jnp.where(r_i < c_i, 1.0, 0.0).astype(BF16)
    start = jnp.dot(jnp.broadcast_to(tiles, (8, LANES)).astype(BF16), before,
                    preferred_element_type=F32)[0:1] * float(tm_rows)
    lane = lax.broadcasted_iota(jnp.int32, (tm, LANES), 1)
    idx = idx_ref[...]
    rank = rank_ref[...]
    out = jnp.zeros((tm, LANES), jnp.int32)
    for kk in range(TOPK_EXPERTS):
        base = jnp.sum(jnp.where(lane == idx[:, kk:kk + 1], start, 0.0), axis=-1, keepdims=True)
        out = jnp.where(lane == kk, base.astype(jnp.int32) + rank[:, kk:kk + 1], out)
    dest_ref[...] = out


def _dest(idx, rank, cnt, tm_rows):
    T = idx.shape[0]
    tm = DEST_TM
    assert tm_rows & (tm_rows - 1) == 0
    row = pl.BlockSpec((tm, LANES), lambda i: (i, 0))
    return pl.pallas_call(
        functools.partial(_dest_kernel, tm_rows=tm_rows),
        out_shape=jax.ShapeDtypeStruct((T, LANES), jnp.int32),
        grid=(T // tm,),
        in_specs=[row, row, pl.BlockSpec((8, LANES), lambda i: (0, 0))],
        out_specs=row,
        compiler_params=_cparams(("parallel",)),
        name="route_dest",
    )(idx, rank, cnt)


INVERT_UNROLL = 8


def _invert_rows(dest_ref, lo_ref, hi_ref, out_ref):
    def fill(j, carry):
        out_ref[j] = 0
        return carry

    for e in range(lo_ref.shape[0]):
        lax.fori_loop(lo_ref[e], hi_ref[e], fill, 0)

    per = INVERT_UNROLL // TOPK_EXPERTS

    def body(b, carry):
        for u in range(INVERT_UNROLL):
            out_ref[dest_ref[b * INVERT_UNROLL + u]] = (b * per + u // TOPK_EXPERTS) * ROW_SUBLANES
        return carry

    lax.fori_loop(0, dest_ref.shape[0] // INVERT_UNROLL, body, 0)


def _routing_tables(idx, rank, cnt, n_experts, tm):
    T = idx.shape[0]
    n_tiles = (T * TOPK_EXPERTS) // tm + n_experts
    counts = cnt[0, :n_experts].astype(jnp.int32)
    padded = ((counts + tm - 1) // tm) * tm
    pad_end = jnp.cumsum(padded)
    tile_start = jnp.arange(n_tiles, dtype=jnp.int32) * tm
    tile_expert = jnp.minimum(jnp.sum((pad_end[None, :] <= tile_start[:, None]).astype(jnp.int32), axis=1),
                              n_experts - 1)
    n_used = pad_end[-1:] // tm
    pad_rows = (pad_end - padded + counts, pad_end)
    dest = _dest(idx, rank, cnt, tm)[:, :TOPK_EXPERTS].reshape(-1)
    return dest, pad_rows, tile_expert, n_used


def kernel(x, c, w_ada, b_ada, g_pre_mix, g_post_mix, g_pre_ffn, g_post_ffn, w_in, w_out, pool_w, pool_scale,
           conv_w, rel_bias, router_w, router_b, w_gate, b_gate, w_up, b_up, w_down, b_down):
    B, S, D = x.shape
    L = w_ada.shape[0]
    T = B * S
    E = router_w.shape[-1]
    attn_w = D // 2
    heads = attn_w // HEAD_DIM
    moe_tm = MOE_TM
    assert D == 2 * ROW_SUBLANES * LANES

    mod = _ada(c, w_ada, b_ada)
    w_in_bf = w_in.astype(BF16)
    x2 = x.reshape(T, D)
    for l in range(L):
        sh_m, sc_m, gt_m, sh_f, sc_f, gt_f = [mod[l, :, i * D:(i + 1) * D].reshape(B, 1, D) for i in range(N_MOD)]
        row = lambda v: v.reshape(1, D)

        qkv, rest = _inproj(l, x2, row(g_pre_mix[l]), sc_m, sh_m, w_in_bf, S, attn_w)
        att = _attention(qkv, rel_bias, B, S, heads)

        wo = w_out[l].astype(BF16)
        rw = jnp.zeros((D, LANES), F32).at[:, :E].set(router_w[l])
        rwh = rw.astype(BF16)
        rwl = (rw - rwh.astype(F32)).astype(BF16)
        rb = jnp.full((1, LANES), NEG_INF, F32).at[0, :E].set(router_b[l])
        x1, h2, idx, gates, rank, cnt = _outproj(
            att, rest, pool_w[l], pool_scale[l], conv_w[l], x2, wo[:attn_w], wo[attn_w:],
            row(g_post_mix[l]), gt_m, row(g_pre_ffn[l]), sc_f, sh_f,
            rwh, rwl, rb, S)

        dest, pad_rows, tile_expert, n_used = _routing_tables(idx, rank, cnt, E, moe_tm)
        yp = _moe(l, tile_expert, dest, pad_rows, n_used, h2, w_gate, b_gate, w_up, b_up, w_down, b_down, moe_tm)
        x2 = _combine(dest * ROW_SUBLANES, yp, gates, x1, row(g_post_ffn[l]), gt_f, S)
    return x2.reshape(B, S, D)
```

```python
import functools
import math

import numpy as np
import jax
import jax.numpy as jnp
from jax import lax
from jax.experimental import pallas as pl
from jax.experimental.pallas import tpu as pltpu

F32 = jnp.float32
BF16 = jnp.bfloat16

LANES = 128
HEAD_DIM = 128
MOBA_BLOCK = 256
MOBA_TOPK = 3
REL_BUCKETS = 32
REL_MAX_DISTANCE = 128
POOL_WINDOWS = (2, 4, 8, 16)
GROUP_DIM = 128
CONV_K = 3
TOPK_EXPERTS = 4
SWIGLU_LIMIT = 7.0
SWIGLU_ALPHA = 1.702
N_MOD = 6
RMS_EPS = 1e-6
NEG_INF = -1e30
ATTN_SCALE = HEAD_DIM ** -0.5
LOG2E = math.log2(math.e)

VMEM_LIMIT = 56 * 1024 * 1024

ADA_TN = 1024
OUTPROJ_TM = 512
MOE_TM = 256
COMBINE_TC = 256
DEST_TM = 2048


def _cparams(sem):
    return pltpu.CompilerParams(dimension_semantics=sem, vmem_limit_bytes=VMEM_LIMIT)


def _rms(x):
    return x * lax.rsqrt(jnp.mean(x * x, axis=-1, keepdims=True) + RMS_EPS)


ROW_SUBLANES = 8


def _bf16_bits(v):
    return lax.bitcast_convert_type(v.astype(BF16).astype(F32), jnp.uint32)


def _pack_words(hi, lo):
    return _bf16_bits(hi) | (_bf16_bits(lo) >> 16)


def _unpack_hi(w):
    return lax.bitcast_convert_type(w & jnp.uint32(0xFFFF0000), F32)


def _unpack_lo(w):
    return lax.bitcast_convert_type(w << 16, F32)


def _ada_kernel(c_ref, w_ref, b_ref, o_ref):
    c = c_ref[...]
    ca = (c * jax.nn.sigmoid(c)).astype(BF16)
    o_ref[0] = jnp.dot(ca, w_ref[0].astype(BF16), preferred_element_type=F32) + b_ref[0]


def _ada(c, w_ada, b_ada):
    L, D, N = w_ada.shape
    B = c.shape[0]
    tn = ADA_TN
    return pl.pallas_call(
        _ada_kernel,
        out_shape=jax.ShapeDtypeStruct((L, B, N), F32),
        grid=(L, N // tn),
        in_specs=[
            pl.BlockSpec((B, D), lambda l, j: (0, 0)),
            pl.BlockSpec((1, D, tn), lambda l, j: (l, 0, j)),
            pl.BlockSpec((1, 1, tn), lambda l, j: (l, 0, j)),
        ],
        out_specs=pl.BlockSpec((1, B, tn), lambda l, j: (l, 0, j)),
        compiler_params=_cparams(("parallel", "parallel")),
        name="ada_mod",
    )(c, w_ada, b_ada.reshape(L, 1, N))


INPROJ_TM = 1024
INPROJ_TN = 1024


def _inproj_kernel(x_ref, g_ref, sc_ref, sh_ref, w_ref, qkv_ref, rest_ref, h_scr, *, n_qkv):
    j = pl.program_id(1)

    @pl.when(j == 0)
    def _():
        h = _rms(x_ref[...]) * g_ref[...]
        h = h * (1.0 + sc_ref[0]) + sh_ref[0]
        h_scr[...] = h.astype(BF16)

    acc = jnp.dot(h_scr[...], w_ref[0], preferred_element_type=F32)

    @pl.when(j < n_qkv)
    def _():
        qkv_ref[...] = acc.astype(BF16)

    @pl.when(j >= n_qkv)
    def _():
        rest_ref[...] = acc


def _inproj(layer, x2, g, sc, sh, w_bf, seq, attn_w):
    T, D = x2.shape
    N = w_bf.shape[2]
    tm, tn = INPROJ_TM, INPROJ_TN
    n_qkv = 3 * attn_w // tn
    n_rest = (N - 3 * attn_w) // tn
    per_b = seq // tm
    return pl.pallas_call(
        functools.partial(_inproj_kernel, n_qkv=n_qkv),
        out_shape=(jax.ShapeDtypeStruct((T, 3 * attn_w), BF16),
                   jax.ShapeDtypeStruct((T, N - 3 * attn_w), F32)),
        grid=(T // tm, n_qkv + n_rest),
        in_specs=[
            pl.BlockSpec((tm, D), lambda i, j: (i, 0)),
            pl.BlockSpec((1, D), lambda i, j: (0, 0)),
            pl.BlockSpec((1, 1, D), lambda i, j: (i // per_b, 0, 0)),
            pl.BlockSpec((1, 1, D), lambda i, j: (i // per_b, 0, 0)),
            pl.BlockSpec((1, D, tn), lambda i, j: (layer, 0, j)),
        ],
        out_specs=(
            pl.BlockSpec((tm, tn), lambda i, j: (i, jnp.minimum(j, n_qkv - 1))),
            pl.BlockSpec((tm, tn), lambda i, j: (i, jnp.maximum(j - n_qkv, 0))),
        ),
        scratch_shapes=[pltpu.VMEM((tm, D), BF16)],
        compiler_params=_cparams(("parallel", "arbitrary")),
        name="mixer_in_proj",
    )(x2, g, sc, sh, w_bf)


def _rel_bucket_np(n):
    n = np.maximum(n, 0)
    max_exact = REL_BUCKETS // 2
    nf = np.maximum(n, max_exact).astype(np.float32)
    large = max_exact + (np.log(nf / np.float32(max_exact)) / np.float32(math.log(REL_MAX_DISTANCE / max_exact))
                         * np.float32(REL_BUCKETS - max_exact)).astype(np.int32)
    large = np.minimum(large, REL_BUCKETS - 1)
    return np.where(n < max_exact, n, large).astype(np.int32)


def _bucket_tables(seq):
    qi = np.arange(MOBA_BLOCK)[:, None]
    ki = np.arange(MOBA_BLOCK)[None, :]
    own = np.where(ki <= qi, _rel_bucket_np(qi - ki), -1)
    prev = _rel_bucket_np(qi - ki + MOBA_BLOCK)
    far = _rel_bucket_np(np.arange(MOBA_BLOCK + 1, max(seq, MOBA_BLOCK + 2)))
    assert np.all(far == far[0])
    return np.stack([own, prev]).astype(np.int32), int(far[0])


ATTN_HEADS_PER_STEP = 4


def _attn_kernel(tab_ref, q_ref, k_ref, v_ref, bkt_ref, o_ref, bias_scr, *, nblk, far_bucket):
    hg = pl.program_id(0)
    b = pl.program_id(1)
    L = MOBA_BLOCK
    G = ATTN_HEADS_PER_STEP

    @pl.when(b == 0)
    def _():
        for g in range(G):
            for m in range(2):
                bk = bkt_ref[m]
                acc = jnp.full((L, L), NEG_INF, F32)
                for r in range(REL_BUCKETS):
                    acc = jnp.where(bk == r, tab_ref[hg * G + g, r] * LOG2E, acc)
                bias_scr[g, m] = acc

    dn = (((1,), (1,)), ((), ()))
    lane = lax.broadcasted_iota(jnp.int32, (L, LANES), 1)
    heads = []
    for g in range(G):
        cols = slice(g * HEAD_DIM, (g + 1) * HEAD_DIM)
        q, k, v = q_ref[:, cols], k_ref[:, cols], v_ref[:, cols]
        kmean = jnp.mean(k.astype(F32).reshape(nblk, L, HEAD_DIM), axis=1)
        kmean = jnp.concatenate([kmean, jnp.zeros((LANES - nblk, HEAD_DIM), F32)], axis=0).astype(BF16)
        gate = lax.dot_general(q, kmean, dn, preferred_element_type=F32)
        heads.append((q, k, v, gate, tab_ref[hg * G + g, far_bucket] * LOG2E))

    for a, g in [(a, g) for a in range(nblk) for g in range(G)]:
        q, k, v, gate, far_bias = heads[g]
        cols = slice(g * HEAD_DIM, (g + 1) * HEAD_DIM)
        qa = q[a * L:(a + 1) * L]
        n = (a + 1) * L
        s = lax.dot_general(qa, k[:n], dn, preferred_element_type=F32) * (ATTN_SCALE * LOG2E)
        hide = None
        if a > MOBA_TOPK:
            ga = jnp.where(lane < a, gate[a * L:(a + 1) * L], NEG_INF)
            cnt = jnp.zeros((L, LANES), F32)
            for i in range(a):
                gi = ga[:, i:i + 1]
                ge = jnp.where(gi >= ga, 1.0, 0.0)
                gt = jnp.where(gi > ga, 1.0, 0.0)
                cnt = cnt + jnp.where(lane > i, ge, gt)
            hide = jnp.where((cnt < float(MOBA_TOPK)) & (lane < a), 0.0, NEG_INF)
            hide_far = hide + far_bias
        pieces = []
        for j in range(a + 1):
            sj = s[:, j * L:(j + 1) * L]
            if j == a:
                sj = sj + bias_scr[g, 0]
            elif j == a - 1:
                sj = sj + bias_scr[g, 1]
                if hide is not None:
                    sj = sj + hide[:, j:j + 1]
            else:
                sj = sj + (far_bias if hide is None else hide_far[:, j:j + 1])
            pieces.append(sj)
        s = pieces[0] if a == 0 else jnp.concatenate(pieces, axis=1)
        m = jnp.max(s, axis=-1, keepdims=True)
        p = jnp.exp2(s - m)
        l = jnp.sum(p, axis=-1, keepdims=True)
        o = jnp.dot(p.astype(BF16), v[:n], preferred_element_type=F32) / l
        o_ref[a * L:(a + 1) * L, cols] = o.astype(o_ref.dtype)


def _attention(qkv, rel_bias, batch, seq, heads):
    T = qkv.shape[0]
    nblk = seq // MOBA_BLOCK
    G = ATTN_HEADS_PER_STEP
    assert heads % G == 0
    ng = heads // G
    bkt, far_bucket = _bucket_tables(seq)
    table = rel_bias.T.astype(F32)
    return pl.pallas_call(
        functools.partial(_attn_kernel, nblk=nblk, far_bucket=far_bucket),
        out_shape=jax.ShapeDtypeStruct((T, heads * HEAD_DIM), BF16),
        grid=(ng, batch),
        in_specs=[
            pl.BlockSpec(memory_space=pltpu.SMEM),
            pl.BlockSpec((seq, G * HEAD_DIM), lambda h, b: (b, h)),
            pl.BlockSpec((seq, G * HEAD_DIM), lambda h, b: (b, ng + h)),
            pl.BlockSpec((seq, G * HEAD_DIM), lambda h, b: (b, 2 * ng + h)),
            pl.BlockSpec((2, MOBA_BLOCK, MOBA_BLOCK), lambda h, b: (0, 0, 0)),
        ],
        out_specs=pl.BlockSpec((seq, G * HEAD_DIM), lambda h, b: (b, h)),
        scratch_shapes=[pltpu.VMEM((G, 2, MOBA_BLOCK, MOBA_BLOCK), F32)],
        compiler_params=_cparams(("arbitrary", "arbitrary")),
        name="moba_attention",
    )(table, qkv, qkv, qkv, jnp.asarray(bkt))


MIX_HALO = 16


def _pool_conv_tile(u_ref, gb_ref, gc_ref, hc_ref, uh_ref, gch_ref, hch_ref, pw_ref, ps_ref, cw_ref, pc_scr,
                    seq_start, t0):
    tm = u_ref.shape[0]
    G = len(POOL_WINDOWS)
    H = MIX_HALO
    pos = lax.broadcasted_iota(jnp.int32, (tm, GROUP_DIM), 0) + t0

    def with_halo(cur_ref, halo_ref, cs):
        return jnp.concatenate([jnp.where(seq_start, 0.0, halo_ref[:, cs]), cur_ref[:, cs]], axis=0)

    def shift(x, k):
        return pltpu.roll(x, k, 0)

    for gi, w in enumerate(POOL_WINDOWS):
        cs = slice(gi * GROUP_DIM, (gi + 1) * GROUP_DIM)
        ug = with_halo(u_ref, uh_ref, cs)
        win = ug
        k = 1
        while k < w:
            win = win + shift(win, k)
            k *= 2
        count = jnp.minimum(pos + 1, w).astype(F32)
        d = win[H:] / count - ug[H:]
        y = jnp.dot(d.astype(BF16), pw_ref[gi].astype(BF16), preferred_element_type=F32)
        pc_scr[:, cs] = (y * ps_ref[:, cs]).astype(pc_scr.dtype)

        uu = with_halo(gc_ref, gch_ref, cs) * with_halo(hc_ref, hch_ref, cs)
        conv = (cw_ref[0:1, cs] * shift(uu, 2) + cw_ref[1:2, cs] * shift(uu, 1)
                + cw_ref[2:3, cs] * uu)[H:]
        oc = slice((G + gi) * GROUP_DIM, (G + gi + 1) * GROUP_DIM)
        pc_scr[:, oc] = (gb_ref[:, cs] * conv).astype(pc_scr.dtype)


def _outproj_kernel(att_ref, u_ref, gb_ref, gc_ref, hc_ref, uh_ref, gch_ref, hch_ref, pw_ref, ps_ref, cw_ref,
                    x_ref, wa_ref, wp_ref, gpost_ref, gtm_ref, gpre_ref, scf_ref, shf_ref,
                    rwh_ref, rwl_ref, rb_ref,
                    x1_ref, h2_ref, idx_ref, gate_ref, rank_ref, cnt_ref, pc_scr, carry_scr, *, per_seq):
    i = pl.program_id(0)
    tm = x_ref.shape[0]

    @pl.when(i == 0)
    def _():
        carry_scr[...] = jnp.zeros_like(carry_scr)

    _pool_conv_tile(u_ref, gb_ref, gc_ref, hc_ref, uh_ref, gch_ref, hch_ref, pw_ref, ps_ref, cw_ref, pc_scr,
                    seq_start=i % per_seq == 0, t0=(i % per_seq) * tm)
    mix = (jnp.dot(att_ref[...], wa_ref[...], preferred_element_type=F32)
           + jnp.dot(pc_scr[...], wp_ref[...], preferred_element_type=F32))
    x1 = x_ref[...] + gtm_ref[0] * (_rms(mix) * gpost_ref[...])
    x1_ref[...] = x1
    h2 = (_rms(x1) * gpre_ref[...]) * (1.0 + scf_ref[0]) + shf_ref[0]
    half = h2.shape[1] // 2
    for s in range(ROW_SUBLANES):
        h2_ref[pl.ds(s, tm, stride=ROW_SUBLANES), :] = _pack_words(h2[:, s * LANES:(s + 1) * LANES],
                                      h2[:, half + s * LANES:half + (s + 1) * LANES])

    hi = h2.astype(BF16)
    lo = (h2 - hi.astype(F32)).astype(BF16)
    r2 = jnp.dot(hi, jnp.concatenate([rwh_ref[...], rwl_ref[...]], axis=1), preferred_element_type=F32)
    logits = (r2[:, :LANES] + r2[:, LANES:]
              + jnp.dot(lo, rwh_ref[...], preferred_element_type=F32)) + rb_ref[...]

    lane = lax.broadcasted_iota(jnp.int32, (tm, LANES), 1)
    work = logits
    mem = jnp.zeros((tm, LANES), F32)
    vals, hots = [], []
    idx_out = jnp.zeros((tm, LANES), jnp.int32)
    for kk in range(TOPK_EXPERTS):
        m = jnp.max(work, axis=-1, keepdims=True)
        ik = jnp.min(jnp.where(work == m, lane, LANES), axis=-1, keepdims=True)
        hot = lane == ik
        vals.append(m)
        hots.append(hot)
        idx_out = jnp.where(lane == kk, ik, idx_out)
        mem = jnp.where(hot, 1.0, mem)
        work = jnp.where(hot, -jnp.inf, work)
    idx_ref[...] = idx_out

    es = [jnp.exp(vk - vals[0]) for vk in vals]
    denom = es[0]
    for e in es[1:]:
        denom = denom + e
    gates = jnp.zeros((tm, LANES), F32)
    for kk in range(TOPK_EXPERTS):
        gates = jnp.where(lane == kk, es[kk] / denom, gates)
    gate_ref[...] = gates

    r_i = lax.broadcasted_iota(jnp.int32, (tm, tm), 0)
    c_i = lax.broadcasted_iota(jnp.int32, (tm, tm), 1)
    tri = jnp.where(c_i < r_i, 1.0, 0.0).astype(BF16)
    before = jnp.dot(tri, mem.astype(BF16), preferred_element_type=F32) + carry_scr[0:1, :]
    ranks = jnp.zeros((tm, LANES), jnp.int32)
    for kk in range(TOPK_EXPERTS):
        rk = jnp.sum(jnp.where(hots[kk], before, 0.0), axis=-1, keepdims=True)
        ranks = jnp.where(lane == kk, rk.astype(jnp.int32), ranks)
    rank_ref[...] = ranks
    carry = carry_scr[...] + jnp.sum(mem, axis=0, keepdims=True)
    carry_scr[...] = carry
    cnt_ref[...] = carry


def _outproj(att, rest, pool_w, pool_scale, conv_w, x2, wa, wp, gpost, gtm, gpre, scf, shf, rwh, rwl, rb, seq):
    T, D = x2.shape
    tm = OUTPROJ_TM
    per_b = seq // tm
    W = len(POOL_WINDOWS) * GROUP_DIM
    row = lambda w: pl.BlockSpec((tm, w), lambda i: (i, 0))
    cur = lambda c: pl.BlockSpec((tm, W), lambda i: (i, c))
    halo = lambda c: pl.BlockSpec((MIX_HALO, W), lambda i: (jnp.maximum(i * (tm // MIX_HALO) - 1, 0), c))
    const = lambda shape: pl.BlockSpec(shape, lambda i: tuple(0 for _ in shape), pipeline_mode=pl.Buffered(1))
    perb = pl.BlockSpec((1, 1, D), lambda i: (i // per_b, 0, 0))
    return pl.pallas_call(
        functools.partial(_outproj_kernel, per_seq=per_b),
        out_shape=(jax.ShapeDtypeStruct((T, D), F32), jax.ShapeDtypeStruct((T * ROW_SUBLANES, LANES), jnp.uint32),
                   jax.ShapeDtypeStruct((T, LANES), jnp.int32), jax.ShapeDtypeStruct((T, LANES), F32),
                   jax.ShapeDtypeStruct((T, LANES), jnp.int32), jax.ShapeDtypeStruct((8, LANES), F32)),
        grid=(T // tm,),
        in_specs=[row(att.shape[1]), cur(0), cur(1), cur(2), cur(3), halo(0), halo(2), halo(3),
                  const(pool_w.shape), const((1, W)), const((CONV_K, W)),
                  row(D), const(wa.shape), const(wp.shape),
                  const((1, D)), perb, const((1, D)), perb, perb,
                  const(rwh.shape), const(rwl.shape), const((1, LANES))],
        out_specs=(row(D), pl.BlockSpec((tm * ROW_SUBLANES, LANES), lambda i: (i, 0)),
                   row(LANES), row(LANES), row(LANES), pl.BlockSpec((8, LANES), lambda i: (0, 0))),
        scratch_shapes=[pltpu.VMEM((tm, 2 * W), BF16), pltpu.VMEM((8, LANES), F32)],
        compiler_params=_cparams(("arbitrary",)),
        name="out_proj_router",
    )(att, rest, rest, rest, rest, rest, rest, rest, pool_w, pool_scale.reshape(1, W), conv_w,
      x2, wa, wp, gpost, gtm, gpre, scf, shf, rwh, rwl, rb)


MOE_F_CHUNK = 256
MOE_D_CHUNK = 512
MOE_AHEAD = 2
MOE_SLOTS = MOE_AHEAD + 1


MOE_W_PHASES = 6


def _moe_kernel(te_ref, dest_ref, nu_ref, par_ref, pos_ref, nxt_ref, pn_ref, lo_ref, hi_ref,
                h_hbm, wg_hbm, bg_ref, wu_hbm, bu_ref, wd_hbm, bd_ref, o_ref,
                xbuf, act_scr, hi_scr, wg_bf, wu_bf, wd_bf, stg_a, stg_b, rt_ref, sem, wsem, *, layer):
    i = pl.program_id(0)
    tm = act_scr.shape[0]
    Fe = act_scr.shape[1]
    D = wd_bf.shape[-1]
    half = D // 2
    n_used = nu_ref[0]
    last = n_used - 1

    def piece(ph, expert):
        if ph < 4:
            src_ref, dst = (wg_hbm, stg_a) if ph < 2 else (wu_hbm, stg_a)
            rows = D // 2
        else:
            src_ref, dst, rows = wd_hbm, stg_b, Fe // 2
        r0 = (ph % 2) * rows
        return pltpu.make_async_copy(src_ref.at[layer, expert, pl.ds(r0, rows), :], dst, wsem)

    def round_piece(ph, slot):
        if ph < 4:
            rows = D // 2
            dst = wg_bf if ph < 2 else wu_bf
            dst[slot, (ph % 2) * rows:(ph % 2 + 1) * rows, :] = stg_a[...].astype(BF16)
        else:
            rows = Fe // 2
            wd_bf[slot, (ph % 2) * rows:(ph % 2 + 1) * rows, :] = stg_b[...].astype(BF16)

    def row_copy(tile_id, r, slot):
        tok8 = rt_ref[tile_id * tm + r]
        src = h_hbm.at[pl.ds(pl.multiple_of(tok8, ROW_SUBLANES), ROW_SUBLANES)]
        return pltpu.make_async_copy(src, xbuf.at[slot, pl.ds(r * ROW_SUBLANES, ROW_SUBLANES)], sem.at[slot])

    def tile_wait(slot):
        pltpu.make_async_copy(h_hbm.at[pl.ds(0, tm * ROW_SUBLANES)], xbuf.at[slot], sem.at[slot]).wait()

    @pl.when(i == 0)
    def _():
        _invert_rows(dest_ref, lo_ref, hi_ref, rt_ref)
        for t in range(MOE_AHEAD):
            def issue(r, carry, t=t):
                row_copy(jnp.minimum(t, last), r, t).start()
                return carry
            lax.fori_loop(0, tm, issue, 0)

    @pl.when(i < n_used)
    def _():
        slot = i % MOE_SLOTS
        nslot = (i + MOE_AHEAD) % MOE_SLOTS
        ntile = jnp.minimum(i + MOE_AHEAD, last)
        n_f, n_d = Fe // MOE_F_CHUNK, D // MOE_D_CHUNK
        per = tm // (n_f + n_d)

        def issue_part(p):
            for r in range(p * per, (p + 1) * per):
                row_copy(ntile, r, nslot).start()

        expert = te_ref[i]
        wslot = par_ref[i]
        k = pos_ref[i]
        nxt = nxt_ref[i]

        @pl.when(i == 0)
        def _():
            for ph in range(MOE_W_PHASES):
                piece(ph, expert).start()
                piece(ph, expert).wait()
                round_piece(ph, wslot)

        @pl.when((k == 0) & (i > 0))
        def _():
            done = jnp.minimum(pn_ref[i] - 1, MOE_W_PHASES)
            for ph in range(MOE_W_PHASES):
                @pl.when(ph >= done)
                def _(ph=ph):
                    @pl.when(ph > done)
                    def _():
                        piece(ph, expert).start()
                    piece(ph, expert).wait()
                    round_piece(ph, wslot)

        for ph in range(MOE_W_PHASES):
            @pl.when((k == ph + 1) & (nxt >= 0))
            def _(ph=ph):
                piece(ph, nxt).wait()
                round_piece(ph, 1 - wslot)

        for ph in range(MOE_W_PHASES):
            @pl.when((k == ph) & (nxt >= 0))
            def _(ph=ph):
                piece(ph, nxt).start(priority=1)

        tile_wait(slot)
        words = [xbuf[slot, pl.ds(s, tm, stride=ROW_SUBLANES), :] for s in range(ROW_SUBLANES)]
        x = jnp.concatenate([_unpack_hi(w).astype(BF16) for w in words]
                            + [_unpack_lo(w).astype(BF16) for w in words], axis=1)
        for c in range(n_f):
            issue_part(c)
            cs = slice(c * MOE_F_CHUNK, (c + 1) * MOE_F_CHUNK)
            g = jnp.dot(x, wg_bf[wslot, :, cs], preferred_element_type=F32) + bg_ref[0, expert, :, cs]
            u = jnp.dot(x, wu_bf[wslot, :, cs], preferred_element_type=F32) + bu_ref[0, expert, :, cs]
            g = jnp.minimum(g, SWIGLU_LIMIT)
            u = jnp.clip(u, -SWIGLU_LIMIT, SWIGLU_LIMIT)
            act_scr[:, cs] = (g * jax.nn.sigmoid(SWIGLU_ALPHA * g) * (u + 1.0)).astype(BF16)
        act = act_scr[...]
        for c in range(n_d):
            issue_part(n_f + c)
            c0 = c * MOE_D_CHUNK
            y = (jnp.dot(act, wd_bf[wslot, :, c0:c0 + MOE_D_CHUNK], preferred_element_type=F32)
                 + bd_ref[0, expert, :, c0:c0 + MOE_D_CHUNK])
            for j in range(MOE_D_CHUNK // LANES):
                col = c0 + j * LANES
                yj = y[:, j * LANES:(j + 1) * LANES]
                if col < half:
                    hi_scr[:, col:col + LANES] = _bf16_bits(yj)
                else:
                    s = (col - half) // LANES
                    o_ref[pl.ds(s, tm, stride=ROW_SUBLANES), :] = hi_scr[:, col - half:col - half + LANES] | (_bf16_bits(yj) >> 16)

        @pl.when(i == last)
        def _():
            for t in range(1, MOE_AHEAD + 1):
                tile_wait((i + t) % MOE_SLOTS)

    @pl.when(i >= n_used)
    def _():
        o_ref[...] = jnp.zeros_like(o_ref)


def _group_tables(tile_expert, n_used):
    n = tile_expert.shape[0]
    idx = jnp.arange(n, dtype=jnp.int32)
    first = jnp.concatenate([jnp.ones((1,), bool), tile_expert[1:] != tile_expert[:-1]])
    par = (jnp.cumsum(first.astype(jnp.int32)) - 1) % 2
    pos = idx - lax.cummax(jnp.where(first, idx, 0))
    later = (idx[None, :] > idx[:, None]) & (idx[None, :] < n_used[0]) & (tile_expert[None, :] != tile_expert[:, None])
    nxt_idx = jnp.min(jnp.where(later, idx[None, :], n), axis=1)
    nxt = jnp.where(nxt_idx < n, tile_expert[jnp.minimum(nxt_idx, n - 1)], -1)
    pn = jnp.concatenate([jnp.zeros((1,), jnp.int32), pos[:-1] + 1])
    return par.astype(jnp.int32), pos.astype(jnp.int32), nxt.astype(jnp.int32), pn.astype(jnp.int32)


def _moe(layer, tile_expert, dest, pad_rows, n_used, h2p, wg, bg, wu, bu, wd, bd, tm):
    L, E, D, Fe = wg.shape
    n_tiles = tile_expert.shape[0]
    lo, hi = pad_rows
    assert tm % (Fe // MOE_F_CHUNK + D // MOE_D_CHUNK) == 0 and (D // 2) % MOE_D_CHUNK == 0
    par, pos, nxt, pn = _group_tables(tile_expert, n_used)
    bspec = lambda w: pl.BlockSpec((1, E, 1, w), lambda i, *_: (layer, 0, 0, 0), pipeline_mode=pl.Buffered(1))
    hbm = pl.BlockSpec(memory_space=pl.ANY)
    return pl.pallas_call(
        functools.partial(_moe_kernel, layer=layer),
        out_shape=jax.ShapeDtypeStruct((n_tiles * tm * ROW_SUBLANES, LANES), jnp.uint32),
        grid_spec=pltpu.PrefetchScalarGridSpec(
            num_scalar_prefetch=9,
            grid=(n_tiles,),
            in_specs=[hbm, hbm, bspec(Fe), hbm, bspec(Fe), hbm, bspec(D)],
            out_specs=pl.BlockSpec((tm * ROW_SUBLANES, LANES), lambda i, *_: (i, 0)),
            scratch_shapes=[pltpu.VMEM((MOE_SLOTS, tm * ROW_SUBLANES, LANES), jnp.uint32),
                            pltpu.VMEM((tm, Fe), BF16), pltpu.VMEM((tm, D // 2), jnp.uint32),
                            pltpu.VMEM((2, D, Fe), BF16), pltpu.VMEM((2, D, Fe), BF16),
                            pltpu.VMEM((2, Fe, D), BF16),
                            pltpu.VMEM((D // 2, Fe), F32), pltpu.VMEM((Fe // 2, D), F32),
                            pltpu.SMEM((n_tiles * tm,), jnp.int32),
                            pltpu.SemaphoreType.DMA((MOE_SLOTS,)), pltpu.SemaphoreType.DMA],
        ),
        compiler_params=_cparams(("arbitrary",)),
        name="moe_experts",
    )(tile_expert, dest, n_used, par, pos, nxt, pn, lo, hi, h2p, wg, bg.reshape(L, E, 1, Fe),
      wu, bu.reshape(L, E, 1, Fe), wd, bd.reshape(L, E, 1, D))


COMBINE_AHEAD = 2
COMBINE_SLOTS = COMBINE_AHEAD + 1
COMBINE_ROW_CHUNK = 8


def _combine_kernel(dest_ref, yp_hbm, gate_ref, x_ref, gpost_ref, gtf_ref, o_ref, buf, sem):
    i = pl.program_id(0)
    n = pl.num_programs(0)
    tc, D = x_ref.shape
    half = D // 2
    K = TOPK_EXPERTS

    def row_copies(tile_id, r, slot):
        base = tile_id * tc * K
        for kk in range(K):
            d = dest_ref[base + r * K + kk]
            src = yp_hbm.at[pl.ds(pl.multiple_of(d, ROW_SUBLANES), ROW_SUBLANES)]
            dst = buf.at[slot, kk, pl.ds(r * ROW_SUBLANES, ROW_SUBLANES)]
            pltpu.make_async_copy(src, dst, sem.at[slot]).start(priority=kk % 2)

    def tile_wait(slot):
        for kk in range(K):
            pltpu.make_async_copy(yp_hbm.at[pl.ds(0, tc * ROW_SUBLANES)], buf.at[slot, kk], sem.at[slot]).wait()

    @pl.when(i == 0)
    def _():
        for t in range(COMBINE_AHEAD):
            def issue(r, carry, t=t):
                row_copies(jnp.minimum(t, n - 1), r, t)
                return carry
            lax.fori_loop(0, tc, issue, 0)

    slot = i % COMBINE_SLOTS
    nslot = (i + COMBINE_AHEAD) % COMBINE_SLOTS
    ntile = jnp.minimum(i + COMBINE_AHEAD, n - 1)
    tile_wait(slot)
    RC = COMBINE_ROW_CHUNK
    for c in range(tc // RC):
        for r in range(c * RC, (c + 1) * RC):
            row_copies(ntile, r, nslot)
        rows = slice(c * RC, (c + 1) * RC)
        gates = gate_ref[rows, :]
        gk = [jnp.broadcast_to(gates[:, kk:kk + 1], (RC, LANES)) for kk in range(K)]
        his, los = [], []
        sq = jnp.zeros((RC, LANES), F32)
        for s in range(ROW_SUBLANES):
            hi = lo = None
            for kk in range(K):
                w = buf[slot, kk, pl.ds(c * RC * ROW_SUBLANES + s, RC, stride=ROW_SUBLANES), :]
                h, l = gk[kk] * _unpack_hi(w), gk[kk] * _unpack_lo(w)
                hi, lo = (h, l) if kk == 0 else (hi + h, lo + l)
            his.append(hi)
            los.append(lo)
            sq = sq + hi * hi + lo * lo
        inv = lax.rsqrt(jnp.sum(sq, axis=-1, keepdims=True) * (1.0 / D) + RMS_EPS)
        for s in range(ROW_SUBLANES):
            for col, y in ((s * LANES, his[s]), (half + s * LANES, los[s])):
                cs = slice(col, col + LANES)
                o_ref[rows, cs] = x_ref[rows, cs] + gtf_ref[0, :, cs] * ((y * inv) * gpost_ref[:, cs])

    @pl.when(i == n - 1)
    def _():
        for t in range(1, COMBINE_AHEAD + 1):
            tile_wait((i + t) % COMBINE_SLOTS)


def _combine(dest, yp, gates, x1, gpost, gtf, seq):
    T, D = x1.shape
    tc = COMBINE_TC
    per_b = seq // tc
    return pl.pallas_call(
        _combine_kernel,
        out_shape=jax.ShapeDtypeStruct((T, D), F32),
        grid_spec=pltpu.PrefetchScalarGridSpec(
            num_scalar_prefetch=1,
            grid=(T // tc,),
            in_specs=[pl.BlockSpec(memory_space=pl.ANY),
                      pl.BlockSpec((tc, LANES), lambda i, d: (i, 0)),
                      pl.BlockSpec((tc, D), lambda i, d: (i, 0)),
                      pl.BlockSpec((1, D), lambda i, d: (0, 0)),
                      pl.BlockSpec((1, 1, D), lambda i, d: (i // per_b, 0, 0))],
            out_specs=pl.BlockSpec((tc, D), lambda i, d: (i, 0)),
            scratch_shapes=[pltpu.VMEM((COMBINE_SLOTS, TOPK_EXPERTS, tc * ROW_SUBLANES, LANES), jnp.uint32),
                            pltpu.SemaphoreType.DMA((COMBINE_SLOTS,))],
        ),
        compiler_params=_cparams(("arbitrary",)),
        name="moe_combine",
    )(dest, yp, gates, x1, gpost, gtf)


def _dest_kernel(idx_ref, rank_ref, cnt_ref, dest_ref, *, tm_rows):
    tm = idx_ref.shape[0]
    tiles = jnp.floor((cnt_ref[0:1, :] + float(tm_rows - 1)) * (1.0 / tm_rows))
    r_i = lax.broadcasted_iota(jnp.int32, (LANES, LANES), 0)
    c_i = lax.broadcasted_iota(jnp.int32, (LANES, LANES), 1)
    before = jnp.where(r_i < c_i, 1.0, 0.0).astype(BF16)
    start = jnp.dot(jnp.broadcast_to(tiles, (8, LANES)).astype(BF16), before,
                    preferred_element_type=F32)[0:1] * float(tm_rows)
    lane = lax.broadcasted_iota(jnp.int32, (tm, LANES), 1)
    idx = idx_ref[...]
    rank = rank_ref[...]
    out = jnp.zeros((tm, LANES), jnp.int32)
    for kk in range(TOPK_EXPERTS):
        base = jnp.sum(jnp.where(lane == idx[:, kk:kk + 1], start, 0.0), axis=-1, keepdims=True)
        out = jnp.where(lane == kk, base.astype(jnp.int32) + rank[:, kk:kk + 1], out)
    dest_ref[...] = out


def _dest(idx, rank, cnt, tm_rows):
    T = idx.shape[0]
    tm = DEST_TM
    assert tm_rows & (tm_rows - 1) == 0
    row = pl.BlockSpec((tm, LANES), lambda i: (i, 0))
    return pl.pallas_call(
        functools.partial(_dest_kernel, tm_rows=tm_rows),
        out_shape=jax.ShapeDtypeStruct((T, LANES), jnp.int32),
        grid=(T // tm,),
        in_specs=[row, row, pl.BlockSpec((8, LANES), lambda i: (0, 0))],
        out_specs=row,
        compiler_params=_cparams(("parallel",)),
        name="route_dest",
    )(idx, rank, cnt)


INVERT_UNROLL = 8


def _invert_rows(dest_ref, lo_ref, hi_ref, out_ref):
    def fill(j, carry):
        out_ref[j] = 0
        return carry

    for e in range(lo_ref.shape[0]):
        lax.fori_loop(lo_ref[e], hi_ref[e], fill, 0)

    per = INVERT_UNROLL // TOPK_EXPERTS

    def body(b, carry):
        for u in range(INVERT_UNROLL):
            out_ref[dest_ref[b * INVERT_UNROLL + u]] = (b * per + u // TOPK_EXPERTS) * ROW_SUBLANES
        return carry

    lax.fori_loop(0, dest_ref.shape[0] // INVERT_UNROLL, body, 0)


def _routing_tables(idx, rank, cnt, n_experts, tm):
    T = idx.shape[0]
    n_tiles = (T * TOPK_EXPERTS) // tm + n_experts
    counts = cnt[0, :n_experts].astype(jnp.int32)
    padded = ((counts + tm - 1) // tm) * tm
    pad_end = jnp.cumsum(padded)
    tile_start = jnp.arange(n_tiles, dtype=jnp.int32) * tm
    tile_expert = jnp.minimum(jnp.sum((pad_end[None, :] <= tile_start[:, None]).astype(jnp.int32), axis=1),
                              n_experts - 1)
    n_used = pad_end[-1:] // tm
    pad_rows = (pad_end - padded + counts, pad_end)
    dest = _dest(idx, rank, cnt, tm)[:, :TOPK_EXPERTS].reshape(-1)
    return dest, pad_rows, tile_expert, n_used


def kernel(x, c, w_ada, b_ada, g_pre_mix, g_post_mix, g_pre_ffn, g_post_ffn, w_in, w_out, pool_w, pool_scale,
           conv_w, rel_bias, router_w, router_b, w_gate, b_gate, w_up, b_up, w_down, b_down):
    B, S, D = x.shape
    L = w_ada.shape[0]
    T = B * S
    E = router_w.shape[-1]
    attn_w = D // 2
    heads = attn_w // HEAD_DIM
    moe_tm = MOE_TM
    assert D == 2 * ROW_SUBLANES * LANES

    mod = _ada(c, w_ada, b_ada)
    w_in_bf = w_in.astype(BF16)
    x2 = x.reshape(T, D)
    for l in range(L):
        sh_m, sc_m, gt_m, sh_f, sc_f, gt_f = [mod[l, :, i * D:(i + 1) * D].reshape(B, 1, D) for i in range(N_MOD)]
        row = lambda v: v.reshape(1, D)

        qkv, rest = _inproj(l, x2, row(g_pre_mix[l]), sc_m, sh_m, w_in_bf, S, attn_w)
        att = _attention(qkv, rel_bias, B, S, heads)

        wo = w_out[l].astype(BF16)
        rw = jnp.zeros((D, LANES), F32).at[:, :E].set(router_w[l])
        rwh = rw.astype(BF16)
        rwl = (rw - rwh.astype(F32)).astype(BF16)
        rb = jnp.full((1, LANES), NEG_INF, F32).at[0, :E].set(router_b[l])
        x1, h2, idx, gates, rank, cnt = _outproj(
            att, rest, pool_w[l], pool_scale[l], conv_w[l], x2, wo[:attn_w], wo[attn_w:],
            row(g_post_mix[l]), gt_m, row(g_pre_ffn[l]), sc_f, sh_f,
            rwh, rwl, rb, S)

        dest, pad_rows, tile_expert, n_used = _routing_tables(idx, rank, cnt, E, moe_tm)
        yp = _moe(l, tile_expert, dest, pad_rows, n_used, h2, w_gate, b_gate, w_up, b_up, w_down, b_down, moe_tm)
        x2 = _combine(dest * ROW_SUBLANES, yp, gates, x1, row(g_post_ffn[l]), gt_f, S)
    return x2.reshape(B, S, D)
```
